```python
import math
import jax, jax.numpy as jnp
from jax import lax
import numpy as np

D_MODEL = 1024
BATCH = 8
SEQ = 4096
DEPTH = 4

HEAD_DIM = 64
POOL_WINDOWS = (2, 4, 8, 16)
POOL_GROUP_DIM = 64
POOL_WIDTH = len(POOL_WINDOWS) * POOL_GROUP_DIM
DIL_CONFIGS = ((128, 1), (512, 4), (2048, 16))
DIL_HEADS = 4
N_DIL_GROUPS = len(DIL_CONFIGS)
DIL_GROUP_WIDTH = DIL_HEADS * HEAD_DIM
DIL_WIDTH = N_DIL_GROUPS * DIL_GROUP_WIDTH
BAND_BLOCK = 128
EVEN_IN = POOL_WIDTH + 3 * DIL_WIDTH
EVEN_OUT = POOL_WIDTH + DIL_GROUP_WIDTH
FOX_HEADS = D_MODEL // HEAD_DIM
FOX_WIDTH = FOX_HEADS * HEAD_DIM
ODD_IN = 3 * FOX_WIDTH + FOX_HEADS
Q_BLOCK = 128
ROPE_THETA = 500000.0
ROPE_DIM = HEAD_DIM // 4
D_FF = 2816
N_EXPERTS = 8
TOP_K = 2
D_FF_EXPERT = 2816
DEEPNORM_ALPHA = (2 * DEPTH) ** 0.25
DEEPNORM_BETA = (8 * DEPTH) ** -0.25
LN_EPS = 1e-5
NEG = -1e30
N_EVEN = (DEPTH + 1) // 2
N_ODD = DEPTH // 2

kernel_name = "hybrid_pool_dilated_fox_moe_trunk"


def layer_norm(x, g, b):
    xf = x.astype(jnp.float32)
    mu = jnp.mean(xf, axis=-1, keepdims=True)
    var = jnp.mean(jnp.square(xf - mu), axis=-1, keepdims=True)
    return ((xf - mu) * lax.rsqrt(var + LN_EPS) * g.astype(jnp.float32) + b.astype(jnp.float32)).astype(x.dtype)


def rope_tables(seq):
    pos = jnp.arange(seq, dtype=jnp.float32)
    inv = ROPE_THETA ** (-jnp.arange(0, ROPE_DIM, 2, dtype=jnp.float32) / ROPE_DIM)
    ang = pos[:, None] * inv[None, :]
    return jnp.cos(ang), jnp.sin(ang)


def partial_rope(x, cos, sin):
    xr = x[..., :ROPE_DIM].astype(jnp.float32)
    x1, x2 = xr[..., :ROPE_DIM // 2], xr[..., ROPE_DIM // 2:]
    c = cos[None, :, None, :]
    s = sin[None, :, None, :]
    rot = jnp.concatenate([x1 * c - x2 * s, x1 * s + x2 * c], axis=-1).astype(x.dtype)
    return jnp.concatenate([rot, x[..., ROPE_DIM:]], axis=-1)


def multiscale_pool(u, w_pool, pool_scale):
    B, S, _ = u.shape
    ug = u.reshape(B, S, len(POOL_WINDOWS), POOL_GROUP_DIM).astype(jnp.float32)
    cs = jnp.pad(jnp.cumsum(ug, axis=1), ((0, 0), (1, 0), (0, 0), (0, 0)))
    t = jnp.arange(S, dtype=jnp.float32)
    pooled = []
    for g, w in enumerate(POOL_WINDOWS):
        hi = cs[:, 1:, g]
        lo = jnp.pad(cs[:, :S + 1 - w, g], ((0, 0), (w - 1, 0), (0, 0)))
        count = jnp.minimum(t + 1.0, float(w))[None, :, None]
        pooled.append((hi - lo) / count - ug[:, :, g])
    pooled = jnp.stack(pooled, axis=2)
    out = jnp.einsum('bsgc,gcd->bsgd', pooled, w_pool.astype(jnp.float32))
    return (out.reshape(B, S, POOL_WIDTH) * pool_scale.astype(jnp.float32)).astype(u.dtype)


def banded_causal_attention(q, k, v, span):
    assert span <= BAND_BLOCK
    N, L, H, Dh = q.shape
    Qb = BAND_BLOCK
    nb = -(-L // Qb)
    Lp = nb * Qb
    qb = jnp.pad(q, ((0, 0), (0, Lp - L), (0, 0), (0, 0))).reshape(N, nb, Qb, H, Dh)
    padk = ((0, 0), (Qb, Lp - L), (0, 0), (0, 0))
    kp = jnp.pad(k, padk).reshape(N, nb + 1, Qb, H, Dh)
    vp = jnp.pad(v, padk).reshape(N, nb + 1, Qb, H, Dh)
    kb = jnp.concatenate([kp[:, :-1], kp[:, 1:]], axis=2)
    vb = jnp.concatenate([vp[:, :-1], vp[:, 1:]], axis=2)
    s = jnp.einsum('nbqhd,nbkhd->nbhqk', qb, kb, preferred_element_type=jnp.float32) * (HEAD_DIM ** -0.5)
    qi = jnp.arange(Qb)[:, None]
    kj = jnp.arange(2 * Qb)[None, :]
    dist = qi + Qb - kj
    kpos = jnp.arange(nb)[:, None, None] * Qb - Qb + kj[None]
    mask = (dist >= 0)[None] & (dist <= span)[None] & (kpos >= 0)
    s = jnp.where(mask[None, :, None], s, NEG)
    m = jnp.max(s, axis=-1, keepdims=True)
    p = jnp.exp(s - m)
    den = jnp.sum(p, axis=-1)
    o = jnp.einsum('nbhqk,nbkhd->nbqhd', p, vb.astype(jnp.float32))
    o = o / jnp.transpose(den, (0, 1, 3, 2))[..., None]
    lse = jnp.transpose(m[..., 0] + jnp.log(den), (0, 1, 3, 2))
    return o.reshape(N, Lp, H, Dh)[:, :L], lse.reshape(N, Lp, H)[:, :L]


def dilated_attention(q, k, v, window, dilation):
    B, S, H, Dh = q.shape
    L = S // dilation

    def to_strided(a):
        return a.reshape(B, L, dilation, H, Dh).transpose(0, 2, 1, 3, 4).reshape(B * dilation, L, H, Dh)

    o, lse = banded_causal_attention(to_strided(q), to_strided(k), to_strided(v), window // dilation)
    o = o.reshape(B, dilation, L, H, Dh).transpose(0, 2, 1, 3, 4).reshape(B, S, H, Dh)
    lse = lse.reshape(B, dilation, L, H).transpose(0, 2, 1, 3).reshape(B, S, H)
    return o, lse


def even_mixer(h, w_in, w_pool, pool_scale, w_out, cos, sin):
    B, S, _ = h.shape
    z = h @ w_in
    a_out = multiscale_pool(z[..., :POOL_WIDTH], w_pool, pool_scale)
    qkv = z[..., POOL_WIDTH:].reshape(B, S, 3, N_DIL_GROUPS, DIL_HEADS, HEAD_DIM)
    outs, lses = [], []
    for g, (window, dilation) in enumerate(DIL_CONFIGS):
        q = partial_rope(qkv[:, :, 0, g], cos, sin)
        k = partial_rope(qkv[:, :, 1, g], cos, sin)
        o, lse = dilated_attention(q, k, qkv[:, :, 2, g], window, dilation)
        outs.append(o)
        lses.append(lse)
    wts = jax.nn.softmax(jnp.stack(lses, axis=0), axis=0)
    b_out = jnp.einsum('gbsh,gbshd->bshd', wts, jnp.stack(outs, axis=0))
    mixed = jnp.concatenate([a_out, b_out.reshape(B, S, DIL_GROUP_WIDTH).astype(h.dtype)], axis=-1)
    return mixed @ w_out


def forgetting_attention(q, k, v, f_logit):
    B, S, H, Dh = q.shape
    logf = jax.nn.log_sigmoid(f_logit.astype(jnp.float32))
    F = jnp.cumsum(logf, axis=1)
    Fk = jnp.transpose(F, (0, 2, 1))
    nb = S // Q_BLOCK
    qb = q.reshape(B, nb, Q_BLOCK, H, Dh).transpose(1, 0, 2, 3, 4)
    Fq = Fk.reshape(B, H, nb, Q_BLOCK).transpose(2, 0, 1, 3)
    kpos = jnp.arange(S)

    def block(args):
        qi, Fi, i = args
        s = jnp.einsum('bqhd,bkhd->bhqk', qi, k, preferred_element_type=jnp.float32) * (HEAD_DIM ** -0.5)
        s = s + (Fi[..., None] - Fk[:, :, None, :])
        qpos = i * Q_BLOCK + jnp.arange(Q_BLOCK)
        s = jnp.where(kpos[None, :] <= qpos[:, None], s, NEG)
        p = jax.nn.softmax(s, axis=-1)
        return jnp.einsum('bhqk,bkhd->bqhd', p.astype(v.dtype), v)

    out = lax.map(block, (qb, Fq, jnp.arange(nb)))
    return out.transpose(1, 0, 2, 3, 4).reshape(B, S, H, Dh)


def odd_mixer(h, w_in, b_forget, w_out):
    B, S, _ = h.shape
    z = h @ w_in
    qkv = z[..., :3 * FOX_WIDTH].reshape(B, S, 3, FOX_HEADS, HEAD_DIM)
    f_logit = z[..., 3 * FOX_WIDTH:] + b_forget
    o = forgetting_attention(qkv[:, :, 0], qkv[:, :, 1], qkv[:, :, 2], f_logit)
    return o.reshape(B, S, FOX_WIDTH) @ w_out


def swiglu(h, w_gu, w_down):
    g, u = jnp.split(h @ w_gu, 2, axis=-1)
    return (jax.nn.silu(g) * u) @ w_down


def moe_swiglu(h, w_router, w_gu_e, w_down_e):
    logits = (h @ w_router).astype(jnp.float32)
    top_v, top_i = lax.top_k(logits, TOP_K)
    gates = jax.nn.softmax(top_v, axis=-1)
    combine = jnp.sum(jax.nn.one_hot(top_i, N_EXPERTS, dtype=jnp.float32) * gates[..., None], axis=-2)
    out = jnp.zeros_like(h)
    for e in range(N_EXPERTS):
        out = out + combine[..., e:e + 1].astype(h.dtype) * swiglu(h, w_gu_e[e], w_down_e[e])
    return out


def setup_inputs(seed: int = 0) -> dict:
    key = jax.random.key(seed)
    ks = jax.random.split(key, 20)
    f32 = jnp.float32
    D = D_MODEL

    def nrm(k, shape, fan_in, gain=1.0):
        return jax.random.normal(k, shape, f32) * (gain * fan_in ** -0.5)

    return {
        "x": jax.random.normal(ks[0], (BATCH, SEQ, D), f32),
        "c": jax.random.normal(ks[1], (BATCH, D), f32),
        "w_ada": nrm(ks[2], (DEPTH, D, 6 * D), D),
        "b_ada": 0.02 * jax.random.normal(ks[3], (DEPTH, 6 * D), f32),
        "ln_g": 1.0 + 0.05 * jax.random.normal(ks[4], (DEPTH, 2, D), f32),
        "ln_b": 0.02 * jax.random.normal(ks[5], (DEPTH, 2, D), f32),
        "w_in_even": nrm(ks[6], (N_EVEN, D, EVEN_IN), D),
        "w_pool": nrm(ks[7], (N_EVEN, len(POOL_WINDOWS), POOL_GROUP_DIM, POOL_GROUP_DIM), POOL_GROUP_DIM),
        "pool_scale": 1.0 + 0.1 * jax.random.normal(ks[8], (N_EVEN, POOL_WIDTH), f32),
        "w_out_even": nrm(ks[9], (N_EVEN, EVEN_OUT, D), EVEN_OUT, DEEPNORM_BETA),
        "w_ffn_gu": nrm(ks[10], (N_EVEN, D, 2 * D_FF), D),
        "w_ffn_down": nrm(ks[11], (N_EVEN, D_FF, D), D_FF, DEEPNORM_BETA),
        "w_in_odd": nrm(ks[12], (N_ODD, D, ODD_IN), D),
        "b_forget": jax.random.uniform(ks[13], (N_ODD, FOX_HEADS), f32, 1.0, 4.0),
        "w_out_odd": nrm(ks[14], (N_ODD, FOX_WIDTH, D), FOX_WIDTH, DEEPNORM_BETA),
        "w_router": nrm(ks[15], (N_ODD, D, N_EXPERTS), D),
        "w_exp_gu": nrm(ks[16], (N_ODD, N_EXPERTS, D, 2 * D_FF_EXPERT), D),
        "w_exp_down": nrm(ks[17], (N_ODD, N_EXPERTS, D_FF_EXPERT, D), D_FF_EXPERT, DEEPNORM_BETA),
    }


def reference(x, c, w_ada, b_ada, ln_g, ln_b, w_in_even, w_pool, pool_scale, w_out_even,
              w_ffn_gu, w_ffn_down, w_in_odd, b_forget, w_out_odd, w_router, w_exp_gu, w_exp_down):
    cos, sin = rope_tables(x.shape[1])
    c_act = jax.nn.silu(c)
    for l in range(DEPTH):
        ada = (c_act @ w_ada[l] + b_ada[l])[:, None, :]
        sh1, sc1, g1, sh2, sc2, g2 = jnp.split(ada, 6, axis=-1)
        h = x * (1 + sc1) + sh1
        if l % 2 == 0:
            i = l // 2
            sub = even_mixer(h, w_in_even[i], w_pool[i], pool_scale[i], w_out_even[i], cos, sin)
        else:
            i = l // 2
            sub = odd_mixer(h, w_in_odd[i], b_forget[i], w_out_odd[i])
        x = layer_norm(DEEPNORM_ALPHA * x + g1 * sub, ln_g[l, 0], ln_b[l, 0])
        h = x * (1 + sc2) + sh2
        if l % 2 == 0:
            sub = swiglu(h, w_ffn_gu[l // 2], w_ffn_down[l // 2])
        else:
            sub = moe_swiglu(h, w_router[l // 2], w_exp_gu[l // 2], w_exp_down[l // 2])
        x = layer_norm(DEEPNORM_ALPHA * x + g2 * sub, ln_g[l, 1], ln_b[l, 1])
    return x
```

```python
import functools
import math

import jax
import jax.numpy as jnp
from jax import lax
from jax.experimental import pallas as pl
from jax.experimental.pallas import tpu as pltpu

F32 = jnp.float32
BF16 = jnp.bfloat16

D_MODEL = 1024
DEPTH = 4
HEAD_DIM = 64
POOL_WINDOWS = (2, 4, 8, 16)
POOL_GROUP_DIM = 64
POOL_WIDTH = 256
DIL_CONFIGS = ((128, 1), (512, 4), (2048, 16))
DIL_HEADS = 4
DIL_GROUP_WIDTH = DIL_HEADS * HEAD_DIM
BAND_BLOCK = 128
FOX_HEADS = 16
FOX_WIDTH = 1024
ROPE_THETA = 500000.0
ROPE_DIM = 16
D_FF = 2816
N_EXPERTS = 8
DEEPNORM_ALPHA = (2 * DEPTH) ** 0.25
LN_EPS = 1e-5
NEG = -1e30
LOG2E = 1.4426950408889634

LANES = 128
TOKEN_TILE = 512
FF_CHUNK = 1408
FOX_Q_TILE = 256
FOX_K_TILE = 256
VMEM_LIMIT = 56 * 1024 * 1024


def _params(sem, vmem=VMEM_LIMIT):
    return pltpu.CompilerParams(dimension_semantics=sem, vmem_limit_bytes=vmem)


def _modulate(x, sc, sh):
    return (x * (1.0 + sc) + sh).astype(BF16)


def _deepnorm_ln(x, sub, gate, g, b):
    y = DEEPNORM_ALPHA * x + gate * sub
    mu = jnp.mean(y, axis=-1, keepdims=True)
    yc = y - mu
    var = jnp.mean(yc * yc, axis=-1, keepdims=True)
    return yc * lax.rsqrt(var + LN_EPS) * g + b


def _ada_kernel(c_ref, w_ref, b_ref, o_ref):
    c = c_ref[...]
    ca = (c * jax.nn.sigmoid(c)).astype(BF16)
    o_ref[0] = jnp.dot(ca, w_ref[0].astype(BF16), preferred_element_type=F32) + b_ref[0]


def _ada(c, w_ada, b_ada):
    depth, d, n = w_ada.shape
    bsz = c.shape[0]
    tn = 1024
    return pl.pallas_call(
        _ada_kernel,
        out_shape=jax.ShapeDtypeStruct((depth, bsz, n), F32),
        grid=(depth, n // tn),
        in_specs=[
            pl.BlockSpec((bsz, d), lambda l, j: (0, 0)),
            pl.BlockSpec((1, d, tn), lambda l, j: (l, 0, j)),
            pl.BlockSpec((1, 1, tn), lambda l, j: (l, 0, j)),
        ],
        out_specs=pl.BlockSpec((1, bsz, tn), lambda l, j: (l, 0, j)),
        compiler_params=_params(("parallel", "parallel")),
        name="ada_mod",
    )(c, w_ada, b_ada.reshape(depth, 1, n))


def _even_in_kernel(x_ref, sc_ref, sh_ref, w_ref, ct_ref, sa_ref, sb_ref, u_ref, *qkv_refs):
    h = _modulate(x_ref[...], sc_ref[0], sh_ref[0])
    u_ref[...] = jnp.dot(h, w_ref[:, 0:POOL_WIDTH], preferred_element_type=F32)
    ct, sa, sb = ct_ref[...], sa_ref[...], sb_ref[...]
    for n in range(9):
        lo = POOL_WIDTH + n * DIL_GROUP_WIDTH
        z = jnp.dot(h, w_ref[:, lo:lo + DIL_GROUP_WIDTH], preferred_element_type=F32)
        if n < 6:
            z = z * ct + pltpu.roll(z, 8, 1) * sa + pltpu.roll(z, DIL_GROUP_WIDTH - 8, 1) * sb
        if n < 3:
            z = z * (HEAD_DIM ** -0.5)
        qkv_refs[n][...] = z.astype(BF16)


def _even_in(x2, sc, sh, w_bf, tabs, seq):
    t, d = x2.shape
    tm = TOKEN_TILE
    tps = seq // tm
    n_in = w_bf.shape[1]
    mod_spec = pl.BlockSpec((1, 1, d), lambda i: (i // tps, 0, 0))
    tab_spec = pl.BlockSpec((tm, DIL_GROUP_WIDTH), lambda i: (i % tps, 0))
    out_spec = pl.BlockSpec((tm, DIL_GROUP_WIDTH), lambda i: (i, 0))
    return pl.pallas_call(
        _even_in_kernel,
        out_shape=[jax.ShapeDtypeStruct((t, POOL_WIDTH), F32)]
        + [jax.ShapeDtypeStruct((t, DIL_GROUP_WIDTH), BF16)] * 9,
        grid=(t // tm,),
        in_specs=[
            pl.BlockSpec((tm, d), lambda i: (i, 0)),
            mod_spec,
            mod_spec,
            pl.BlockSpec((d, n_in), lambda i: (0, 0)),
            tab_spec,
            tab_spec,
            tab_spec,
        ],
        out_specs=[out_spec] * 10,
        compiler_params=_params(("parallel",)),
        name="even_in_proj",
    )(x2, sc, sh, w_bf, *tabs)


def _rope_tables(seq):
    pos = jnp.arange(seq, dtype=F32)
    inv = ROPE_THETA ** (-jnp.arange(0, ROPE_DIM, 2, dtype=F32) / ROPE_DIM)
    ang = pos[:, None] * inv[None, :]
    cos, sin = jnp.cos(ang), jnp.sin(ang)
    half = ROPE_DIM // 2
    zeros = jnp.zeros((seq, half), F32)
    rest = HEAD_DIM - ROPE_DIM
    ct = jnp.concatenate([cos, cos, jnp.ones((seq, rest), F32)], axis=1)
    sa = jnp.concatenate([zeros, sin, jnp.zeros((seq, rest), F32)], axis=1)
    sb = jnp.concatenate([-sin, zeros, jnp.zeros((seq, rest), F32)], axis=1)
    return tuple(jnp.tile(a, (1, DIL_HEADS)) for a in (ct, sa, sb))


def _band_kernel(q_ref, kp_ref, kc_ref, vp_ref, vc_ref, o_ref, lse_ref):
    mb = pl.program_id(2)
    blk = BAND_BLOCK
    q = q_ref[0]
    kk = jnp.concatenate([kp_ref[0], kc_ref[0]], axis=0)
    vv = jnp.concatenate([vp_ref[0], vc_ref[0]], axis=0)
    qi = lax.broadcasted_iota(jnp.int32, (blk, 2 * blk), 0)
    kj = lax.broadcasted_iota(jnp.int32, (blk, 2 * blk), 1)
    dist = qi + blk - kj
    first_key = jnp.where(mb > 0, 0, blk)
    valid = (dist >= 0) & (dist <= blk) & (kj >= first_key)
    lane_q = lax.broadcasted_iota(jnp.int32, q.shape, 1) // HEAD_DIM
    lane_v = lax.broadcasted_iota(jnp.int32, vv.shape, 1) // HEAD_DIM
    lane_o = lax.broadcasted_iota(jnp.int32, (blk, DIL_GROUP_WIDTH), 1) // HEAD_DIM
    acc = jnp.zeros((blk, DIL_GROUP_WIDTH), F32)
    den_full = jnp.ones((blk, DIL_GROUP_WIDTH), F32)
    lse_full = jnp.zeros((blk, DIL_GROUP_WIDTH), F32)
    for h in range(DIL_HEADS):
        qh = jnp.where(lane_q == h, q, jnp.zeros_like(q))
        s = lax.dot_general(qh, kk, (((1,), (1,)), ((), ())), preferred_element_type=F32)
        s = jnp.where(valid, s, NEG)
        m = jnp.max(s, axis=-1, keepdims=True)
        p = jnp.exp(s - m)
        den = jnp.sum(p, axis=-1, keepdims=True)
        vh = jnp.where(lane_v == h, vv, jnp.zeros_like(vv))
        acc = acc + jnp.dot(p.astype(BF16), vh, preferred_element_type=F32)
        den_full = jnp.where(lane_o == h, den, den_full)
        lse_full = jnp.where(lane_o == h, m + jnp.log(den), lse_full)
    o_ref[0] = acc / den_full
    lse_ref[0] = lse_full


def _band_attention(q, k, v, bsz, seq, dil):
    ln = seq // dil
    w = DIL_GROUP_WIDTH
    view = lambda a: a.reshape(bsz, ln, dil * w)
    cur = pl.BlockSpec((1, BAND_BLOCK, w), lambda b, r, m: (b, m, r))
    prev = pl.BlockSpec((1, BAND_BLOCK, w), lambda b, r, m: (b, jnp.maximum(m - 1, 0), r))
    o, lse = pl.pallas_call(
        _band_kernel,
        out_shape=[jax.ShapeDtypeStruct((bsz, ln, dil * w), F32)] * 2,
        grid=(bsz, dil, ln // BAND_BLOCK),
        in_specs=[cur, prev, cur, prev, cur],
        out_specs=[cur, cur],
        compiler_params=_params(("parallel", "parallel", "parallel")),
        name=f"band_attn_d{dil}",
    )(view(q), view(k), view(k), view(v), view(v))
    return o.reshape(bsz * seq, w), lse.reshape(bsz * seq, w)


def _even_out_kernel(tps, u_ref, up_ref, o0, o1, o2, l0, l1, l2, wp_ref, ps_ref, wo_ref, x_ref, gate_ref,
                     g_ref, b_ref, out_ref):
    i = pl.program_id(0)
    tm = u_ref.shape[0]
    halo = up_ref.shape[0]
    u = u_ref[...]
    prev = jnp.where(i % tps == 0, jnp.zeros_like(up_ref[...]), up_ref[...])
    ext = jnp.concatenate([prev, u], axis=0)
    lane_g = lax.broadcasted_iota(jnp.int32, (tm, POOL_WIDTH), 1) // POOL_GROUP_DIM
    row = lax.broadcasted_iota(jnp.int32, (tm, POOL_WIDTH), 0)
    pos1 = ((i % tps) * tm + row + 1).astype(F32)
    s = ext
    pooled = jnp.zeros((tm, POOL_WIDTH), F32)
    wvec = jnp.zeros((tm, POOL_WIDTH), F32)
    shift = 1
    for g, w in enumerate(POOL_WINDOWS):
        while shift < w:
            s = s + pltpu.roll(s, shift, 0)
            shift *= 2
        pooled = jnp.where(lane_g == g, s[halo:, :], pooled)
        wvec = jnp.where(lane_g == g, float(w), wvec)
    pooled = pooled / jnp.minimum(pos1, wvec) - u
    a_out = jnp.dot(pooled.astype(BF16), wp_ref[...], preferred_element_type=F32) * ps_ref[...]
    la, lb, lc = l0[...], l1[...], l2[...]
    mx = jnp.maximum(jnp.maximum(la, lb), lc)
    ea, eb, ec = jnp.exp(la - mx), jnp.exp(lb - mx), jnp.exp(lc - mx)
    b_out = (ea * o0[...] + eb * o1[...] + ec * o2[...]) / (ea + eb + ec)
    mixed = jnp.concatenate([a_out, b_out], axis=-1).astype(BF16)
    sub = jnp.dot(mixed, wo_ref[...], preferred_element_type=F32)
    out_ref[...] = _deepnorm_ln(x_ref[...], sub, gate_ref[0], g_ref[...], b_ref[...])


def _even_out(u, os_, lses, wpool_bd, pool_scale, wo_bf, x2, gate, ln_g, ln_b, seq):
    t, d = x2.shape
    tm = TOKEN_TILE
    tps = seq // tm
    halo = 16
    row = lambda w: pl.BlockSpec((tm, w), lambda i: (i, 0))
    full = lambda a: pl.BlockSpec(a.shape, lambda i: (0,) * a.ndim)
    return pl.pallas_call(
        functools.partial(_even_out_kernel, tps),
        out_shape=jax.ShapeDtypeStruct((t, d), F32),
        grid=(t // tm,),
        in_specs=[
            row(POOL_WIDTH),
            pl.BlockSpec((halo, POOL_WIDTH), lambda i: (jnp.maximum(i * (tm // halo) - 1, 0), 0)),
            *[row(DIL_GROUP_WIDTH)] * 6,
            full(wpool_bd),
            full(pool_scale),
            full(wo_bf),
            row(d),
            pl.BlockSpec((1, 1, d), lambda i: (i // tps, 0, 0)),
            full(ln_g),
            full(ln_b),
        ],
        out_specs=row(d),
        compiler_params=_params(("parallel",)),
        name="even_out_proj",
    )(u, u, *os_, *lses, wpool_bd, pool_scale, wo_bf, x2, gate, ln_g, ln_b)


def _ffn_kernel(n_exp, x_ref, sc_ref, sh_ref, comb_ref, wg_ref, wu_ref, wd_ref, gate_ref, g_ref, b_ref,
                out_ref, h_ref, acc_ref):
    e = pl.program_id(1)
    c = pl.program_id(2)

    @pl.when((e == 0) & (c == 0))
    def _():
        h_ref[...] = _modulate(x_ref[...], sc_ref[0], sh_ref[0])
        acc_ref[...] = jnp.zeros_like(acc_ref)

    h = h_ref[...]
    gt = jnp.dot(h, wg_ref[0], preferred_element_type=F32)
    up = jnp.dot(h, wu_ref[0], preferred_element_type=F32)
    a = (gt * jax.nn.sigmoid(gt) * up).astype(BF16)
    y = jnp.dot(a, wd_ref[0], preferred_element_type=F32)
    if n_exp > 1:
        lane = lax.broadcasted_iota(jnp.int32, comb_ref.shape, 1)
        y = y * jnp.sum(jnp.where(lane == e, comb_ref[...], 0.0), axis=-1, keepdims=True)
    acc_ref[...] += y

    @pl.when((e == n_exp - 1) & (c == pl.num_programs(2) - 1))
    def _():
        out_ref[...] = _deepnorm_ln(x_ref[...], acc_ref[...], gate_ref[0], g_ref[...], b_ref[...])


def _ffn(x2, sc, sh, comb, wgu_bf, wd_bf, gate, ln_g, ln_b, seq):
    t, d = x2.shape
    n_exp = wgu_bf.shape[0]
    tm = TOKEN_TILE
    tps = seq // tm
    nch = D_FF // FF_CHUNK
    mod = pl.BlockSpec((1, 1, d), lambda i, e, c: (i // tps, 0, 0))
    row = lambda w: pl.BlockSpec((tm, w), lambda i, e, c: (i, 0))
    vec = pl.BlockSpec((1, d), lambda i, e, c: (0, 0))
    return pl.pallas_call(
        functools.partial(_ffn_kernel, n_exp),
        out_shape=jax.ShapeDtypeStruct((t, d), F32),
        grid=(t // tm, n_exp, nch),
        in_specs=[
            row(d),
            mod,
            mod,
            row(LANES),
            pl.BlockSpec((1, d, FF_CHUNK), lambda i, e, c: (e, 0, c)),
            pl.BlockSpec((1, d, FF_CHUNK), lambda i, e, c: (e, 0, nch + c)),
            pl.BlockSpec((1, FF_CHUNK, d), lambda i, e, c: (e, c, 0)),
            mod,
            vec,
            vec,
        ],
        out_specs=row(d),
        scratch_shapes=[pltpu.VMEM((tm, d), BF16), pltpu.VMEM((tm, d), F32)],
        compiler_params=_params(("parallel", "arbitrary", "arbitrary")),
        name=f"ffn_e{n_exp}",
    )(x2, sc, sh, comb, wgu_bf, wgu_bf, wd_bf, gate, ln_g, ln_b)


def _split3(x):
    hi = x.astype(BF16)
    r = x - hi.astype(F32)
    mid = r.astype(BF16)
    lo = (r - mid.astype(F32)).astype(BF16)
    return hi, mid, lo


def _odd_in_kernel(tps, x_ref, sc_ref, sh_ref, w_ref, wf_ref, bf_ref, q_ref, k_ref, v_ref, f_ref, carry_ref):
    i = pl.program_id(0)
    tm = x_ref.shape[0]
    h = _modulate(x_ref[...], sc_ref[0], sh_ref[0])
    q = jnp.dot(h, w_ref[:, 0:FOX_WIDTH], preferred_element_type=F32)
    q_ref[...] = (q * (HEAD_DIM ** -0.5 * LOG2E)).astype(BF16)
    k_ref[...] = jnp.dot(h, w_ref[:, FOX_WIDTH:2 * FOX_WIDTH], preferred_element_type=F32).astype(BF16)
    v_ref[...] = jnp.dot(h, w_ref[:, 2 * FOX_WIDTH:3 * FOX_WIDTH], preferred_element_type=F32).astype(BF16)
    f = jnp.dot(h, wf_ref[...], preferred_element_type=F32) + bf_ref[...]
    logf = jnp.minimum(f, 0.0) - jnp.log(1.0 + jnp.exp(-jnp.abs(f)))
    r = lax.broadcasted_iota(jnp.int32, (tm, tm), 0)
    cidx = lax.broadcasted_iota(jnp.int32, (tm, tm), 1)
    tri = jnp.where(cidx <= r, 1.0, 0.0).astype(BF16)
    hi, mid, lo = _split3(logf)
    cs = (jnp.dot(tri, lo, preferred_element_type=F32) + jnp.dot(tri, mid, preferred_element_type=F32)
          + jnp.dot(tri, hi, preferred_element_type=F32))

    @pl.when(i % tps == 0)
    def _():
        carry_ref[...] = jnp.zeros_like(carry_ref)

    cs = cs + carry_ref[...]
    f_ref[...] = cs
    carry_ref[...] = cs[tm - 1:tm, :]


def _odd_in(x2, sc, sh, w_bf, wf_bf, bf_pad, seq):
    t, d = x2.shape
    tm = TOKEN_TILE
    tps = seq // tm
    mod = pl.BlockSpec((1, 1, d), lambda i: (i // tps, 0, 0))
    row = lambda w: pl.BlockSpec((tm, w), lambda i: (i, 0))
    full = lambda a: pl.BlockSpec(a.shape, lambda i: (0,) * a.ndim)
    return pl.pallas_call(
        functools.partial(_odd_in_kernel, tps),
        out_shape=[jax.ShapeDtypeStruct((t, FOX_WIDTH), BF16)] * 3 + [jax.ShapeDtypeStruct((t, LANES), F32)],
        grid=(t // tm,),
        in_specs=[row(d), mod, mod, full(w_bf), full(wf_bf), full(bf_pad)],
        out_specs=[row(FOX_WIDTH)] * 3 + [row(LANES)],
        scratch_shapes=[pltpu.VMEM((1, LANES), F32)],
        compiler_params=_params(("arbitrary",)),
        name="odd_in_proj",
    )(x2, sc, sh, w_bf, wf_bf, bf_pad)


def _fox_kernel(q_ref, k_ref, v_ref, f_ref, o_ref, acc_ref, m_ref, l_ref):
    qi = pl.program_id(2)
    tq = q_ref.shape[1]
    tk = FOX_K_TILE
    half = HEAD_DIM
    q = q_ref[0]
    lane_q = lax.broadcasted_iota(jnp.int32, q.shape, 1)
    qa = jnp.where(lane_q < half, q, jnp.zeros_like(q))
    qb = jnp.where(lane_q < half, jnp.zeros_like(q), q)
    lane_v = lax.broadcasted_iota(jnp.int32, (tk, LANES), 1)
    lane_o = lax.broadcasted_iota(jnp.int32, (tq, LANES), 1)
    f0 = f_ref[0, :, :, pl.ds(pl.multiple_of(qi * tq, tq), tk)][:, :, 0:1]
    acc_ref[...] = jnp.zeros_like(acc_ref)
    m_ref[...] = jnp.full(m_ref.shape, NEG, F32)
    l_ref[...] = jnp.zeros_like(l_ref)

    def step(j, masked):
        start = pl.multiple_of(j * tk, tk)
        k = k_ref[0, pl.ds(start, tk), :]
        v = v_ref[0, pl.ds(start, tk), :]
        bias = (f0 - f_ref[0, :, :, pl.ds(start, tk)]) * LOG2E
        va = jnp.where(lane_v < half, v, jnp.zeros_like(v))
        vb = jnp.where(lane_v < half, jnp.zeros_like(v), v)
        upd = None
        alphas = []
        for hh, (qh, vh) in enumerate(((qa, va), (qb, vb))):
            s = lax.dot_general(qh, k, (((1,), (1,)), ((), ())), preferred_element_type=F32) + bias[hh]
            if masked:
                rr = lax.broadcasted_iota(jnp.int32, s.shape, 0) + qi * tq
                cc = lax.broadcasted_iota(jnp.int32, s.shape, 1) + start
                s = jnp.where(cc <= rr, s, NEG)
            m_old = m_ref[hh]
            m_new = jnp.maximum(m_old, jnp.max(s, axis=-1, keepdims=True))
            alpha = jnp.exp2(m_old - m_new)
            p = jnp.exp2(s - m_new)
            l_ref[hh] = alpha * l_ref[hh] + jnp.sum(p, axis=-1, keepdims=True)
            m_ref[hh] = m_new
            pv = jnp.dot(p.astype(BF16), vh, preferred_element_type=F32)
            upd = pv if upd is None else upd + pv
            alphas.append(alpha)
        acc_ref[...] = acc_ref[...] * jnp.where(lane_o < half, alphas[0], alphas[1]) + upd

    nfull = (qi * tq) // tk

    def body(j, carry):
        step(j, False)
        return carry

    lax.fori_loop(0, nfull, body, 0)
    for jj in range(tq // tk):
        step(nfull + jj, True)
    o_ref[0] = (acc_ref[...] / jnp.where(lane_o < half, l_ref[0], l_ref[1])).astype(BF16)


def _fox_attention(q, k, v, fk, bsz, seq):
    tq = FOX_Q_TILE
    npairs = FOX_HEADS // 2
    return pl.pallas_call(
        _fox_kernel,
        out_shape=jax.ShapeDtypeStruct((bsz, seq, FOX_WIDTH), BF16),
        grid=(bsz, npairs, seq // tq),
        in_specs=[
            pl.BlockSpec((1, tq, LANES), lambda b, j, i: (b, i, j)),
            pl.BlockSpec((1, seq, LANES), lambda b, j, i: (b, 0, j)),
            pl.BlockSpec((1, seq, LANES), lambda b, j, i: (b, 0, j)),
            pl.BlockSpec((1, 2, 1, seq), lambda b, j, i: (b, j, 0, 0)),
        ],
        out_specs=pl.BlockSpec((1, tq, LANES), lambda b, j, i: (b, i, j)),
        scratch_shapes=[
            pltpu.VMEM((tq, LANES), F32),
            pltpu.VMEM((2, tq, 1), F32),
            pltpu.VMEM((2, tq, 1), F32),
        ],
        compiler_params=_params(("parallel", "parallel", "arbitrary")),
        name="fox_attn",
    )(q, k, v, fk)


def _odd_out_kernel(o_ref, wo_ref, x_ref, gate_ref, g_ref, b_ref, out_ref):
    sub = jnp.dot(o_ref[...], wo_ref[...], preferred_element_type=F32)
    out_ref[...] = _deepnorm_ln(x_ref[...], sub, gate_ref[0], g_ref[...], b_ref[...])


def _odd_out(o2, wo_bf, x2, gate, ln_g, ln_b, seq):
    t, d = x2.shape
    tm = TOKEN_TILE
    tps = seq // tm
    row = lambda w: pl.BlockSpec((tm, w), lambda i: (i, 0))
    full = lambda a: pl.BlockSpec(a.shape, lambda i: (0,) * a.ndim)
    return pl.pallas_call(
        _odd_out_kernel,
        out_shape=jax.ShapeDtypeStruct((t, d), F32),
        grid=(t // tm,),
        in_specs=[row(FOX_WIDTH), full(wo_bf), row(d), pl.BlockSpec((1, 1, d), lambda i: (i // tps, 0, 0)),
                  full(ln_g), full(ln_b)],
        out_specs=row(d),
        compiler_params=_params(("parallel",)),
        name="odd_out_proj",
    )(o2, wo_bf, x2, gate, ln_g, ln_b)


def _router_kernel(x_ref, sc_ref, sh_ref, wr_ref, comb_ref):
    h = _modulate(x_ref[...], sc_ref[0], sh_ref[0])
    logits = jnp.dot(h, wr_ref[...], preferred_element_type=F32)
    lane = lax.broadcasted_iota(jnp.int32, logits.shape, 1)
    logits = jnp.where(lane < N_EXPERTS, logits, -jnp.inf)
    m1 = jnp.max(logits, axis=-1, keepdims=True)
    i1 = jnp.min(jnp.where(logits == m1, lane, LANES), axis=-1, keepdims=True)
    rest = jnp.where(lane == i1, -jnp.inf, logits)
    m2 = jnp.max(rest, axis=-1, keepdims=True)
    i2 = jnp.min(jnp.where(rest == m2, lane, LANES), axis=-1, keepdims=True)
    e2 = jnp.exp(m2 - m1)
    g1 = 1.0 / (1.0 + e2)
    g2 = e2 / (1.0 + e2)
    comb_ref[...] = jnp.where(lane == i1, g1, 0.0) + jnp.where(lane == i2, g2, 0.0)


def _router(x2, sc, sh, wr_pad, seq):
    t, d = x2.shape
    tm = TOKEN_TILE
    tps = seq // tm
    mod = pl.BlockSpec((1, 1, d), lambda i: (i // tps, 0, 0))
    return pl.pallas_call(
        _router_kernel,
        out_shape=jax.ShapeDtypeStruct((t, LANES), F32),
        grid=(t // tm,),
        in_specs=[pl.BlockSpec((tm, d), lambda i: (i, 0)), mod, mod,
                  pl.BlockSpec(wr_pad.shape, lambda i: (0, 0))],
        out_specs=pl.BlockSpec((tm, LANES), lambda i: (i, 0)),
        compiler_params=_params(("parallel",)),
        name="router",
    )(x2, sc, sh, wr_pad)


def _block_diag(w_pool):
    g, c, _ = w_pool.shape
    out = jnp.zeros((g * c, g * c), w_pool.dtype)
    for i in range(g):
        out = out.at[i * c:(i + 1) * c, i * c:(i + 1) * c].set(w_pool[i])
    return out


def kernel(x, c, w_ada, b_ada, ln_g, ln_b, w_in_even, w_pool, pool_scale, w_out_even, w_ffn_gu, w_ffn_down,
           w_in_odd, b_forget, w_out_odd, w_router, w_exp_gu, w_exp_down):
    bsz, seq, d = x.shape
    t = bsz * seq
    tabs = _rope_tables(seq)
    ada = _ada(c, w_ada, b_ada)
    x2 = x.reshape(t, d)
    ones_comb = jnp.ones((t, LANES), F32)
    for l in range(DEPTH):
        i = l // 2
        sh1, sc1, g1, sh2, sc2, g2 = [ada[l, :, n * d:(n + 1) * d].reshape(bsz, 1, d) for n in range(6)]
        lng = ln_g[l].reshape(2, 1, d)
        lnb = ln_b[l].reshape(2, 1, d)
        if l % 2 == 0:
            outs = _even_in(x2, sc1, sh1, w_in_even[i].astype(BF16), tabs, seq)
            u, qkv = outs[0], outs[1:]
            os_, lses = [], []
            for g, (_, dil) in enumerate(DIL_CONFIGS):
                o, lse = _band_attention(qkv[g], qkv[3 + g], qkv[6 + g], bsz, seq, dil)
                os_.append(o)
                lses.append(lse)
            x2 = _even_out(u, os_, lses, _block_diag(w_pool[i]).astype(BF16), pool_scale[i].reshape(1, -1),
                           w_out_even[i].astype(BF16), x2, g1, lng[0], lnb[0], seq)
            x2 = _ffn(x2, sc2, sh2, ones_comb, w_ffn_gu[i][None].astype(BF16), w_ffn_down[i][None].astype(BF16),
                      g2, lng[1], lnb[1], seq)
        else:
            w_in = w_in_odd[i]
            wf = jnp.zeros((d, LANES), F32).at[:, :FOX_HEADS].set(w_in[:, 3 * FOX_WIDTH:]).astype(BF16)
            bfp = jnp.zeros((1, LANES), F32).at[0, :FOX_HEADS].set(b_forget[i])
            q, k, v, fcum = _odd_in(x2, sc1, sh1, w_in[:, :3 * FOX_WIDTH].astype(BF16), wf, bfp, seq)
            fk = fcum[:, :FOX_HEADS].reshape(bsz, seq, FOX_HEADS).transpose(0, 2, 1).reshape(bsz, FOX_HEADS, 1, seq)
            o = _fox_attention(q.reshape(bsz, seq, FOX_WIDTH), k.reshape(bsz, seq, FOX_WIDTH),
                               v.reshape(bsz, seq, FOX_WIDTH), fk, bsz, seq)
            x2 = _odd_out(o.reshape(t, FOX_WIDTH), w_out_odd[i].astype(BF16), x2, g1, lng[0], lnb[0], seq)
            wr = jnp.zeros((d, LANES), F32).at[:, :N_EXPERTS].set(w_router[i]).astype(BF16)
            comb = _router(x2, sc2, sh2, wr, seq)
            x2 = _ffn(x2, sc2, sh2, comb, w_exp_gu[i].astype(BF16), w_exp_down[i].astype(BF16), g2, lng[1],
                      lnb[1], seq)
    return x2.reshape(bsz, seq, d)
```

```python
import functools
import math

import numpy as np

import jax
import jax.numpy as jnp
from jax import lax
from jax.experimental import pallas as pl
from jax.experimental.pallas import tpu as pltpu

F32 = jnp.float32
BF16 = jnp.bfloat16

D_MODEL = 1024
DEPTH = 4
HEAD_DIM = 64
POOL_WINDOWS = (2, 4, 8, 16)
POOL_GROUP_DIM = 64
POOL_WIDTH = 256
DIL_CONFIGS = ((128, 1), (512, 4), (2048, 16))
DIL_HEADS = 4
DIL_GROUP_WIDTH = DIL_HEADS * HEAD_DIM
BAND_BLOCK = 128
FOX_HEADS = 16
FOX_WIDTH = 1024
ROPE_THETA = 500000.0
ROPE_DIM = 16
D_FF = 2816
N_EXPERTS = 8
DEEPNORM_ALPHA = (2 * DEPTH) ** 0.25
LN_EPS = 1e-5
NEG = -1e30
LOG2E = 1.4426950408889634

LANES = 128
TOKEN_TILE = 512
FF_CHUNK = 1408
FOX_Q_TILE = 512
FOX_K_TILE = 512
FOX_ROW_CHUNK = 64
VMEM_LIMIT = 56 * 1024 * 1024


def _params(sem, vmem=VMEM_LIMIT):
    return pltpu.CompilerParams(dimension_semantics=sem, vmem_limit_bytes=vmem)


def _modulate(x, sc, sh):
    return (x * (1.0 + sc) + sh).astype(BF16)


def _deepnorm_ln(x, sub, gate, g, b):
    y = DEEPNORM_ALPHA * x + gate * sub
    mu = jnp.mean(y, axis=-1, keepdims=True)
    yc = y - mu
    var = jnp.mean(yc * yc, axis=-1, keepdims=True)
    return yc * lax.rsqrt(var + LN_EPS) * g + b


def _ada_kernel(c_ref, w_ref, b_ref, o_ref):
    c = c_ref[...]
    ca = (c * jax.nn.sigmoid(c)).astype(BF16)
    o_ref[0] = jnp.dot(ca, w_ref[0].astype(BF16), preferred_element_type=F32) + b_ref[0]


def _ada(c, w_ada, b_ada):
    depth, d, n = w_ada.shape
    bsz = c.shape[0]
    tn = 1024
    return pl.pallas_call(
        _ada_kernel,
        out_shape=jax.ShapeDtypeStruct((depth, bsz, n), F32),
        grid=(depth, n // tn),
        in_specs=[
            pl.BlockSpec((bsz, d), lambda l, j: (0, 0)),
            pl.BlockSpec((1, d, tn), lambda l, j: (l, 0, j)),
            pl.BlockSpec((1, 1, tn), lambda l, j: (l, 0, j)),
        ],
        out_specs=pl.BlockSpec((1, bsz, tn), lambda l, j: (l, 0, j)),
        compiler_params=_params(("parallel", "parallel")),
        name="ada_mod",
    )(c, w_ada, b_ada.reshape(depth, 1, n))


def _even_in_kernel(x_ref, sc_ref, sh_ref, w_ref, ct_ref, sa_ref, sb_ref, u_ref, *qkv_refs):
    h = _modulate(x_ref[...], sc_ref[0], sh_ref[0])
    u_ref[...] = jnp.dot(h, w_ref[:, 0:POOL_WIDTH], preferred_element_type=F32)
    ct, sa, sb = ct_ref[...], sa_ref[...], sb_ref[...]
    for n in range(9):
        lo = POOL_WIDTH + n * DIL_GROUP_WIDTH
        z = jnp.dot(h, w_ref[:, lo:lo + DIL_GROUP_WIDTH], preferred_element_type=F32)
        if n < 6:
            z = z * ct + pltpu.roll(z, 8, 1) * sa + pltpu.roll(z, DIL_GROUP_WIDTH - 8, 1) * sb
        if n < 3:
            z = z * (HEAD_DIM ** -0.5)
        qkv_refs[n][...] = z.astype(BF16)


def _even_in(x2, sc, sh, w_bf, tabs, seq):
    t, d = x2.shape
    tm = TOKEN_TILE
    tps = seq // tm
    n_in = w_bf.shape[1]
    mod_spec = pl.BlockSpec((1, 1, d), lambda i: (i // tps, 0, 0))
    tab_spec = pl.BlockSpec((tm, DIL_GROUP_WIDTH), lambda i: (i % tps, 0))
    out_spec = pl.BlockSpec((tm, DIL_GROUP_WIDTH), lambda i: (i, 0))
    return pl.pallas_call(
        _even_in_kernel,
        out_shape=[jax.ShapeDtypeStruct((t, POOL_WIDTH), F32)]
        + [jax.ShapeDtypeStruct((t, DIL_GROUP_WIDTH), BF16)] * 9,
        grid=(t // tm,),
        in_specs=[
            pl.BlockSpec((tm, d), lambda i: (i, 0)),
            mod_spec,
            mod_spec,
            pl.BlockSpec((d, n_in), lambda i: (0, 0)),
            tab_spec,
            tab_spec,
            tab_spec,
        ],
        out_specs=[out_spec] * 10,
        compiler_params=_params(("parallel",)),
        name="even_in_proj",
    )(x2, sc, sh, w_bf, *tabs)


def _rope_tables(seq):
    pos = jnp.arange(seq, dtype=F32)
    inv = ROPE_THETA ** (-jnp.arange(0, ROPE_DIM, 2, dtype=F32) / ROPE_DIM)
    ang = pos[:, None] * inv[None, :]
    cos, sin = jnp.cos(ang), jnp.sin(ang)
    half = ROPE_DIM // 2
    zeros = jnp.zeros((seq, half), F32)
    rest = HEAD_DIM - ROPE_DIM
    ct = jnp.concatenate([cos, cos, jnp.ones((seq, rest), F32)], axis=1)
    sa = jnp.concatenate([zeros, sin, jnp.zeros((seq, rest), F32)], axis=1)
    sb = jnp.concatenate([-sin, zeros, jnp.zeros((seq, rest), F32)], axis=1)
    return tuple(jnp.tile(a, (1, DIL_HEADS)) for a in (ct, sa, sb))


def _band_kernel(q_ref, kp_ref, kc_ref, vp_ref, vc_ref, o_ref, lse_ref):
    mb = pl.program_id(2)
    blk = BAND_BLOCK
    q = q_ref[0]
    kk = jnp.concatenate([kp_ref[0], kc_ref[0]], axis=0)
    vv = jnp.concatenate([vp_ref[0], vc_ref[0]], axis=0)
    qi = lax.broadcasted_iota(jnp.int32, (blk, 2 * blk), 0)
    kj = lax.broadcasted_iota(jnp.int32, (blk, 2 * blk), 1)
    dist = qi + blk - kj
    first_key = jnp.where(mb > 0, 0, blk)
    valid = (dist >= 0) & (dist <= blk) & (kj >= first_key)
    lane_q = lax.broadcasted_iota(jnp.int32, q.shape, 1) // HEAD_DIM
    lane_v = lax.broadcasted_iota(jnp.int32, vv.shape, 1) // HEAD_DIM
    lane_o = lax.broadcasted_iota(jnp.int32, (blk, DIL_GROUP_WIDTH), 1) // HEAD_DIM
    acc = jnp.zeros((blk, DIL_GROUP_WIDTH), F32)
    den_full = jnp.ones((blk, DIL_GROUP_WIDTH), F32)
    lse_full = jnp.zeros((blk, DIL_GROUP_WIDTH), F32)
    for h in range(DIL_HEADS):
        qh = jnp.where(lane_q == h, q, jnp.zeros_like(q))
        s = lax.dot_general(qh, kk, (((1,), (1,)), ((), ())), preferred_element_type=F32)
        s = jnp.where(valid, s, NEG)
        m = jnp.max(s, axis=-1, keepdims=True)
        p = jnp.exp(s - m)
        den = jnp.sum(p, axis=-1, keepdims=True)
        vh = jnp.where(lane_v == h, vv, jnp.zeros_like(vv))
        acc = acc + jnp.dot(p.astype(BF16), vh, preferred_element_type=F32)
        den_full = jnp.where(lane_o == h, den, den_full)
        lse_full = jnp.where(lane_o == h, m + jnp.log(den), lse_full)
    o_ref[0] = acc / den_full
    lse_ref[0] = lse_full


def _band_attention(q, k, v, bsz, seq, dil):
    ln = seq // dil
    w = DIL_GROUP_WIDTH
    view = lambda a: a.reshape(bsz, ln, dil * w)
    cur = pl.BlockSpec((1, BAND_BLOCK, w), lambda b, r, m: (b, m, r))
    prev = pl.BlockSpec((1, BAND_BLOCK, w), lambda b, r, m: (b, jnp.maximum(m - 1, 0), r))
    o, lse = pl.pallas_call(
        _band_kernel,
        out_shape=[jax.ShapeDtypeStruct((bsz, ln, dil * w), F32)] * 2,
        grid=(bsz, dil, ln // BAND_BLOCK),
        in_specs=[cur, prev, cur, prev, cur],
        out_specs=[cur, cur],
        compiler_params=_params(("parallel", "parallel", "parallel")),
        name=f"band_attn_d{dil}",
    )(view(q), view(k), view(k), view(v), view(v))
    return o.reshape(bsz * seq, w), lse.reshape(bsz * seq, w)


def _even_out_kernel(tps, u_ref, up_ref, o0, o1, o2, l0, l1, l2, wp_ref, ps_ref, wo_ref, x_ref, gate_ref,
                     g_ref, b_ref, out_ref):
    i = pl.program_id(0)
    tm = u_ref.shape[0]
    halo = up_ref.shape[0]
    u = u_ref[...]
    prev = jnp.where(i % tps == 0, jnp.zeros_like(up_ref[...]), up_ref[...])
    ext = jnp.concatenate([prev, u], axis=0)
    lane_g = lax.broadcasted_iota(jnp.int32, (tm, POOL_WIDTH), 1) // POOL_GROUP_DIM
    row = lax.broadcasted_iota(jnp.int32, (tm, POOL_WIDTH), 0)
    pos1 = ((i % tps) * tm + row + 1).astype(F32)
    s = ext
    pooled = jnp.zeros((tm, POOL_WIDTH), F32)
    wvec = jnp.zeros((tm, POOL_WIDTH), F32)
    shift = 1
    for g, w in enumerate(POOL_WINDOWS):
        while shift < w:
            s = s + pltpu.roll(s, shift, 0)
            shift *= 2
        pooled = jnp.where(lane_g == g, s[halo:, :], pooled)
        wvec = jnp.where(lane_g == g, float(w), wvec)
    pooled = pooled / jnp.minimum(pos1, wvec) - u
    a_out = jnp.dot(pooled.astype(BF16), wp_ref[...], preferred_element_type=F32) * ps_ref[...]
    la, lb, lc = l0[...], l1[...], l2[...]
    mx = jnp.maximum(jnp.maximum(la, lb), lc)
    ea, eb, ec = jnp.exp(la - mx), jnp.exp(lb - mx), jnp.exp(lc - mx)
    b_out = (ea * o0[...] + eb * o1[...] + ec * o2[...]) / (ea + eb + ec)
    mixed = jnp.concatenate([a_out, b_out], axis=-1).astype(BF16)
    sub = jnp.dot(mixed, wo_ref[...], preferred_element_type=F32)
    out_ref[...] = _deepnorm_ln(x_ref[...], sub, gate_ref[0], g_ref[...], b_ref[...])


def _even_out(u, os_, lses, wpool_bd, pool_scale, wo_bf, x2, gate, ln_g, ln_b, seq):
    t, d = x2.shape
    tm = TOKEN_TILE
    tps = seq // tm
    halo = 16
    row = lambda w: pl.BlockSpec((tm, w), lambda i: (i, 0))
    full = lambda a: pl.BlockSpec(a.shape, lambda i: (0,) * a.ndim)
    return pl.pallas_call(
        functools.partial(_even_out_kernel, tps),
        out_shape=jax.ShapeDtypeStruct((t, d), F32),
        grid=(t // tm,),
        in_specs=[
            row(POOL_WIDTH),
            pl.BlockSpec((halo, POOL_WIDTH), lambda i: (jnp.maximum(i * (tm // halo) - 1, 0), 0)),
            *[row(DIL_GROUP_WIDTH)] * 6,
            full(wpool_bd),
            full(pool_scale),
            full(wo_bf),
            row(d),
            pl.BlockSpec((1, 1, d), lambda i: (i // tps, 0, 0)),
            full(ln_g),
            full(ln_b),
        ],
        out_specs=row(d),
        compiler_params=_params(("parallel",)),
        name="even_out_proj",
    )(u, u, *os_, *lses, wpool_bd, pool_scale, wo_bf, x2, gate, ln_g, ln_b)


def _ffn_kernel(n_exp, x_ref, sc_ref, sh_ref, comb_ref, wg_ref, wu_ref, wd_ref, gate_ref, g_ref, b_ref,
                out_ref, h_ref, acc_ref):
    e = pl.program_id(1)
    c = pl.program_id(2)

    @pl.when((e == 0) & (c == 0))
    def _():
        h_ref[...] = _modulate(x_ref[...], sc_ref[0], sh_ref[0])
        acc_ref[...] = jnp.zeros_like(acc_ref)

    h = h_ref[...]
    gt = jnp.dot(h, wg_ref[0], preferred_element_type=F32)
    up = jnp.dot(h, wu_ref[0], preferred_element_type=F32)
    a = (gt * jax.nn.sigmoid(gt) * up).astype(BF16)
    y = jnp.dot(a, wd_ref[0], preferred_element_type=F32)
    if n_exp > 1:
        lane = lax.broadcasted_iota(jnp.int32, comb_ref.shape, 1)
        y = y * jnp.sum(jnp.where(lane == e, comb_ref[...], 0.0), axis=-1, keepdims=True)
    acc_ref[...] += y

    @pl.when((e == n_exp - 1) & (c == pl.num_programs(2) - 1))
    def _():
        out_ref[...] = _deepnorm_ln(x_ref[...], acc_ref[...], gate_ref[0], g_ref[...], b_ref[...])


def _ffn(x2, sc, sh, comb, wgu_bf, wd_bf, gate, ln_g, ln_b, seq):
    t, d = x2.shape
    n_exp = wgu_bf.shape[0]
    tm = TOKEN_TILE
    tps = seq // tm
    nch = D_FF // FF_CHUNK
    mod = pl.BlockSpec((1, 1, d), lambda i, e, c: (i // tps, 0, 0))
    row = lambda w: pl.BlockSpec((tm, w), lambda i, e, c: (i, 0))
    vec = pl.BlockSpec((1, d), lambda i, e, c: (0, 0))
    return pl.pallas_call(
        functools.partial(_ffn_kernel, n_exp),
        out_shape=jax.ShapeDtypeStruct((t, d), F32),
        grid=(t // tm, n_exp, nch),
        in_specs=[
            row(d),
            mod,
            mod,
            row(LANES),
            pl.BlockSpec((1, d, FF_CHUNK), lambda i, e, c: (e, 0, c)),
            pl.BlockSpec((1, d, FF_CHUNK), lambda i, e, c: (e, 0, nch + c)),
            pl.BlockSpec((1, FF_CHUNK, d), lambda i, e, c: (e, c, 0)),
            mod,
            vec,
            vec,
        ],
        out_specs=row(d),
        scratch_shapes=[pltpu.VMEM((tm, d), BF16), pltpu.VMEM((tm, d), F32)],
        compiler_params=_params(("parallel", "arbitrary", "arbitrary")),
        name=f"ffn_e{n_exp}",
    )(x2, sc, sh, comb, wgu_bf, wgu_bf, wd_bf, gate, ln_g, ln_b)


def _split3(x):
    hi = x.astype(BF16)
    r = x - hi.astype(F32)
    mid = r.astype(BF16)
    lo = (r - mid.astype(F32)).astype(BF16)
    return hi, mid, lo


def _odd_in_kernel(tps, x_ref, sc_ref, sh_ref, w_ref, wf_ref, bf_ref, place_ref, q_ref, k_ref, v_ref, f_ref,
                   carry_ref):
    i = pl.program_id(0)
    tm = x_ref.shape[0]
    h = _modulate(x_ref[...], sc_ref[0], sh_ref[0])
    q = jnp.dot(h, w_ref[:, 0:FOX_WIDTH], preferred_element_type=F32)
    q_ref[...] = (q * (HEAD_DIM ** -0.5 * LOG2E)).astype(BF16)
    k_ref[...] = jnp.dot(h, w_ref[:, FOX_WIDTH:2 * FOX_WIDTH], preferred_element_type=F32).astype(BF16)
    v_ref[...] = jnp.dot(h, w_ref[:, 2 * FOX_WIDTH:3 * FOX_WIDTH], preferred_element_type=F32).astype(BF16)
    f = jnp.dot(h, wf_ref[...], preferred_element_type=F32) + bf_ref[...]
    logf = jnp.minimum(f, 0.0) - jnp.log(1.0 + jnp.exp(-jnp.abs(f)))
    r = lax.broadcasted_iota(jnp.int32, (tm, tm), 0)
    cidx = lax.broadcasted_iota(jnp.int32, (tm, tm), 1)
    tri = jnp.where(cidx <= r, 1.0, 0.0).astype(BF16)
    hi, mid, lo = _split3(logf)
    cs = (jnp.dot(tri, lo, preferred_element_type=F32) + jnp.dot(tri, mid, preferred_element_type=F32)
          + jnp.dot(tri, hi, preferred_element_type=F32))

    @pl.when(i % tps == 0)
    def _():
        carry_ref[...] = jnp.zeros_like(carry_ref)

    cs = cs + carry_ref[...]
    carry_ref[...] = cs[tm - 1:tm, :]
    parts = jnp.concatenate(_split3(cs * LOG2E), axis=1)
    f_ref[...] = jnp.dot(parts, place_ref[...], preferred_element_type=F32).astype(BF16)


def _gate_placement():
    place = np.zeros((3 * LANES, FOX_WIDTH), np.float32)
    for h in range(FOX_HEADS):
        base = (h // 2) * LANES + (HEAD_DIM if h % 2 == 0 else 0)
        for p in range(3):
            place[p * LANES + h, base + p] = 1.0
    return jnp.asarray(place, BF16)


def _odd_in(x2, sc, sh, w_bf, wf_bf, bf_pad, seq):
    t, d = x2.shape
    tm = TOKEN_TILE
    tps = seq // tm
    mod = pl.BlockSpec((1, 1, d), lambda i: (i // tps, 0, 0))
    row = lambda w: pl.BlockSpec((tm, w), lambda i: (i, 0))
    full = lambda a: pl.BlockSpec(a.shape, lambda i: (0,) * a.ndim)
    place = _gate_placement()
    return pl.pallas_call(
        functools.partial(_odd_in_kernel, tps),
        out_shape=[jax.ShapeDtypeStruct((t, FOX_WIDTH), BF16)] * 4,
        grid=(t // tm,),
        in_specs=[row(d), mod, mod, full(w_bf), full(wf_bf), full(bf_pad), full(place)],
        out_specs=[row(FOX_WIDTH)] * 4,
        scratch_shapes=[pltpu.VMEM((1, LANES), F32)],
        compiler_params=_params(("arbitrary",)),
        name="odd_in_proj",
    )(x2, sc, sh, w_bf, wf_bf, bf_pad, place)


def _fox_kernel(q_ref, k_ref, v_ref, f_ref, o_ref, acc_ref, m_ref, s0_ref, s1_ref, p0_ref, p1_ref):
    qi = pl.program_id(2)
    tq = q_ref.shape[1]
    tk = FOX_K_TILE
    half = HEAD_DIM
    q = q_ref[0]
    lane_q = lax.broadcasted_iota(jnp.int32, q.shape, 1)
    lane_k = lax.broadcasted_iota(jnp.int32, (tk, LANES), 1)
    qa = jnp.where(lane_q < half, q, jnp.where(lane_q < half + 3, -1.0, 0.0).astype(BF16))
    qb = jnp.where(lane_q >= half, q, jnp.where(lane_q < 3, -1.0, 0.0).astype(BF16))
    ones_a = jnp.where(lane_k == half, 1.0, 0.0).astype(BF16)
    ones_b = jnp.where(lane_k == 0, 1.0, 0.0).astype(BF16)
    acc_ref[...] = jnp.zeros_like(acc_ref)
    m_ref[...] = jnp.full(m_ref.shape, NEG, F32)

    def scores(j, s_ref):
        start = pl.multiple_of(j * tk, tk)
        k = k_ref[0, pl.ds(start, tk), :]
        fa = f_ref[0, pl.ds(start, tk), :]
        ka = jnp.where(lane_k < half, k, fa)
        kb = jnp.where(lane_k >= half, k, fa)
        for hh, (qh, kh) in enumerate(((qa, ka), (qb, kb))):
            s_ref[hh] = lax.dot_general(qh, kh, (((1,), (1,)), ((), ())), preferred_element_type=F32)

    def attend(j, s_ref, p_ref, masked):
        start = pl.multiple_of(j * tk, tk)
        v = v_ref[0, pl.ds(start, tk), :]
        va = jnp.where(lane_k < half, v, ones_a)
        vb = jnp.where(lane_k >= half, v, ones_b)
        for hh, vh in enumerate((va, vb)):
            for r0 in range(0, tq, FOX_ROW_CHUNK):
                rows = pl.ds(r0, FOX_ROW_CHUNK)
                s = s_ref[hh, rows, :]
                if masked:
                    rr = lax.broadcasted_iota(jnp.int32, s.shape, 0) + (qi * tq + r0)
                    cc = lax.broadcasted_iota(jnp.int32, s.shape, 1) + start
                    s = jnp.where(cc <= rr, s, NEG)
                m_old = m_ref[hh, rows, :]
                m_new = jnp.maximum(m_old, jnp.max(s, axis=-1, keepdims=True))
                p_ref[hh, rows, :] = jnp.exp2(s - jnp.tile(m_new, (1, tk // LANES))).astype(BF16)
                acc_ref[hh, rows, :] = acc_ref[hh, rows, :] * jnp.exp2(m_old - m_new)
                m_ref[hh, rows, :] = m_new
            acc_ref[hh] += jnp.dot(p_ref[hh], vh, preferred_element_type=F32)

    scores(0, s0_ref)

    def body(jp, carry):
        scores(2 * jp + 1, s1_ref)
        attend(2 * jp, s0_ref, p0_ref, False)
        scores(2 * jp + 2, s0_ref)
        attend(2 * jp + 1, s1_ref, p1_ref, False)
        return carry

    lax.fori_loop(0, qi // 2, body, 0)

    @pl.when(qi % 2 == 0)
    def _():
        attend(qi, s0_ref, p0_ref, True)

    @pl.when(qi % 2 == 1)
    def _():
        scores(qi, s1_ref)
        attend(qi - 1, s0_ref, p0_ref, False)
        attend(qi, s1_ref, p1_ref, True)

    acc_a, acc_b = acc_ref[0], acc_ref[1]
    lane_o = lax.broadcasted_iota(jnp.int32, (tq, LANES), 1)
    o = jnp.where(lane_o < half, acc_a / acc_a[:, half:half + 1], acc_b / acc_b[:, 0:1])
    o_ref[0] = o.astype(BF16)


def _fox_attention(q, k, v, fa, bsz, seq):
    tq = FOX_Q_TILE
    npairs = FOX_HEADS // 2
    tile = pl.BlockSpec((1, tq, LANES), lambda b, j, i: (b, i, j))
    whole = pl.BlockSpec((1, seq, LANES), lambda b, j, i: (b, 0, j))
    return pl.pallas_call(
        _fox_kernel,
        out_shape=jax.ShapeDtypeStruct((bsz, seq, FOX_WIDTH), BF16),
        grid=(bsz, npairs, seq // tq),
        in_specs=[tile, whole, whole, whole],
        out_specs=tile,
        scratch_shapes=[pltpu.VMEM((2, tq, LANES), F32), pltpu.VMEM((2, tq, LANES), F32),
                        pltpu.VMEM((2, tq, FOX_K_TILE), F32), pltpu.VMEM((2, tq, FOX_K_TILE), F32),
                        pltpu.VMEM((2, tq, FOX_K_TILE), BF16), pltpu.VMEM((2, tq, FOX_K_TILE), BF16)],
        compiler_params=_params(("parallel", "parallel", "arbitrary")),
        name="fox_attn",
    )(q, k, v, fa)


def _odd_out_kernel(o_ref, wo_ref, x_ref, gate_ref, g_ref, b_ref, out_ref):
    sub = jnp.dot(o_ref[...], wo_ref[...], preferred_element_type=F32)
    out_ref[...] = _deepnorm_ln(x_ref[...], sub, gate_ref[0], g_ref[...], b_ref[...])


def _odd_out(o2, wo_bf, x2, gate, ln_g, ln_b, seq):
    t, d = x2.shape
    tm = TOKEN_TILE
    tps = seq // tm
    row = lambda w: pl.BlockSpec((tm, w), lambda i: (i, 0))
    full = lambda a: pl.BlockSpec(a.shape, lambda i: (0,) * a.ndim)
    return pl.pallas_call(
        _odd_out_kernel,
        out_shape=jax.ShapeDtypeStruct((t, d), F32),
        grid=(t // tm,),
        in_specs=[row(FOX_WIDTH), full(wo_bf), row(d), pl.BlockSpec((1, 1, d), lambda i: (i // tps, 0, 0)),
                  full(ln_g), full(ln_b)],
        out_specs=row(d),
        compiler_params=_params(("parallel",)),
        name="odd_out_proj",
    )(o2, wo_bf, x2, gate, ln_g, ln_b)


def _router_kernel(x_ref, sc_ref, sh_ref, wr_ref, comb_ref):
    h = _modulate(x_ref[...], sc_ref[0], sh_ref[0])
    logits = jnp.dot(h, wr_ref[...], preferred_element_type=F32)
    lane = lax.broadcasted_iota(jnp.int32, logits.shape, 1)
    logits = jnp.where(lane < N_EXPERTS, logits, -jnp.inf)
    m1 = jnp.max(logits, axis=-1, keepdims=True)
    i1 = jnp.min(jnp.where(logits == m1, lane, LANES), axis=-1, keepdims=True)
    rest = jnp.where(lane == i1, -jnp.inf, logits)
    m2 = jnp.max(rest, axis=-1, keepdims=True)
    i2 = jnp.min(jnp.where(rest == m2, lane, LANES), axis=-1, keepdims=True)
    e2 = jnp.exp(m2 - m1)
    g1 = 1.0 / (1.0 + e2)
    g2 = e2 / (1.0 + e2)
    comb_ref[...] = jnp.where(lane == i1, g1, 0.0) + jnp.where(lane == i2, g2, 0.0)


def _router(x2, sc, sh, wr_pad, seq):
    t, d = x2.shape
    tm = TOKEN_TILE
    tps = seq // tm
    mod = pl.BlockSpec((1, 1, d), lambda i: (i // tps, 0, 0))
    return pl.pallas_call(
        _router_kernel,
        out_shape=jax.ShapeDtypeStruct((t, LANES), F32),
        grid=(t // tm,),
        in_specs=[pl.BlockSpec((tm, d), lambda i: (i, 0)), mod, mod,
                  pl.BlockSpec(wr_pad.shape, lambda i: (0, 0))],
        out_specs=pl.BlockSpec((tm, LANES), lambda i: (i, 0)),
        compiler_params=_params(("parallel",)),
        name="router",
    )(x2, sc, sh, wr_pad)


def _block_diag(w_pool):
    g, c, _ = w_pool.shape
    out = jnp.zeros((g * c, g * c), w_pool.dtype)
    for i in range(g):
        out = out.at[i * c:(i + 1) * c, i * c:(i + 1) * c].set(w_pool[i])
    return out


def kernel(x, c, w_ada, b_ada, ln_g, ln_b, w_in_even, w_pool, pool_scale, w_out_even, w_ffn_gu, w_ffn_down,
           w_in_odd, b_forget, w_out_odd, w_router, w_exp_gu, w_exp_down):
    bsz, seq, d = x.shape
    t = bsz * seq
    tabs = _rope_tables(seq)
    ada = _ada(c, w_ada, b_ada)
    x2 = x.reshape(t, d)
    ones_comb = jnp.ones((t, LANES), F32)
    for l in range(DEPTH):
        i = l // 2
        sh1, sc1, g1, sh2, sc2, g2 = [ada[l, :, n * d:(n + 1) * d].reshape(bsz, 1, d) for n in range(6)]
        lng = ln_g[l].reshape(2, 1, d)
        lnb = ln_b[l].reshape(2, 1, d)
        if l % 2 == 0:
            outs = _even_in(x2, sc1, sh1, w_in_even[i].astype(BF16), tabs, seq)
            u, qkv = outs[0], outs[1:]
            os_, lses = [], []
            for g, (_, dil) in enumerate(DIL_CONFIGS):
                o, lse = _band_attention(qkv[g], qkv[3 + g], qkv[6 + g], bsz, seq, dil)
                os_.append(o)
                lses.append(lse)
            x2 = _even_out(u, os_, lses, _block_diag(w_pool[i]).astype(BF16), pool_scale[i].reshape(1, -1),
                           w_out_even[i].astype(BF16), x2, g1, lng[0], lnb[0], seq)
            x2 = _ffn(x2, sc2, sh2, ones_comb, w_ffn_gu[i][None].astype(BF16), w_ffn_down[i][None].astype(BF16),
                      g2, lng[1], lnb[1], seq)
        else:
            w_in = w_in_odd[i]
            wf = jnp.zeros((d, LANES), F32).at[:, :FOX_HEADS].set(w_in[:, 3 * FOX_WIDTH:]).astype(BF16)
            bfp = jnp.zeros((1, LANES), F32).at[0, :FOX_HEADS].set(b_forget[i])
            q, k, v, fa = [a.reshape(bsz, seq, FOX_WIDTH) for a in
                           _odd_in(x2, sc1, sh1, w_in[:, :3 * FOX_WIDTH].astype(BF16), wf, bfp, seq)]
            o = _fox_attention(q, k, v, fa, bsz, seq)
            x2 = _odd_out(o.reshape(t, FOX_WIDTH), w_out_odd[i].astype(BF16), x2, g1, lng[0], lnb[0], seq)
            wr = jnp.zeros((d, LANES), F32).at[:, :N_EXPERTS].set(w_router[i]).astype(BF16)
            comb = _router(x2, sc2, sh2, wr, seq)
            x2 = _ffn(x2, sc2, sh2, comb, w_exp_gu[i].astype(BF16), w_exp_down[i].astype(BF16), g2, lng[1],
                      lnb[1], seq)
    return x2.reshape(bsz, seq, d)
```

```python
import functools
import math

import numpy as np

import jax
import jax.numpy as jnp
from jax import lax
from jax.experimental import pallas as pl
from jax.experimental.pallas import tpu as pltpu

F32 = jnp.float32
BF16 = jnp.bfloat16

D_MODEL = 1024
DEPTH = 4
HEAD_DIM = 64
POOL_WINDOWS = (2, 4, 8, 16)
POOL_GROUP_DIM = 64
POOL_WIDTH = 256
DIL_CONFIGS = ((128, 1), (512, 4), (2048, 16))
DIL_HEADS = 4
DIL_GROUP_WIDTH = DIL_HEADS * HEAD_DIM
BAND_BLOCK = 128
FOX_HEADS = 16
FOX_WIDTH = 1024
ROPE_THETA = 500000.0
ROPE_DIM = 16
D_FF = 2816
N_EXPERTS = 8
DEEPNORM_ALPHA = (2 * DEPTH) ** 0.25
LN_EPS = 1e-5
NEG = -1e30
LOG2E = 1.4426950408889634

LANES = 128
TOKEN_TILE = 512
FF_CHUNK = 1408
FOX_Q_TILE = 512
FOX_K_TILE = 512
FOX_ROW_CHUNK = 64
MOE_CHUNK = 512
MOE_TILE = 512
ROW_BLOCK = 128
MAX_SPAN = MOE_CHUNK // ROW_BLOCK + 1
FF_SUB = 256
NO_ROW = -1e9
VMEM_LIMIT = 56 * 1024 * 1024


def _params(sem, vmem=VMEM_LIMIT):
    return pltpu.CompilerParams(dimension_semantics=sem, vmem_limit_bytes=vmem)


def _modulate(x, sc, sh):
    return (x * (1.0 + sc) + sh).astype(BF16)


def _deepnorm_ln(x, sub, gate, g, b):
    y = DEEPNORM_ALPHA * x + gate * sub
    mu = jnp.mean(y, axis=-1, keepdims=True)
    yc = y - mu
    var = jnp.mean(yc * yc, axis=-1, keepdims=True)
    return yc * lax.rsqrt(var + LN_EPS) * g + b


def _ada_kernel(c_ref, w_ref, b_ref, o_ref):
    c = c_ref[...]
    ca = (c * jax.nn.sigmoid(c)).astype(BF16)
    o_ref[0] = jnp.dot(ca, w_ref[0].astype(BF16), preferred_element_type=F32) + b_ref[0]


def _ada(c, w_ada, b_ada):
    depth, d, n = w_ada.shape
    bsz = c.shape[0]
    tn = 1024
    return pl.pallas_call(
        _ada_kernel,
        out_shape=jax.ShapeDtypeStruct((depth, bsz, n), F32),
        grid=(depth, n // tn),
        in_specs=[
            pl.BlockSpec((bsz, d), lambda l, j: (0, 0)),
            pl.BlockSpec((1, d, tn), lambda l, j: (l, 0, j)),
            pl.BlockSpec((1, 1, tn), lambda l, j: (l, 0, j)),
        ],
        out_specs=pl.BlockSpec((1, bsz, tn), lambda l, j: (l, 0, j)),
        compiler_params=_params(("parallel", "parallel")),
        name="ada_mod",
    )(c, w_ada, b_ada.reshape(depth, 1, n))


def _even_in_kernel(x_ref, sc_ref, sh_ref, w_ref, ct_ref, sa_ref, sb_ref, u_ref, *qkv_refs):
    h = _modulate(x_ref[...], sc_ref[0], sh_ref[0])
    u_ref[...] = jnp.dot(h, w_ref[:, 0:POOL_WIDTH], preferred_element_type=F32)
    ct, sa, sb = ct_ref[...], sa_ref[...], sb_ref[...]
    for n in range(9):
        lo = POOL_WIDTH + n * DIL_GROUP_WIDTH
        z = jnp.dot(h, w_ref[:, lo:lo + DIL_GROUP_WIDTH], preferred_element_type=F32)
        if n < 6:
            z = z * ct + pltpu.roll(z, 8, 1) * sa + pltpu.roll(z, DIL_GROUP_WIDTH - 8, 1) * sb
        if n < 3:
            z = z * (HEAD_DIM ** -0.5)
        qkv_refs[n][...] = z.astype(BF16)


def _even_in(x2, sc, sh, w_bf, tabs, seq):
    t, d = x2.shape
    tm = TOKEN_TILE
    tps = seq // tm
    n_in = w_bf.shape[1]
    mod_spec = pl.BlockSpec((1, 1, d), lambda i: (i // tps, 0, 0))
    tab_spec = pl.BlockSpec((tm, DIL_GROUP_WIDTH), lambda i: (i % tps, 0))
    out_spec = pl.BlockSpec((tm, DIL_GROUP_WIDTH), lambda i: (i, 0))
    return pl.pallas_call(
        _even_in_kernel,
        out_shape=[jax.ShapeDtypeStruct((t, POOL_WIDTH), F32)]
        + [jax.ShapeDtypeStruct((t, DIL_GROUP_WIDTH), BF16)] * 9,
        grid=(t // tm,),
        in_specs=[
            pl.BlockSpec((tm, d), lambda i: (i, 0)),
            mod_spec,
            mod_spec,
            pl.BlockSpec((d, n_in), lambda i: (0, 0)),
            tab_spec,
            tab_spec,
            tab_spec,
        ],
        out_specs=[out_spec] * 10,
        compiler_params=_params(("parallel",)),
        name="even_in_proj",
    )(x2, sc, sh, w_bf, *tabs)


def _rope_tables(seq):
    pos = jnp.arange(seq, dtype=F32)
    inv = ROPE_THETA ** (-jnp.arange(0, ROPE_DIM, 2, dtype=F32) / ROPE_DIM)
    ang = pos[:, None] * inv[None, :]
    cos, sin = jnp.cos(ang), jnp.sin(ang)
    half = ROPE_DIM // 2
    zeros = jnp.zeros((seq, half), F32)
    rest = HEAD_DIM - ROPE_DIM
    ct = jnp.concatenate([cos, cos, jnp.ones((seq, rest), F32)], axis=1)
    sa = jnp.concatenate([zeros, sin, jnp.zeros((seq, rest), F32)], axis=1)
    sb = jnp.concatenate([-sin, zeros, jnp.zeros((seq, rest), F32)], axis=1)
    return tuple(jnp.tile(a, (1, DIL_HEADS)) for a in (ct, sa, sb))


def _band_kernel(q_ref, kp_ref, kc_ref, vp_ref, vc_ref, o_ref, lse_ref):
    mb = pl.program_id(2)
    blk = BAND_BLOCK
    q = q_ref[0]
    kk = jnp.concatenate([kp_ref[0], kc_ref[0]], axis=0)
    vv = jnp.concatenate([vp_ref[0], vc_ref[0]], axis=0)
    qi = lax.broadcasted_iota(jnp.int32, (blk, 2 * blk), 0)
    kj = lax.broadcasted_iota(jnp.int32, (blk, 2 * blk), 1)
    dist = qi + blk - kj
    first_key = jnp.where(mb > 0, 0, blk)
    valid = (dist >= 0) & (dist <= blk) & (kj >= first_key)
    lane_q = lax.broadcasted_iota(jnp.int32, q.shape, 1) // HEAD_DIM
    lane_v = lax.broadcasted_iota(jnp.int32, vv.shape, 1) // HEAD_DIM
    lane_o = lax.broadcasted_iota(jnp.int32, (blk, DIL_GROUP_WIDTH), 1) // HEAD_DIM
    acc = jnp.zeros((blk, DIL_GROUP_WIDTH), F32)
    den_full = jnp.ones((blk, DIL_GROUP_WIDTH), F32)
    lse_full = jnp.zeros((blk, DIL_GROUP_WIDTH), F32)
    for h in range(DIL_HEADS):
        qh = jnp.where(lane_q == h, q, jnp.zeros_like(q))
        s = lax.dot_general(qh, kk, (((1,), (1,)), ((), ())), preferred_element_type=F32)
        s = jnp.where(valid, s, NEG)
        m = jnp.max(s, axis=-1, keepdims=True)
        p = jnp.exp(s - m)
        den = jnp.sum(p, axis=-1, keepdims=True)
        vh = jnp.where(lane_v == h, vv, jnp.zeros_like(vv))
        acc = acc + jnp.dot(p.astype(BF16), vh, preferred_element_type=F32)
        den_full = jnp.where(lane_o == h, den, den_full)
        lse_full = jnp.where(lane_o == h, m + jnp.log(den), lse_full)
    o_ref[0] = acc / den_full
    lse_ref[0] = lse_full


def _band_attention(q, k, v, bsz, seq, dil):
    ln = seq // dil
    w = DIL_GROUP_WIDTH
    view = lambda a: a.reshape(bsz, ln, dil * w)
    cur = pl.BlockSpec((1, BAND_BLOCK, w), lambda b, r, m: (b, m, r))
    prev = pl.BlockSpec((1, BAND_BLOCK, w), lambda b, r, m: (b, jnp.maximum(m - 1, 0), r))
    o, lse = pl.pallas_call(
        _band_kernel,
        out_shape=[jax.ShapeDtypeStruct((bsz, ln, dil * w), F32)] * 2,
        grid=(bsz, dil, ln // BAND_BLOCK),
        in_specs=[cur, prev, cur, prev, cur],
        out_specs=[cur, cur],
        compiler_params=_params(("parallel", "parallel", "parallel")),
        name=f"band_attn_d{dil}",
    )(view(q), view(k), view(k), view(v), view(v))
    return o.reshape(bsz * seq, w), lse.reshape(bsz * seq, w)


def _even_out_kernel(tps, u_ref, up_ref, o0, o1, o2, l0, l1, l2, wp_ref, ps_ref, wo_ref, x_ref, gate_ref,
                     g_ref, b_ref, out_ref):
    i = pl.program_id(0)
    tm = u_ref.shape[0]
    halo = up_ref.shape[0]
    u = u_ref[...]
    prev = jnp.where(i % tps == 0, jnp.zeros_like(up_ref[...]), up_ref[...])
    ext = jnp.concatenate([prev, u], axis=0)
    lane_g = lax.broadcasted_iota(jnp.int32, (tm, POOL_WIDTH), 1) // POOL_GROUP_DIM
    row = lax.broadcasted_iota(jnp.int32, (tm, POOL_WIDTH), 0)
    pos1 = ((i % tps) * tm + row + 1).astype(F32)
    s = ext
    pooled = jnp.zeros((tm, POOL_WIDTH), F32)
    wvec = jnp.zeros((tm, POOL_WIDTH), F32)
    shift = 1
    for g, w in enumerate(POOL_WINDOWS):
        while shift < w:
            s = s + pltpu.roll(s, shift, 0)
            shift *= 2
        pooled = jnp.where(lane_g == g, s[halo:, :], pooled)
        wvec = jnp.where(lane_g == g, float(w), wvec)
    pooled = pooled / jnp.minimum(pos1, wvec) - u
    a_out = jnp.dot(pooled.astype(BF16), wp_ref[...], preferred_element_type=F32) * ps_ref[...]
    la, lb, lc = l0[...], l1[...], l2[...]
    mx = jnp.maximum(jnp.maximum(la, lb), lc)
    ea, eb, ec = jnp.exp(la - mx), jnp.exp(lb - mx), jnp.exp(lc - mx)
    b_out = (ea * o0[...] + eb * o1[...] + ec * o2[...]) / (ea + eb + ec)
    mixed = jnp.concatenate([a_out, b_out], axis=-1).astype(BF16)
    sub = jnp.dot(mixed, wo_ref[...], preferred_element_type=F32)
    out_ref[...] = _deepnorm_ln(x_ref[...], sub, gate_ref[0], g_ref[...], b_ref[...])


def _even_out(u, os_, lses, wpool_bd, pool_scale, wo_bf, x2, gate, ln_g, ln_b, seq):
    t, d = x2.shape
    tm = TOKEN_TILE
    tps = seq // tm
    halo = 16
    row = lambda w: pl.BlockSpec((tm, w), lambda i: (i, 0))
    full = lambda a: pl.BlockSpec(a.shape, lambda i: (0,) * a.ndim)
    return pl.pallas_call(
        functools.partial(_even_out_kernel, tps),
        out_shape=jax.ShapeDtypeStruct((t, d), F32),
        grid=(t // tm,),
        in_specs=[
            row(POOL_WIDTH),
            pl.BlockSpec((halo, POOL_WIDTH), lambda i: (jnp.maximum(i * (tm // halo) - 1, 0), 0)),
            *[row(DIL_GROUP_WIDTH)] * 6,
            full(wpool_bd),
            full(pool_scale),
            full(wo_bf),
            row(d),
            pl.BlockSpec((1, 1, d), lambda i: (i // tps, 0, 0)),
            full(ln_g),
            full(ln_b),
        ],
        out_specs=row(d),
        compiler_params=_params(("parallel",)),
        name="even_out_proj",
    )(u, u, *os_, *lses, wpool_bd, pool_scale, wo_bf, x2, gate, ln_g, ln_b)


def _ffn_kernel(n_exp, x_ref, sc_ref, sh_ref, comb_ref, wg_ref, wu_ref, wd_ref, gate_ref, g_ref, b_ref,
                out_ref, h_ref, acc_ref):
    e = pl.program_id(1)
    c = pl.program_id(2)

    @pl.when((e == 0) & (c == 0))
    def _():
        h_ref[...] = _modulate(x_ref[...], sc_ref[0], sh_ref[0])
        acc_ref[...] = jnp.zeros_like(acc_ref)

    h = h_ref[...]
    gt = jnp.dot(h, wg_ref[0], preferred_element_type=F32)
    up = jnp.dot(h, wu_ref[0], preferred_element_type=F32)
    a = (gt * jax.nn.sigmoid(gt) * up).astype(BF16)
    y = jnp.dot(a, wd_ref[0], preferred_element_type=F32)
    if n_exp > 1:
        lane = lax.broadcasted_iota(jnp.int32, comb_ref.shape, 1)
        y = y * jnp.sum(jnp.where(lane == e, comb_ref[...], 0.0), axis=-1, keepdims=True)
    acc_ref[...] += y

    @pl.when((e == n_exp - 1) & (c == pl.num_programs(2) - 1))
    def _():
        out_ref[...] = _deepnorm_ln(x_ref[...], acc_ref[...], gate_ref[0], g_ref[...], b_ref[...])


def _ffn(x2, sc, sh, comb, wgu_bf, wd_bf, gate, ln_g, ln_b, seq):
    t, d = x2.shape
    n_exp = wgu_bf.shape[0]
    tm = TOKEN_TILE
    tps = seq // tm
    nch = D_FF // FF_CHUNK
    mod = pl.BlockSpec((1, 1, d), lambda i, e, c: (i // tps, 0, 0))
    row = lambda w: pl.BlockSpec((tm, w), lambda i, e, c: (i, 0))
    vec = pl.BlockSpec((1, d), lambda i, e, c: (0, 0))
    return pl.pallas_call(
        functools.partial(_ffn_kernel, n_exp),
        out_shape=jax.ShapeDtypeStruct((t, d), F32),
        grid=(t // tm, n_exp, nch),
        in_specs=[
            row(d),
            mod,
            mod,
            row(LANES),
            pl.BlockSpec((1, d, FF_CHUNK), lambda i, e, c: (e, 0, c)),
            pl.BlockSpec((1, d, FF_CHUNK), lambda i, e, c: (e, 0, nch + c)),
            pl.BlockSpec((1, FF_CHUNK, d), lambda i, e, c: (e, c, 0)),
            mod,
            vec,
            vec,
        ],
        out_specs=row(d),
        scratch_shapes=[pltpu.VMEM((tm, d), BF16), pltpu.VMEM((tm, d), F32)],
        compiler_params=_params(("parallel", "arbitrary", "arbitrary")),
        name=f"ffn_e{n_exp}",
    )(x2, sc, sh, comb, wgu_bf, wgu_bf, wd_bf, gate, ln_g, ln_b)


def _split3(x):
    hi = x.astype(BF16)
    r = x - hi.astype(F32)
    mid = r.astype(BF16)
    lo = (r - mid.astype(F32)).astype(BF16)
    return hi, mid, lo


def _odd_in_kernel(tps, x_ref, sc_ref, sh_ref, w_ref, wf_ref, bf_ref, place_ref, q_ref, k_ref, v_ref, f_ref,
                   carry_ref):
    i = pl.program_id(0)
    tm = x_ref.shape[0]
    h = _modulate(x_ref[...], sc_ref[0], sh_ref[0])
    q = jnp.dot(h, w_ref[:, 0:FOX_WIDTH], preferred_element_type=F32)
    q_ref[...] = (q * (HEAD_DIM ** -0.5 * LOG2E)).astype(BF16)
    k_ref[...] = jnp.dot(h, w_ref[:, FOX_WIDTH:2 * FOX_WIDTH], preferred_element_type=F32).astype(BF16)
    v_ref[...] = jnp.dot(h, w_ref[:, 2 * FOX_WIDTH:3 * FOX_WIDTH], preferred_element_type=F32).astype(BF16)
    f = jnp.dot(h, wf_ref[...], preferred_element_type=F32) + bf_ref[...]
    logf = jnp.minimum(f, 0.0) - jnp.log(1.0 + jnp.exp(-jnp.abs(f)))
    r = lax.broadcasted_iota(jnp.int32, (tm, tm), 0)
    cidx = lax.broadcasted_iota(jnp.int32, (tm, tm), 1)
    tri = jnp.where(cidx <= r, 1.0, 0.0).astype(BF16)
    hi, mid, lo = _split3(logf)
    cs = (jnp.dot(tri, lo, preferred_element_type=F32) + jnp.dot(tri, mid, preferred_element_type=F32)
          + jnp.dot(tri, hi, preferred_element_type=F32))

    @pl.when(i % tps == 0)
    def _():
        carry_ref[...] = jnp.zeros_like(carry_ref)

    cs = cs + carry_ref[...]
    carry_ref[...] = cs[tm - 1:tm, :]
    parts = jnp.concatenate(_split3(cs * LOG2E), axis=1)
    f_ref[...] = jnp.dot(parts, place_ref[...], preferred_element_type=F32).astype(BF16)


def _gate_placement():
    place = np.zeros((3 * LANES, FOX_WIDTH), np.float32)
    for h in range(FOX_HEADS):
        base = (h // 2) * LANES + (HEAD_DIM if h % 2 == 0 else 0)
        for p in range(3):
            place[p * LANES + h, base + p] = 1.0
    return jnp.asarray(place, BF16)


def _odd_in(x2, sc, sh, w_bf, wf_bf, bf_pad, seq):
    t, d = x2.shape
    tm = TOKEN_TILE
    tps = seq // tm
    mod = pl.BlockSpec((1, 1, d), lambda i: (i // tps, 0, 0))
    row = lambda w: pl.BlockSpec((tm, w), lambda i: (i, 0))
    full = lambda a: pl.BlockSpec(a.shape, lambda i: (0,) * a.ndim)
    place = _gate_placement()
    return pl.pallas_call(
        functools.partial(_odd_in_kernel, tps),
        out_shape=[jax.ShapeDtypeStruct((t, FOX_WIDTH), BF16)] * 4,
        grid=(t // tm,),
        in_specs=[row(d), mod, mod, full(w_bf), full(wf_bf), full(bf_pad), full(place)],
        out_specs=[row(FOX_WIDTH)] * 4,
        scratch_shapes=[pltpu.VMEM((1, LANES), F32)],
        compiler_params=_params(("arbitrary",)),
        name="odd_in_proj",
    )(x2, sc, sh, w_bf, wf_bf, bf_pad, place)


def _fox_kernel(q_ref, k_ref, v_ref, f_ref, o_ref, acc_ref, m_ref, s0_ref, s1_ref, p0_ref, p1_ref):
    qi = pl.program_id(2)
    tq = q_ref.shape[1]
    tk = FOX_K_TILE
    half = HEAD_DIM
    q = q_ref[0]
    lane_q = lax.broadcasted_iota(jnp.int32, q.shape, 1)
    lane_k = lax.broadcasted_iota(jnp.int32, (tk, LANES), 1)
    qa = jnp.where(lane_q < half, q, jnp.where(lane_q < half + 3, -1.0, 0.0).astype(BF16))
    qb = jnp.where(lane_q >= half, q, jnp.where(lane_q < 3, -1.0, 0.0).astype(BF16))
    ones_a = jnp.where(lane_k == half, 1.0, 0.0).astype(BF16)
    ones_b = jnp.where(lane_k == 0, 1.0, 0.0).astype(BF16)
    acc_ref[...] = jnp.zeros_like(acc_ref)
    m_ref[...] = jnp.full(m_ref.shape, NEG, F32)

    def scores(j, s_ref):
        start = pl.multiple_of(j * tk, tk)
        k = k_ref[0, pl.ds(start, tk), :]
        fa = f_ref[0, pl.ds(start, tk), :]
        ka = jnp.where(lane_k < half, k, fa)
        kb = jnp.where(lane_k >= half, k, fa)
        for hh, (qh, kh) in enumerate(((qa, ka), (qb, kb))):
            s_ref[hh] = lax.dot_general(qh, kh, (((1,), (1,)), ((), ())), preferred_element_type=F32)

    def attend(j, s_ref, p_ref, masked):
        start = pl.multiple_of(j * tk, tk)
        v = v_ref[0, pl.ds(start, tk), :]
        va = jnp.where(lane_k < half, v, ones_a)
        vb = jnp.where(lane_k >= half, v, ones_b)
        for hh, vh in enumerate((va, vb)):
            for r0 in range(0, tq, FOX_ROW_CHUNK):
                rows = pl.ds(r0, FOX_ROW_CHUNK)
                s = s_ref[hh, rows, :]
                if masked:
                    rr = lax.broadcasted_iota(jnp.int32, s.shape, 0) + (qi * tq + r0)
                    cc = lax.broadcasted_iota(jnp.int32, s.shape, 1) + start
                    s = jnp.where(cc <= rr, s, NEG)
                m_old = m_ref[hh, rows, :]
                m_new = jnp.maximum(m_old, jnp.max(s, axis=-1, keepdims=True))
                p_ref[hh, rows, :] = jnp.exp2(s - jnp.tile(m_new, (1, tk // LANES))).astype(BF16)
                acc_ref[hh, rows, :] = acc_ref[hh, rows, :] * jnp.exp2(m_old - m_new)
                m_ref[hh, rows, :] = m_new
            acc_ref[hh] += jnp.dot(p_ref[hh], vh, preferred_element_type=F32)

    scores(0, s0_ref)

    def body(jp, carry):
        scores(2 * jp + 1, s1_ref)
        attend(2 * jp, s0_ref, p0_ref, False)
        scores(2 * jp + 2, s0_ref)
        attend(2 * jp + 1, s1_ref, p1_ref, False)
        return carry

    lax.fori_loop(0, qi // 2, body, 0)

    @pl.when(qi % 2 == 0)
    def _():
        attend(qi, s0_ref, p0_ref, True)

    @pl.when(qi % 2 == 1)
    def _():
        scores(qi, s1_ref)
        attend(qi - 1, s0_ref, p0_ref, False)
        attend(qi, s1_ref, p1_ref, True)

    acc_a, acc_b = acc_ref[0], acc_ref[1]
    lane_o = lax.broadcasted_iota(jnp.int32, (tq, LANES), 1)
    o = jnp.where(lane_o < half, acc_a / acc_a[:, half:half + 1], acc_b / acc_b[:, 0:1])
    o_ref[0] = o.astype(BF16)


def _fox_attention(q, k, v, fa, bsz, seq):
    tq = FOX_Q_TILE
    npairs = FOX_HEADS // 2
    tile = pl.BlockSpec((1, tq, LANES), lambda b, j, i: (b, i, j))
    whole = pl.BlockSpec((1, seq, LANES), lambda b, j, i: (b, 0, j))
    return pl.pallas_call(
        _fox_kernel,
        out_shape=jax.ShapeDtypeStruct((bsz, seq, FOX_WIDTH), BF16),
        grid=(bsz, npairs, seq // tq),
        in_specs=[tile, whole, whole, whole],
        out_specs=tile,
        scratch_shapes=[pltpu.VMEM((2, tq, LANES), F32), pltpu.VMEM((2, tq, LANES), F32),
                        pltpu.VMEM((2, tq, FOX_K_TILE), F32), pltpu.VMEM((2, tq, FOX_K_TILE), F32),
                        pltpu.VMEM((2, tq, FOX_K_TILE), BF16), pltpu.VMEM((2, tq, FOX_K_TILE), BF16)],
        compiler_params=_params(("parallel", "parallel", "arbitrary")),
        name="fox_attn",
    )(q, k, v, fa)


def _odd_out_kernel(o_ref, wo_ref, x_ref, gate_ref, g_ref, b_ref, out_ref):
    sub = jnp.dot(o_ref[...], wo_ref[...], preferred_element_type=F32)
    out_ref[...] = _deepnorm_ln(x_ref[...], sub, gate_ref[0], g_ref[...], b_ref[...])


def _odd_out(o2, wo_bf, x2, gate, ln_g, ln_b, seq):
    t, d = x2.shape
    tm = TOKEN_TILE
    tps = seq // tm
    row = lambda w: pl.BlockSpec((tm, w), lambda i: (i, 0))
    full = lambda a: pl.BlockSpec(a.shape, lambda i: (0,) * a.ndim)
    return pl.pallas_call(
        _odd_out_kernel,
        out_shape=jax.ShapeDtypeStruct((t, d), F32),
        grid=(t // tm,),
        in_specs=[row(FOX_WIDTH), full(wo_bf), row(d), pl.BlockSpec((1, 1, d), lambda i: (i // tps, 0, 0)),
                  full(ln_g), full(ln_b)],
        out_specs=row(d),
        compiler_params=_params(("parallel",)),
        name="odd_out_proj",
    )(o2, wo_bf, x2, gate, ln_g, ln_b)


def _router_kernel(x_ref, sc_ref, sh_ref, wrt_ref, tri_ref, rankt_ref, cnt_ref, comb_ref, rank_ref):
    c = x_ref.shape[0]
    h = _modulate(x_ref[...], sc_ref[0], sh_ref[0])
    lt = lax.dot_general(wrt_ref[...], h, (((1,), (1,)), ((), ())), preferred_element_type=F32)[0:N_EXPERTS]
    row = lax.broadcasted_iota(jnp.int32, lt.shape, 0)
    m1 = jnp.max(lt, axis=0, keepdims=True)
    i1 = jnp.min(jnp.where(lt == m1, row, N_EXPERTS), axis=0, keepdims=True)
    rest = jnp.where(row == i1, -jnp.inf, lt)
    m2 = jnp.max(rest, axis=0, keepdims=True)
    i2 = jnp.min(jnp.where(rest == m2, row, N_EXPERTS), axis=0, keepdims=True)
    e2 = jnp.exp(m2 - m1)
    g1 = 1.0 / (1.0 + e2)
    g2 = e2 / (1.0 + e2)
    combt = jnp.where(row == i1, g1, jnp.where(row == i2, g2, 0.0))
    sel = jnp.where((row == i1) | (row == i2), 1.0, 0.0)
    sel16 = jnp.concatenate([sel, jnp.zeros_like(sel)], axis=0).astype(BF16)
    earlier = jnp.dot(sel16, tri_ref[...], preferred_element_type=F32)[0:N_EXPERTS]
    rankt = jnp.where(sel > 0.0, earlier, NO_ROW)
    rankt_ref[...] = rankt
    cnt_ref[0] = jnp.broadcast_to(jnp.sum(sel, axis=1, keepdims=True), (N_EXPERTS, LANES))
    pad = LANES - N_EXPERTS
    comb_ref[...] = jnp.concatenate([combt, jnp.zeros((pad, c), F32)], axis=0).T
    rank_ref[...] = jnp.concatenate([rankt, jnp.full((pad, c), NO_ROW, F32)], axis=0).T


def _router(x2, sc, sh, wrt, seq):
    t, d = x2.shape
    cs = MOE_CHUNK
    cps = seq // cs
    mod = pl.BlockSpec((1, 1, d), lambda i: (i // cps, 0, 0))
    tri = jnp.asarray(np.triu(np.ones((cs, cs), np.float32), 1), BF16)
    return pl.pallas_call(
        _router_kernel,
        out_shape=[jax.ShapeDtypeStruct((N_EXPERTS, t), F32),
                   jax.ShapeDtypeStruct((t // cs, N_EXPERTS, LANES), F32),
                   jax.ShapeDtypeStruct((t, LANES), F32),
                   jax.ShapeDtypeStruct((t, LANES), F32)],
        grid=(t // cs,),
        in_specs=[pl.BlockSpec((cs, d), lambda i: (i, 0)), mod, mod,
                  pl.BlockSpec(wrt.shape, lambda i: (0, 0)), pl.BlockSpec(tri.shape, lambda i: (0, 0))],
        out_specs=[pl.BlockSpec((N_EXPERTS, cs), lambda i: (0, i)),
                   pl.BlockSpec((1, N_EXPERTS, LANES), lambda i: (i, 0, 0)),
                   pl.BlockSpec((cs, LANES), lambda i: (i, 0)),
                   pl.BlockSpec((cs, LANES), lambda i: (i, 0))],
        compiler_params=_params(("parallel",)),
        name="router",
    )(x2, sc, sh, wrt, tri)


def _route_meta(cnt, n_tiles):
    before = jnp.cumsum(cnt, axis=0) - cnt
    total = jnp.sum(cnt, axis=0)
    region = ((total + MOE_TILE - 1) // MOE_TILE) * MOE_TILE
    base = jnp.cumsum(region) - region
    n_used = (jnp.sum(region) // MOE_TILE).reshape(1)
    starts = jnp.arange(n_tiles, dtype=jnp.int32) * MOE_TILE
    tile_expert = jnp.minimum(jnp.sum(starts[:, None] >= (base + region)[None, :], axis=1), N_EXPERTS - 1)
    i32 = lambda a: a.astype(jnp.int32)
    return i32(before.reshape(-1)), i32(cnt.reshape(-1)), i32(base), i32(total), i32(tile_expert), i32(n_used)


def _chunk_block(before_ref, cnt_ref, chunk, e, j):
    g = before_ref[chunk * N_EXPERTS + e]
    n = cnt_ref[chunk * N_EXPERTS + e]
    k = g // ROW_BLOCK + j
    return g, n, k, (n > 0) & (k * ROW_BLOCK < g + n)


def _dispatch_kernel(before_ref, cnt_ref, base_ref, total_ref, x_ref, sc_ref, sh_ref, rankt_ref, xs_ref,
                     stage_ref, slot_ref, zero_ref, sem, fsem, zsem, usem):
    c = pl.program_id(0)
    last = c == pl.num_programs(0) - 1
    h = _modulate(x_ref[...], sc_ref[0], sh_ref[0])

    @pl.when(c == 0)
    def _():
        stage_ref[...] = jnp.zeros_like(stage_ref)
        zero_ref[...] = jnp.zeros_like(zero_ref)

    block_row = lax.broadcasted_iota(jnp.int32, (ROW_BLOCK, MOE_CHUNK), 0).astype(F32)

    def block_copy(e, j, k):
        row0 = pl.multiple_of(base_ref[e] + k * ROW_BLOCK, ROW_BLOCK)
        return pltpu.make_async_copy(slot_ref.at[e, j], xs_ref.at[pl.ds(row0, ROW_BLOCK)], sem.at[e, j])

    for e in range(N_EXPERTS):
        rel = rankt_ref[e:e + 1, :]
        for j in range(MAX_SPAN):
            g, n, k, touched = _chunk_block(before_ref, cnt_ref, c, e, j)
            complete = g + n >= (k + 1) * ROW_BLOCK

            @pl.when(touched)
            def _():
                off = (g - k * ROW_BLOCK).astype(F32)
                onehot = jnp.where(rel + off == block_row, 1.0, 0.0).astype(BF16)
                blk = jnp.dot(onehot, h, preferred_element_type=F32)
                if j == 0:
                    blk = blk + stage_ref[e].astype(F32)
                blk = blk.astype(BF16)
                slot_ref[e, j] = blk
                stage_ref[e] = jnp.where(complete, jnp.zeros_like(blk), blk)

                @pl.when(complete)
                def _():
                    block_copy(e, j, k).start()

    def tail_copies(e):
        total = total_ref[e]
        n_data = (total + ROW_BLOCK - 1) // ROW_BLOCK
        n_region = ((total + MOE_TILE - 1) // MOE_TILE) * (MOE_TILE // ROW_BLOCK)
        part_row = pl.multiple_of(base_ref[e] + (total // ROW_BLOCK) * ROW_BLOCK, ROW_BLOCK)
        part = pltpu.make_async_copy(stage_ref.at[e], xs_ref.at[pl.ds(part_row, ROW_BLOCK)], fsem.at[e])
        out = [(total % ROW_BLOCK != 0, part)]
        for z in range(MOE_TILE // ROW_BLOCK - 1):
            zrow = pl.multiple_of(base_ref[e] + (n_data + z) * ROW_BLOCK, ROW_BLOCK)
            zc = pltpu.make_async_copy(zero_ref, xs_ref.at[pl.ds(zrow, ROW_BLOCK)], zsem.at[e, z])
            out.append((n_data + z < n_region, zc))
        return out

    def unused_copies():
        e = N_EXPERTS - 1
        used = base_ref[e] + ((total_ref[e] + MOE_TILE - 1) // MOE_TILE) * MOE_TILE
        out = []
        for z in range(usem.shape[0]):
            zrow = pl.multiple_of(used + z * ROW_BLOCK, ROW_BLOCK)
            zc = pltpu.make_async_copy(zero_ref, xs_ref.at[pl.ds(zrow, ROW_BLOCK)], usem.at[z])
            out.append((zrow < xs_ref.shape[0], zc))
        return out

    def all_tail_copies():
        return [cc for e in range(N_EXPERTS) for cc in tail_copies(e)] + unused_copies()

    @pl.when(last)
    def _():
        for cond, cp in all_tail_copies():
            @pl.when(cond)
            def _():
                cp.start()

    for e in range(N_EXPERTS):
        for j in range(MAX_SPAN):
            g, n, k, touched = _chunk_block(before_ref, cnt_ref, c, e, j)

            @pl.when(touched & (g + n >= (k + 1) * ROW_BLOCK))
            def _():
                block_copy(e, j, k).wait()

    @pl.when(last)
    def _():
        for cond, cp in all_tail_copies():
            @pl.when(cond)
            def _():
                cp.wait()


def _dispatch(meta, x2, sc, sh, rankt, n_tiles, seq):
    before, cnt, base, total = meta
    t, d = x2.shape
    cs = MOE_CHUNK
    cps = seq // cs
    mod = pl.BlockSpec((1, 1, d), lambda i, *_: (i // cps, 0, 0))
    grid_spec = pltpu.PrefetchScalarGridSpec(
        num_scalar_prefetch=4,
        grid=(t // cs,),
        in_specs=[pl.BlockSpec((cs, d), lambda i, *_: (i, 0)), mod, mod,
                  pl.BlockSpec((N_EXPERTS, cs), lambda i, *_: (0, i))],
        out_specs=pl.BlockSpec(memory_space=pl.ANY),
        scratch_shapes=[pltpu.VMEM((N_EXPERTS, ROW_BLOCK, d), BF16),
                        pltpu.VMEM((N_EXPERTS, MAX_SPAN, ROW_BLOCK, d), BF16),
                        pltpu.VMEM((ROW_BLOCK, d), BF16),
                        pltpu.SemaphoreType.DMA((N_EXPERTS, MAX_SPAN)),
                        pltpu.SemaphoreType.DMA((N_EXPERTS,)),
                        pltpu.SemaphoreType.DMA((N_EXPERTS, MOE_TILE // ROW_BLOCK - 1)),
                        pltpu.SemaphoreType.DMA(((n_tiles * MOE_TILE - 2 * t) // ROW_BLOCK,))],
    )
    return pl.pallas_call(
        _dispatch_kernel,
        out_shape=jax.ShapeDtypeStruct((n_tiles * MOE_TILE, d), BF16),
        grid_spec=grid_spec,
        compiler_params=_params(("arbitrary",)),
        name="moe_dispatch",
    )(before, cnt, base, total, x2, sc, sh, rankt)


def _experts_kernel(te_ref, nu_ref, x_ref, wgu_ref, wd_ref, y_ref, acc_ref):
    @pl.when(pl.program_id(0) >= nu_ref[0])
    def _():
        y_ref[...] = jnp.zeros_like(y_ref)

    @pl.when(pl.program_id(0) < nu_ref[0])
    def _():
        x = x_ref[...]
        for c in range(D_FF // FF_SUB):
            lo = c * FF_SUB
            gt = jnp.dot(x, wgu_ref[0, :, lo:lo + FF_SUB], preferred_element_type=F32)
            up = jnp.dot(x, wgu_ref[0, :, D_FF + lo:D_FF + lo + FF_SUB], preferred_element_type=F32)
            a = (gt * jax.nn.sigmoid(gt) * up).astype(BF16)
            part = jnp.dot(a, wd_ref[0, lo:lo + FF_SUB, :], preferred_element_type=F32)
            if c == 0:
                acc_ref[...] = part
            else:
                acc_ref[...] += part
        y_ref[...] = acc_ref[...].astype(BF16)


def _experts(tile_expert, n_used, xs, wgu_bf, wd_bf):
    rows, d = xs.shape
    tm = MOE_TILE
    clamp = lambda i, nu: jnp.minimum(i, nu[0] - 1)
    grid_spec = pltpu.PrefetchScalarGridSpec(
        num_scalar_prefetch=2,
        grid=(rows // tm,),
        in_specs=[pl.BlockSpec((tm, d), lambda i, te, nu: (clamp(i, nu), 0)),
                  pl.BlockSpec((1, d, 2 * D_FF), lambda i, te, nu: (te[clamp(i, nu)], 0, 0)),
                  pl.BlockSpec((1, D_FF, d), lambda i, te, nu: (te[clamp(i, nu)], 0, 0))],
        out_specs=pl.BlockSpec((tm, d), lambda i, te, nu: (i, 0)),
        scratch_shapes=[pltpu.VMEM((tm, d), F32)],
    )
    return pl.pallas_call(
        _experts_kernel,
        out_shape=jax.ShapeDtypeStruct((rows, d), BF16),
        grid_spec=grid_spec,
        compiler_params=_params(("arbitrary",)),
        name="moe_experts",
    )(tile_expert, n_used, xs, wgu_bf, wd_bf)


def _combine_kernel(before_ref, cnt_ref, base_ref, rank_ref, comb_ref, x_ref, gate_ref, g_ref, b_ref, y_ref,
                    out_ref, ybuf, res_ref, acc_ref, sem):
    c = pl.program_id(0)
    slot = c % 2

    def block_copy(chunk, sl, e, j):
        _, _, k, touched = _chunk_block(before_ref, cnt_ref, chunk, e, j)
        row0 = pl.multiple_of(base_ref[e] + k * ROW_BLOCK, ROW_BLOCK)
        return touched, pltpu.make_async_copy(y_ref.at[pl.ds(row0, ROW_BLOCK)], ybuf.at[sl, e, j], sem.at[sl, e, j])

    def fetch(chunk, sl):
        for e in range(N_EXPERTS):
            for j in range(MAX_SPAN):
                touched, cp = block_copy(chunk, sl, e, j)

                @pl.when(touched)
                def _():
                    cp.start()

    @pl.when(c == 0)
    def _():
        ybuf[...] = jnp.zeros_like(ybuf)
        fetch(0, 0)

    @pl.when(c + 1 < pl.num_programs(0))
    def _():
        fetch(c + 1, 1 - slot)

    lane = lax.broadcasted_iota(jnp.int32, (MOE_CHUNK, ROW_BLOCK), 1).astype(F32)
    acc_ref[...] = jnp.zeros_like(acc_ref)
    for e in range(N_EXPERTS):
        col = rank_ref[:, e:e + 1]
        res = None
        for j in range(MAX_SPAN):
            g, _, k, touched = _chunk_block(before_ref, cnt_ref, c, e, j)
            off = (g - k * ROW_BLOCK).astype(F32)

            def gathered():
                onehot = jnp.where(col + off == lane, 1.0, 0.0).astype(BF16)
                return jnp.dot(onehot, ybuf[slot, e, j], preferred_element_type=F32)

            if j < 2:
                @pl.when(touched)
                def _():
                    block_copy(c, slot, e, j)[1].wait()

                res = gathered() if res is None else res + gathered()
                if j == 1:
                    res_ref[...] = res
            else:
                @pl.when(touched)
                def _():
                    block_copy(c, slot, e, j)[1].wait()
                    res_ref[...] += gathered()
        acc_ref[...] += comb_ref[:, e:e + 1] * res_ref[...]
    out_ref[...] = _deepnorm_ln(x_ref[...], acc_ref[...], gate_ref[0], g_ref[...], b_ref[...])


def _combine(meta, rank, comb, x2, gate, ln_g, ln_b, y, seq):
    before, cnt, base, _ = meta
    t, d = x2.shape
    cs = MOE_CHUNK
    cps = seq // cs
    row = lambda w: pl.BlockSpec((cs, w), lambda i, *_: (i, 0))
    vec = pl.BlockSpec((1, d), lambda i, *_: (0, 0))
    grid_spec = pltpu.PrefetchScalarGridSpec(
        num_scalar_prefetch=3,
        grid=(t // cs,),
        in_specs=[row(LANES), row(LANES), row(d), pl.BlockSpec((1, 1, d), lambda i, *_: (i // cps, 0, 0)),
                  vec, vec, pl.BlockSpec(memory_space=pl.ANY)],
        out_specs=row(d),
        scratch_shapes=[pltpu.VMEM((2, N_EXPERTS, MAX_SPAN, ROW_BLOCK, d), BF16),
                        pltpu.VMEM((cs, d), F32),
                        pltpu.VMEM((cs, d), F32),
                        pltpu.SemaphoreType.DMA((2, N_EXPERTS, MAX_SPAN))],
    )
    return pl.pallas_call(
        _combine_kernel,
        out_shape=jax.ShapeDtypeStruct((t, d), F32),
        grid_spec=grid_spec,
        compiler_params=_params(("arbitrary",)),
        name="moe_combine",
    )(before, cnt, base, rank, comb, x2, gate, ln_g, ln_b, y)


def _block_diag(w_pool):
    g, c, _ = w_pool.shape
    out = jnp.zeros((g * c, g * c), w_pool.dtype)
    for i in range(g):
        out = out.at[i * c:(i + 1) * c, i * c:(i + 1) * c].set(w_pool[i])
    return out


def kernel(x, c, w_ada, b_ada, ln_g, ln_b, w_in_even, w_pool, pool_scale, w_out_even, w_ffn_gu, w_ffn_down,
           w_in_odd, b_forget, w_out_odd, w_router, w_exp_gu, w_exp_down):
    bsz, seq, d = x.shape
    t = bsz * seq
    tabs = _rope_tables(seq)
    ada = _ada(c, w_ada, b_ada)
    x2 = x.reshape(t, d)
    ones_comb = jnp.ones((t, LANES), F32)
    for l in range(DEPTH):
        i = l // 2
        sh1, sc1, g1, sh2, sc2, g2 = [ada[l, :, n * d:(n + 1) * d].reshape(bsz, 1, d) for n in range(6)]
        lng = ln_g[l].reshape(2, 1, d)
        lnb = ln_b[l].reshape(2, 1, d)
        if l % 2 == 0:
            outs = _even_in(x2, sc1, sh1, w_in_even[i].astype(BF16), tabs, seq)
            u, qkv = outs[0], outs[1:]
            os_, lses = [], []
            for g, (_, dil) in enumerate(DIL_CONFIGS):
                o, lse = _band_attention(qkv[g], qkv[3 + g], qkv[6 + g], bsz, seq, dil)
                os_.append(o)
                lses.append(lse)
            x2 = _even_out(u, os_, lses, _block_diag(w_pool[i]).astype(BF16), pool_scale[i].reshape(1, -1),
                           w_out_even[i].astype(BF16), x2, g1, lng[0], lnb[0], seq)
            x2 = _ffn(x2, sc2, sh2, ones_comb, w_ffn_gu[i][None].astype(BF16), w_ffn_down[i][None].astype(BF16),
                      g2, lng[1], lnb[1], seq)
        else:
            w_in = w_in_odd[i]
            wf = jnp.zeros((d, LANES), F32).at[:, :FOX_HEADS].set(w_in[:, 3 * FOX_WIDTH:]).astype(BF16)
            bfp = jnp.zeros((1, LANES), F32).at[0, :FOX_HEADS].set(b_forget[i])
            q, k, v, fa = [a.reshape(bsz, seq, FOX_WIDTH) for a in
                           _odd_in(x2, sc1, sh1, w_in[:, :3 * FOX_WIDTH].astype(BF16), wf, bfp, seq)]
            o = _fox_attention(q, k, v, fa, bsz, seq)
            x2 = _odd_out(o.reshape(t, FOX_WIDTH), w_out_odd[i].astype(BF16), x2, g1, lng[0], lnb[0], seq)
            wrt = jnp.zeros((2 * N_EXPERTS, d), F32).at[:N_EXPERTS].set(w_router[i].T).astype(BF16)
            rankt, cnt, comb, rank = _router(x2, sc2, sh2, wrt, seq)
            n_tiles = (2 * t + N_EXPERTS * (MOE_TILE - 1) + MOE_TILE - 1) // MOE_TILE
            *meta, tile_expert, n_used = _route_meta(cnt[:, :, 0].astype(jnp.int32), n_tiles)
            xs = _dispatch(meta, x2, sc2, sh2, rankt, n_tiles, seq)
            y = _experts(tile_expert, n_used, xs, w_exp_gu[i].astype(BF16), w_exp_down[i].astype(BF16))
            x2 = _combine(meta, rank, comb, x2, g2, lng[1], lnb[1], y, seq)
    return x2.reshape(bsz, seq, d)
```

```python
import functools
import math

import numpy as np

import jax
import jax.numpy as jnp
from jax import lax
from jax.experimental import pallas as pl
from jax.experimental.pallas import tpu as pltpu

F32 = jnp.float32
BF16 = jnp.bfloat16

D_MODEL = 1024
DEPTH = 4
HEAD_DIM = 64
POOL_WINDOWS = (2, 4, 8, 16)
POOL_GROUP_DIM = 64
POOL_WIDTH = 256
DIL_CONFIGS = ((128, 1), (512, 4), (2048, 16))
DIL_HEADS = 4
DIL_GROUP_WIDTH = DIL_HEADS * HEAD_DIM
BAND_BLOCK = 128
BAND_BLOCKS_PER_STEP = 4
FOX_HEADS = 16
FOX_WIDTH = 1024
ROPE_THETA = 500000.0
ROPE_DIM = 16
D_FF = 2816
N_EXPERTS = 8
DEEPNORM_ALPHA = (2 * DEPTH) ** 0.25
LN_EPS = 1e-5
NEG = -1e30
LOG2E = 1.4426950408889634

LANES = 128
TOKEN_TILE = 512
FF_CHUNK = 1408
FOX_Q_TILE = 512
FOX_K_TILE = 512
FOX_ROW_CHUNK = 64
MOE_CHUNK = 512
MOE_TILE = 512
ROW_BLOCK = 128
MAX_SPAN = MOE_CHUNK // ROW_BLOCK + 1
FF_SUB = 256
COMBINE_WIN = 2 * ROW_BLOCK
COMBINE_MAX_WIN = (ROW_BLOCK - 1 + MOE_CHUNK + COMBINE_WIN - 1) // COMBINE_WIN
NO_ROW = -1e9
VMEM_LIMIT = 56 * 1024 * 1024


def _params(sem, vmem=VMEM_LIMIT):
    return pltpu.CompilerParams(dimension_semantics=sem, vmem_limit_bytes=vmem)


def _modulate(x, sc, sh):
    return (x * (1.0 + sc) + sh).astype(BF16)


def _deepnorm_ln(x, sub, gate, g, b):
    y = DEEPNORM_ALPHA * x + gate * sub
    mu = jnp.mean(y, axis=-1, keepdims=True)
    yc = y - mu
    var = jnp.mean(yc * yc, axis=-1, keepdims=True)
    return yc * lax.rsqrt(var + LN_EPS) * g + b


def _ada_kernel(c_ref, w_ref, b_ref, o_ref):
    c = c_ref[...]
    ca = (c * jax.nn.sigmoid(c)).astype(BF16)
    o_ref[0] = jnp.dot(ca, w_ref[0].astype(BF16), preferred_element_type=F32) + b_ref[0]


def _ada(c, w_ada, b_ada):
    depth, d, n = w_ada.shape
    bsz = c.shape[0]
    tn = 1024
    return pl.pallas_call(
        _ada_kernel,
        out_shape=jax.ShapeDtypeStruct((depth, bsz, n), F32),
        grid=(depth, n // tn),
        in_specs=[
            pl.BlockSpec((bsz, d), lambda l, j: (0, 0)),
            pl.BlockSpec((1, d, tn), lambda l, j: (l, 0, j)),
            pl.BlockSpec((1, 1, tn), lambda l, j: (l, 0, j)),
        ],
        out_specs=pl.BlockSpec((1, bsz, tn), lambda l, j: (l, 0, j)),
        compiler_params=_params(("parallel", "parallel")),
        name="ada_mod",
    )(c, w_ada, b_ada.reshape(depth, 1, n))


def _even_in_kernel(x_ref, sc_ref, sh_ref, w_ref, ct_ref, sa_ref, sb_ref, u_ref, *rest):
    qkv_refs, z_ref = rest[:9], rest[9]
    tm = x_ref.shape[0]
    h = _modulate(x_ref[...], sc_ref[0], sh_ref[0])
    u_ref[...] = jnp.dot(h, w_ref[:, 0:POOL_WIDTH], preferred_element_type=F32)
    ct, sa, sb = ct_ref[...], sa_ref[...], sb_ref[...]
    for n in range(9):
        lo = POOL_WIDTH + n * DIL_GROUP_WIDTH
        z = jnp.dot(h, w_ref[:, lo:lo + DIL_GROUP_WIDTH], preferred_element_type=F32)
        if n < 6:
            z = z * ct + pltpu.roll(z, 8, 1) * sa + pltpu.roll(z, DIL_GROUP_WIDTH - 8, 1) * sb
        if n < 3:
            z = z * (HEAD_DIM ** -0.5)
        dil = DIL_CONFIGS[n % 3][1]
        if dil == 1:
            qkv_refs[n][0, 0] = z.astype(BF16)
        else:
            for half in range(2):
                z_ref[half] = z[:, half * LANES:(half + 1) * LANES]
            for r in range(dil):
                rows = [z_ref[half, pl.ds(r, tm // dil, stride=dil), :] for half in range(2)]
                qkv_refs[n][0, r] = jnp.concatenate(rows, axis=1).astype(BF16)


def _even_in(x2, sc, sh, w_bf, tabs, seq):
    t, d = x2.shape
    tm = TOKEN_TILE
    tps = seq // tm
    bsz = t // seq
    n_in = w_bf.shape[1]
    mod_spec = pl.BlockSpec((1, 1, d), lambda i: (i // tps, 0, 0))
    tab_spec = pl.BlockSpec((tm, DIL_GROUP_WIDTH), lambda i: (i % tps, 0))
    out_spec = pl.BlockSpec((tm, DIL_GROUP_WIDTH), lambda i: (i, 0))
    dils = [dil for _, dil in DIL_CONFIGS] * 3
    qkv_shapes = [jax.ShapeDtypeStruct((bsz, dil, seq // dil, DIL_GROUP_WIDTH), BF16) for dil in dils]
    qkv_specs = [pl.BlockSpec((1, dil, tm // dil, DIL_GROUP_WIDTH), lambda i: (i // tps, 0, i % tps, 0))
                 for dil in dils]
    return pl.pallas_call(
        _even_in_kernel,
        out_shape=[jax.ShapeDtypeStruct((t, POOL_WIDTH), F32)] + qkv_shapes,
        grid=(t // tm,),
        in_specs=[
            pl.BlockSpec((tm, d), lambda i: (i, 0)),
            mod_spec,
            mod_spec,
            pl.BlockSpec((d, n_in), lambda i: (0, 0)),
            tab_spec,
            tab_spec,
            tab_spec,
        ],
        out_specs=[out_spec] + qkv_specs,
        scratch_shapes=[pltpu.VMEM((DIL_GROUP_WIDTH // LANES, tm, LANES), F32)],
        compiler_params=_params(("parallel",)),
        name="even_in_proj",
    )(x2, sc, sh, w_bf, *tabs)


def _rope_tables(seq):
    pos = jnp.arange(seq, dtype=F32)
    inv = ROPE_THETA ** (-jnp.arange(0, ROPE_DIM, 2, dtype=F32) / ROPE_DIM)
    ang = pos[:, None] * inv[None, :]
    cos, sin = jnp.cos(ang), jnp.sin(ang)
    half = ROPE_DIM // 2
    zeros = jnp.zeros((seq, half), F32)
    rest = HEAD_DIM - ROPE_DIM
    ct = jnp.concatenate([cos, cos, jnp.ones((seq, rest), F32)], axis=1)
    sa = jnp.concatenate([zeros, sin, jnp.zeros((seq, rest), F32)], axis=1)
    sb = jnp.concatenate([-sin, zeros, jnp.zeros((seq, rest), F32)], axis=1)
    return tuple(jnp.tile(a, (1, DIL_HEADS)) for a in (ct, sa, sb))


def _band_kernel(q_ref, kp_ref, kc_ref, vp_ref, vc_ref, o_ref, lse_ref):
    mt = pl.program_id(2)
    blk = BAND_BLOCK
    nblk = q_ref.shape[2] // blk
    kall = jnp.concatenate([kp_ref[0, 0], kc_ref[0, 0]], axis=0)
    vall = jnp.concatenate([vp_ref[0, 0], vc_ref[0, 0]], axis=0)
    qi = lax.broadcasted_iota(jnp.int32, (blk, 2 * blk), 0)
    kj = lax.broadcasted_iota(jnp.int32, (blk, 2 * blk), 1)
    dist = qi + blk - kj
    in_band = (dist >= 0) & (dist <= blk)
    lane_q = lax.broadcasted_iota(jnp.int32, (blk, DIL_GROUP_WIDTH), 1) // HEAD_DIM
    lane_v = lax.broadcasted_iota(jnp.int32, (2 * blk, DIL_GROUP_WIDTH), 1) // HEAD_DIM
    for i in range(nblk):
        q = q_ref[0, 0, i * blk:(i + 1) * blk, :]
        kk = kall[i * blk:(i + 2) * blk]
        vv = vall[i * blk:(i + 2) * blk]
        if i == 0:
            valid = in_band & (kj >= jnp.where(mt > 0, 0, blk))
        else:
            valid = in_band
        acc = jnp.zeros((blk, DIL_GROUP_WIDTH), F32)
        den_full = jnp.ones((blk, DIL_GROUP_WIDTH), F32)
        lse_full = jnp.zeros((blk, DIL_GROUP_WIDTH), F32)
        for h in range(DIL_HEADS):
            qh = jnp.where(lane_q == h, q, jnp.zeros_like(q))
            s = lax.dot_general(qh, kk, (((1,), (1,)), ((), ())), preferred_element_type=F32)
            s = jnp.where(valid, s, NEG)
            m = jnp.max(s, axis=-1, keepdims=True)
            p = jnp.exp(s - m)
            den = jnp.sum(p, axis=-1, keepdims=True)
            vh = jnp.where(lane_v == h, vv, jnp.zeros_like(vv))
            acc = acc + jnp.dot(p.astype(BF16), vh, preferred_element_type=F32)
            den_full = jnp.where(lane_q == h, den, den_full)
            lse_full = jnp.where(lane_q == h, m + jnp.log(den), lse_full)
        o_ref[0, 0, i * blk:(i + 1) * blk, :] = acc / den_full
        lse_ref[0, 0, i * blk:(i + 1) * blk, :] = lse_full


def _band_attention(q, k, v):
    bsz, dil, ln, w = q.shape
    nblk = min(BAND_BLOCKS_PER_STEP, ln // BAND_BLOCK)
    rows = nblk * BAND_BLOCK
    cur = pl.BlockSpec((1, 1, rows, w), lambda b, r, m: (b, r, m, 0))
    prev = pl.BlockSpec((1, 1, BAND_BLOCK, w), lambda b, r, m: (b, r, jnp.maximum(m * nblk - 1, 0), 0))
    return pl.pallas_call(
        _band_kernel,
        out_shape=[jax.ShapeDtypeStruct(q.shape, F32)] * 2,
        grid=(bsz, dil, ln // rows),
        in_specs=[cur, prev, cur, prev, cur],
        out_specs=[cur, cur],
        compiler_params=_params(("parallel", "parallel", "parallel")),
        name=f"band_attn_d{dil}",
    )(q, k, k, v, v)


def _even_out_kernel(tps, u_ref, up_ref, o0, o1, o2, l0, l1, l2, wp_ref, ps_ref, wo_ref, x_ref, gate_ref,
                     g_ref, b_ref, out_ref, il_ref):
    i = pl.program_id(0)
    tm = u_ref.shape[0]
    halo = up_ref.shape[0]
    u = u_ref[...]
    prev = jnp.where(i % tps == 0, jnp.zeros_like(up_ref[...]), up_ref[...])
    ext = jnp.concatenate([prev, u], axis=0)
    lane_g = lax.broadcasted_iota(jnp.int32, (tm, POOL_WIDTH), 1) // POOL_GROUP_DIM
    row = lax.broadcasted_iota(jnp.int32, (tm, POOL_WIDTH), 0)
    pos1 = ((i % tps) * tm + row + 1).astype(F32)
    s = ext
    pooled = jnp.zeros((tm, POOL_WIDTH), F32)
    wvec = jnp.zeros((tm, POOL_WIDTH), F32)
    shift = 1
    for g, w in enumerate(POOL_WINDOWS):
        while shift < w:
            s = s + pltpu.roll(s, shift, 0)
            shift *= 2
        pooled = jnp.where(lane_g == g, s[halo:, :], pooled)
        wvec = jnp.where(lane_g == g, float(w), wvec)
    pooled = pooled / jnp.minimum(pos1, wvec) - u
    a_out = jnp.dot(pooled.astype(BF16), wp_ref[...], preferred_element_type=F32) * ps_ref[...]
    def token_order(ref, buf):
        dil = ref.shape[1]
        if dil == 1:
            return ref[0, 0]
        for r in range(dil):
            blk = ref[0, r]
            for half in range(2):
                buf[half, pl.ds(r, tm // dil, stride=dil), :] = blk[:, half * LANES:(half + 1) * LANES]
        return jnp.concatenate([buf[0], buf[1]], axis=1)

    la, lb, lc = [token_order(ref, il_ref.at[n]) for n, ref in enumerate((l0, l1, l2))]
    oa, ob, oc = [token_order(ref, il_ref.at[3 + n]) for n, ref in enumerate((o0, o1, o2))]
    mx = jnp.maximum(jnp.maximum(la, lb), lc)
    ea, eb, ec = jnp.exp(la - mx), jnp.exp(lb - mx), jnp.exp(lc - mx)
    b_out = (ea * oa + eb * ob + ec * oc) / (ea + eb + ec)
    mixed = jnp.concatenate([a_out, b_out], axis=-1).astype(BF16)
    sub = jnp.dot(mixed, wo_ref[...], preferred_element_type=F32)
    out_ref[...] = _deepnorm_ln(x_ref[...], sub, gate_ref[0], g_ref[...], b_ref[...])


def _even_out(u, os_, lses, wpool_bd, pool_scale, wo_bf, x2, gate, ln_g, ln_b, seq):
    t, d = x2.shape
    tm = TOKEN_TILE
    tps = seq // tm
    halo = 16
    row = lambda w: pl.BlockSpec((tm, w), lambda i: (i, 0))
    full = lambda a: pl.BlockSpec(a.shape, lambda i: (0,) * a.ndim)
    strided = [pl.BlockSpec((1, a.shape[1], tm // a.shape[1], DIL_GROUP_WIDTH), lambda i: (i // tps, 0, i % tps, 0))
               for a in (*os_, *lses)]
    return pl.pallas_call(
        functools.partial(_even_out_kernel, tps),
        out_shape=jax.ShapeDtypeStruct((t, d), F32),
        grid=(t // tm,),
        in_specs=[
            row(POOL_WIDTH),
            pl.BlockSpec((halo, POOL_WIDTH), lambda i: (jnp.maximum(i * (tm // halo) - 1, 0), 0)),
            *strided,
            full(wpool_bd),
            full(pool_scale),
            full(wo_bf),
            row(d),
            pl.BlockSpec((1, 1, d), lambda i: (i // tps, 0, 0)),
            full(ln_g),
            full(ln_b),
        ],
        out_specs=row(d),
        scratch_shapes=[pltpu.VMEM((6, DIL_GROUP_WIDTH // LANES, tm, LANES), F32)],
        compiler_params=_params(("parallel",)),
        name="even_out_proj",
    )(u, u, *os_, *lses, wpool_bd, pool_scale, wo_bf, x2, gate, ln_g, ln_b)


def _ffn_kernel(n_exp, x_ref, sc_ref, sh_ref, comb_ref, wg_ref, wu_ref, wd_ref, gate_ref, g_ref, b_ref,
                out_ref, h_ref, acc_ref):
    e = pl.program_id(1)
    c = pl.program_id(2)

    @pl.when((e == 0) & (c == 0))
    def _():
        h_ref[...] = _modulate(x_ref[...], sc_ref[0], sh_ref[0])
        acc_ref[...] = jnp.zeros_like(acc_ref)

    h = h_ref[...]
    gt = jnp.dot(h, wg_ref[0], preferred_element_type=F32)
    up = jnp.dot(h, wu_ref[0], preferred_element_type=F32)
    a = (gt * jax.nn.sigmoid(gt) * up).astype(BF16)
    y = jnp.dot(a, wd_ref[0], preferred_element_type=F32)
    if n_exp > 1:
        lane = lax.broadcasted_iota(jnp.int32, comb_ref.shape, 1)
        y = y * jnp.sum(jnp.where(lane == e, comb_ref[...], 0.0), axis=-1, keepdims=True)
    acc_ref[...] += y

    @pl.when((e == n_exp - 1) & (c == pl.num_programs(2) - 1))
    def _():
        out_ref[...] = _deepnorm_ln(x_ref[...], acc_ref[...], gate_ref[0], g_ref[...], b_ref[...])


def _ffn(x2, sc, sh, comb, wgu_bf, wd_bf, gate, ln_g, ln_b, seq):
    t, d = x2.shape
    n_exp = wgu_bf.shape[0]
    tm = TOKEN_TILE
    tps = seq // tm
    nch = D_FF // FF_CHUNK
    mod = pl.BlockSpec((1, 1, d), lambda i, e, c: (i // tps, 0, 0))
    row = lambda w: pl.BlockSpec((tm, w), lambda i, e, c: (i, 0))
    vec = pl.BlockSpec((1, d), lambda i, e, c: (0, 0))
    return pl.pallas_call(
        functools.partial(_ffn_kernel, n_exp),
        out_shape=jax.ShapeDtypeStruct((t, d), F32),
        grid=(t // tm, n_exp, nch),
        in_specs=[
            row(d),
            mod,
            mod,
            row(LANES),
            pl.BlockSpec((1, d, FF_CHUNK), lambda i, e, c: (e, 0, c)),
            pl.BlockSpec((1, d, FF_CHUNK), lambda i, e, c: (e, 0, nch + c)),
            pl.BlockSpec((1, FF_CHUNK, d), lambda i, e, c: (e, c, 0)),
            mod,
            vec,
            vec,
        ],
        out_specs=row(d),
        scratch_shapes=[pltpu.VMEM((tm, d), BF16), pltpu.VMEM((tm, d), F32)],
        compiler_params=_params(("parallel", "arbitrary", "arbitrary")),
        name=f"ffn_e{n_exp}",
    )(x2, sc, sh, comb, wgu_bf, wgu_bf, wd_bf, gate, ln_g, ln_b)


def _split3(x):
    hi = x.astype(BF16)
    r = x - hi.astype(F32)
    mid = r.astype(BF16)
    lo = (r - mid.astype(F32)).astype(BF16)
    return hi, mid, lo


def _odd_in_kernel(tps, x_ref, sc_ref, sh_ref, w_ref, wf_ref, bf_ref, place_ref, q_ref, k_ref, v_ref, f_ref,
                   carry_ref):
    i = pl.program_id(0)
    tm = x_ref.shape[0]
    h = _modulate(x_ref[...], sc_ref[0], sh_ref[0])
    q = jnp.dot(h, w_ref[:, 0:FOX_WIDTH], preferred_element_type=F32)
    q_ref[...] = (q * (HEAD_DIM ** -0.5 * LOG2E)).astype(BF16)
    k_ref[...] = jnp.dot(h, w_ref[:, FOX_WIDTH:2 * FOX_WIDTH], preferred_element_type=F32).astype(BF16)
    v_ref[...] = jnp.dot(h, w_ref[:, 2 * FOX_WIDTH:3 * FOX_WIDTH], preferred_element_type=F32).astype(BF16)
    f = jnp.dot(h, wf_ref[...], preferred_element_type=F32) + bf_ref[...]
    logf = jnp.minimum(f, 0.0) - jnp.log(1.0 + jnp.exp(-jnp.abs(f)))
    r = lax.broadcasted_iota(jnp.int32, (tm, tm), 0)
    cidx = lax.broadcasted_iota(jnp.int32, (tm, tm), 1)
    tri = jnp.where(cidx <= r, 1.0, 0.0).astype(BF16)
    hi, mid, lo = _split3(logf)
    cs = (jnp.dot(tri, lo, preferred_element_type=F32) + jnp.dot(tri, mid, preferred_element_type=F32)
          + jnp.dot(tri, hi, preferred_element_type=F32))

    @pl.when(i % tps == 0)
    def _():
        carry_ref[...] = jnp.zeros_like(carry_ref)

    cs = cs + carry_ref[...]
    carry_ref[...] = cs[tm - 1:tm, :]
    parts = jnp.concatenate(_split3(cs * LOG2E), axis=1)
    f_ref[...] = jnp.dot(parts, place_ref[...], preferred_element_type=F32).astype(BF16)


def _gate_placement():
    place = np.zeros((3 * LANES, FOX_WIDTH), np.float32)
    for h in range(FOX_HEADS):
        base = (h // 2) * LANES + (HEAD_DIM if h % 2 == 0 else 0)
        for p in range(3):
            place[p * LANES + h, base + p] = 1.0
    return jnp.asarray(place, BF16)


def _odd_in(x2, sc, sh, w_bf, wf_bf, bf_pad, seq):
    t, d = x2.shape
    tm = TOKEN_TILE
    tps = seq // tm
    mod = pl.BlockSpec((1, 1, d), lambda i: (i // tps, 0, 0))
    row = lambda w: pl.BlockSpec((tm, w), lambda i: (i, 0))
    full = lambda a: pl.BlockSpec(a.shape, lambda i: (0,) * a.ndim)
    place = _gate_placement()
    return pl.pallas_call(
        functools.partial(_odd_in_kernel, tps),
        out_shape=[jax.ShapeDtypeStruct((t, FOX_WIDTH), BF16)] * 4,
        grid=(t // tm,),
        in_specs=[row(d), mod, mod, full(w_bf), full(wf_bf), full(bf_pad), full(place)],
        out_specs=[row(FOX_WIDTH)] * 4,
        scratch_shapes=[pltpu.VMEM((1, LANES), F32)],
        compiler_params=_params(("arbitrary",)),
        name="odd_in_proj",
    )(x2, sc, sh, w_bf, wf_bf, bf_pad, place)


def _fox_kernel(q_ref, k_ref, v_ref, f_ref, o_ref, acc_ref, m_ref, s0_ref, s1_ref, p0_ref, p1_ref):
    qi = pl.program_id(2)
    tq = q_ref.shape[1]
    tk = FOX_K_TILE
    half = HEAD_DIM
    q = q_ref[0]
    lane_q = lax.broadcasted_iota(jnp.int32, q.shape, 1)
    lane_k = lax.broadcasted_iota(jnp.int32, (tk, LANES), 1)
    qa = jnp.where(lane_q < half, q, jnp.where(lane_q < half + 3, -1.0, 0.0).astype(BF16))
    qb = jnp.where(lane_q >= half, q, jnp.where(lane_q < 3, -1.0, 0.0).astype(BF16))
    ones_a = jnp.where(lane_k == half, 1.0, 0.0).astype(BF16)
    ones_b = jnp.where(lane_k == 0, 1.0, 0.0).astype(BF16)
    acc_ref[...] = jnp.zeros_like(acc_ref)
    m_ref[...] = jnp.full(m_ref.shape, NEG, F32)

    def scores(j, s_ref):
        start = pl.multiple_of(j * tk, tk)
        k = k_ref[0, pl.ds(start, tk), :]
        fa = f_ref[0, pl.ds(start, tk), :]
        ka = jnp.where(lane_k < half, k, fa)
        kb = jnp.where(lane_k >= half, k, fa)
        for hh, (qh, kh) in enumerate(((qa, ka), (qb, kb))):
            s_ref[hh] = lax.dot_general(qh, kh, (((1,), (1,)), ((), ())), preferred_element_type=F32)

    def attend(j, s_ref, p_ref, masked):
        start = pl.multiple_of(j * tk, tk)
        v = v_ref[0, pl.ds(start, tk), :]
        va = jnp.where(lane_k < half, v, ones_a)
        vb = jnp.where(lane_k >= half, v, ones_b)
        for hh, vh in enumerate((va, vb)):
            for r0 in range(0, tq, FOX_ROW_CHUNK):
                rows = pl.ds(r0, FOX_ROW_CHUNK)
                s = s_ref[hh, rows, :]
                if masked:
                    rr = lax.broadcasted_iota(jnp.int32, s.shape, 0) + (qi * tq + r0)
                    cc = lax.broadcasted_iota(jnp.int32, s.shape, 1) + start
                    s = jnp.where(cc <= rr, s, NEG)
                m_old = m_ref[hh, rows, :]
                m_new = jnp.maximum(m_old, jnp.max(s, axis=-1, keepdims=True))
                p_ref[hh, rows, :] = jnp.exp2(s - jnp.tile(m_new, (1, tk // LANES))).astype(BF16)
                acc_ref[hh, rows, :] = acc_ref[hh, rows, :] * jnp.exp2(m_old - m_new)
                m_ref[hh, rows, :] = m_new
            acc_ref[hh] += jnp.dot(p_ref[hh], vh, preferred_element_type=F32)

    scores(0, s0_ref)

    def body(jp, carry):
        scores(2 * jp + 1, s1_ref)
        attend(2 * jp, s0_ref, p0_ref, False)
        scores(2 * jp + 2, s0_ref)
        attend(2 * jp + 1, s1_ref, p1_ref, False)
        return carry

    lax.fori_loop(0, qi // 2, body, 0)

    @pl.when(qi % 2 == 0)
    def _():
        attend(qi, s0_ref, p0_ref, True)

    @pl.when(qi % 2 == 1)
    def _():
        scores(qi, s1_ref)
        attend(qi - 1, s0_ref, p0_ref, False)
        attend(qi, s1_ref, p1_ref, True)

    acc_a, acc_b = acc_ref[0], acc_ref[1]
    lane_o = lax.broadcasted_iota(jnp.int32, (tq, LANES), 1)
    o = jnp.where(lane_o < half, acc_a / acc_a[:, half:half + 1], acc_b / acc_b[:, 0:1])
    o_ref[0] = o.astype(BF16)


def _fox_attention(q, k, v, fa, bsz, seq):
    tq = FOX_Q_TILE
    npairs = FOX_HEADS // 2
    tile = pl.BlockSpec((1, tq, LANES), lambda b, j, i: (b, i, j))
    whole = pl.BlockSpec((1, seq, LANES), lambda b, j, i: (b, 0, j))
    return pl.pallas_call(
        _fox_kernel,
        out_shape=jax.ShapeDtypeStruct((bsz, seq, FOX_WIDTH), BF16),
        grid=(bsz, npairs, seq // tq),
        in_specs=[tile, whole, whole, whole],
        out_specs=tile,
        scratch_shapes=[pltpu.VMEM((2, tq, LANES), F32), pltpu.VMEM((2, tq, LANES), F32),
                        pltpu.VMEM((2, tq, FOX_K_TILE), F32), pltpu.VMEM((2, tq, FOX_K_TILE), F32),
                        pltpu.VMEM((2, tq, FOX_K_TILE), BF16), pltpu.VMEM((2, tq, FOX_K_TILE), BF16)],
        compiler_params=_params(("parallel", "parallel", "arbitrary")),
        name="fox_attn",
    )(q, k, v, fa)


def _odd_out_kernel(o_ref, wo_ref, x_ref, gate_ref, g_ref, b_ref, out_ref):
    sub = jnp.dot(o_ref[...], wo_ref[...], preferred_element_type=F32)
    out_ref[...] = _deepnorm_ln(x_ref[...], sub, gate_ref[0], g_ref[...], b_ref[...])


def _odd_out(o2, wo_bf, x2, gate, ln_g, ln_b, seq):
    t, d = x2.shape
    tm = TOKEN_TILE
    tps = seq // tm
    row = lambda w: pl.BlockSpec((tm, w), lambda i: (i, 0))
    full = lambda a: pl.BlockSpec(a.shape, lambda i: (0,) * a.ndim)
    return pl.pallas_call(
        _odd_out_kernel,
        out_shape=jax.ShapeDtypeStruct((t, d), F32),
        grid=(t // tm,),
        in_specs=[row(FOX_WIDTH), full(wo_bf), row(d), pl.BlockSpec((1, 1, d), lambda i: (i // tps, 0, 0)),
                  full(ln_g), full(ln_b)],
        out_specs=row(d),
        compiler_params=_params(("parallel",)),
        name="odd_out_proj",
    )(o2, wo_bf, x2, gate, ln_g, ln_b)


def _router_kernel(x_ref, sc_ref, sh_ref, wrt_ref, tri_ref, rankt_ref, cnt_ref, comb_ref, rank_ref):
    c = x_ref.shape[0]
    h = _modulate(x_ref[...], sc_ref[0], sh_ref[0])
    lt = lax.dot_general(wrt_ref[...], h, (((1,), (1,)), ((), ())), preferred_element_type=F32)[0:N_EXPERTS]
    row = lax.broadcasted_iota(jnp.int32, lt.shape, 0)
    m1 = jnp.max(lt, axis=0, keepdims=True)
    i1 = jnp.min(jnp.where(lt == m1, row, N_EXPERTS), axis=0, keepdims=True)
    rest = jnp.where(row == i1, -jnp.inf, lt)
    m2 = jnp.max(rest, axis=0, keepdims=True)
    i2 = jnp.min(jnp.where(rest == m2, row, N_EXPERTS), axis=0, keepdims=True)
    e2 = jnp.exp(m2 - m1)
    g1 = 1.0 / (1.0 + e2)
    g2 = e2 / (1.0 + e2)
    combt = jnp.where(row == i1, g1, jnp.where(row == i2, g2, 0.0))
    sel = jnp.where((row == i1) | (row == i2), 1.0, 0.0)
    sel16 = jnp.concatenate([sel, jnp.zeros_like(sel)], axis=0).astype(BF16)
    earlier = jnp.dot(sel16, tri_ref[...], preferred_element_type=F32)[0:N_EXPERTS]
    rankt = jnp.where(sel > 0.0, earlier, NO_ROW)
    rankt_ref[...] = rankt
    cnt_ref[0] = jnp.broadcast_to(jnp.sum(sel, axis=1, keepdims=True), (N_EXPERTS, LANES))
    pad = LANES - N_EXPERTS
    comb_ref[...] = jnp.concatenate([combt, jnp.zeros((pad, c), F32)], axis=0).T
    rank_ref[...] = jnp.concatenate([rankt, jnp.full((pad, c), NO_ROW, F32)], axis=0).T


def _router(x2, sc, sh, wrt, seq):
    t, d = x2.shape
    cs = MOE_CHUNK
    cps = seq // cs
    mod = pl.BlockSpec((1, 1, d), lambda i: (i // cps, 0, 0))
    tri = jnp.asarray(np.triu(np.ones((cs, cs), np.float32), 1), BF16)
    return pl.pallas_call(
        _router_kernel,
        out_shape=[jax.ShapeDtypeStruct((N_EXPERTS, t), F32),
                   jax.ShapeDtypeStruct((t // cs, N_EXPERTS, LANES), F32),
                   jax.ShapeDtypeStruct((t, LANES), F32),
                   jax.ShapeDtypeStruct((t, LANES), F32)],
        grid=(t // cs,),
        in_specs=[pl.BlockSpec((cs, d), lambda i: (i, 0)), mod, mod,
                  pl.BlockSpec(wrt.shape, lambda i: (0, 0)), pl.BlockSpec(tri.shape, lambda i: (0, 0))],
        out_specs=[pl.BlockSpec((N_EXPERTS, cs), lambda i: (0, i)),
                   pl.BlockSpec((1, N_EXPERTS, LANES), lambda i: (i, 0, 0)),
                   pl.BlockSpec((cs, LANES), lambda i: (i, 0)),
                   pl.BlockSpec((cs, LANES), lambda i: (i, 0))],
        compiler_params=_params(("parallel",)),
        name="router",
    )(x2, sc, sh, wrt, tri)


def _route_meta(cnt, n_tiles):
    before = jnp.cumsum(cnt, axis=0) - cnt
    total = jnp.sum(cnt, axis=0)
    region = ((total + MOE_TILE - 1) // MOE_TILE) * MOE_TILE
    base = jnp.cumsum(region) - region
    n_used = (jnp.sum(region) // MOE_TILE).reshape(1)
    starts = jnp.arange(n_tiles, dtype=jnp.int32) * MOE_TILE
    tile_expert = jnp.minimum(jnp.sum(starts[:, None] >= (base + region)[None, :], axis=1), N_EXPERTS - 1)
    i32 = lambda a: a.astype(jnp.int32)
    return i32(before.reshape(-1)), i32(cnt.reshape(-1)), i32(base), i32(total), i32(tile_expert), i32(n_used)


def _chunk_block(before_ref, cnt_ref, chunk, e, j):
    g = before_ref[chunk * N_EXPERTS + e]
    n = cnt_ref[chunk * N_EXPERTS + e]
    k = g // ROW_BLOCK + j
    return g, n, k, (n > 0) & (k * ROW_BLOCK < g + n)


def _dispatch_kernel(before_ref, cnt_ref, base_ref, total_ref, x_ref, sc_ref, sh_ref, rankt_ref, xs_ref,
                     stage_ref, slot_ref, zero_ref, sem, fsem, zsem, usem):
    c = pl.program_id(0)
    last = c == pl.num_programs(0) - 1
    h = _modulate(x_ref[...], sc_ref[0], sh_ref[0])

    @pl.when(c == 0)
    def _():
        stage_ref[...] = jnp.zeros_like(stage_ref)
        zero_ref[...] = jnp.zeros_like(zero_ref)

    block_row = lax.broadcasted_iota(jnp.int32, (ROW_BLOCK, MOE_CHUNK), 0).astype(F32)
    pair_row = lax.broadcasted_iota(jnp.int32, (2 * ROW_BLOCK, MOE_CHUNK), 0).astype(F32)

    def block_copy(e, j, k):
        row0 = pl.multiple_of(base_ref[e] + k * ROW_BLOCK, ROW_BLOCK)
        return pltpu.make_async_copy(slot_ref.at[e, j], xs_ref.at[pl.ds(row0, ROW_BLOCK)], sem.at[e, j])

    selectors = []
    for e in range(N_EXPERTS):
        g = before_ref[c * N_EXPERTS + e]
        off = (g % ROW_BLOCK).astype(F32)
        selectors.append(jnp.where(rankt_ref[e:e + 1, :] + off == pair_row, 1.0, 0.0).astype(BF16))
    first_two = jnp.dot(jnp.concatenate(selectors, axis=0), h, preferred_element_type=F32)

    for e in range(N_EXPERTS):
        rel = rankt_ref[e:e + 1, :]
        for j in range(MAX_SPAN):
            g, n, k, touched = _chunk_block(before_ref, cnt_ref, c, e, j)
            complete = g + n >= (k + 1) * ROW_BLOCK

            @pl.when(touched)
            def _():
                if j < 2:
                    r0 = (2 * e + j) * ROW_BLOCK
                    blk = first_two[r0:r0 + ROW_BLOCK]
                else:
                    off = (g - k * ROW_BLOCK).astype(F32)
                    onehot = jnp.where(rel + off == block_row, 1.0, 0.0).astype(BF16)
                    blk = jnp.dot(onehot, h, preferred_element_type=F32)
                if j == 0:
                    blk = blk + stage_ref[e].astype(F32)
                blk = blk.astype(BF16)
                slot_ref[e, j] = blk
                stage_ref[e] = jnp.where(complete, jnp.zeros_like(blk), blk)

                @pl.when(complete)
                def _():
                    block_copy(e, j, k).start()

    def tail_copies(e):
        total = total_ref[e]
        n_data = (total + ROW_BLOCK - 1) // ROW_BLOCK
        n_region = ((total + MOE_TILE - 1) // MOE_TILE) * (MOE_TILE // ROW_BLOCK)
        part_row = pl.multiple_of(base_ref[e] + (total // ROW_BLOCK) * ROW_BLOCK, ROW_BLOCK)
        part = pltpu.make_async_copy(stage_ref.at[e], xs_ref.at[pl.ds(part_row, ROW_BLOCK)], fsem.at[e])
        out = [(total % ROW_BLOCK != 0, part)]
        for z in range(MOE_TILE // ROW_BLOCK - 1):
            zrow = pl.multiple_of(base_ref[e] + (n_data + z) * ROW_BLOCK, ROW_BLOCK)
            zc = pltpu.make_async_copy(zero_ref, xs_ref.at[pl.ds(zrow, ROW_BLOCK)], zsem.at[e, z])
            out.append((n_data + z < n_region, zc))
        return out

    def unused_copies():
        e = N_EXPERTS - 1
        used = base_ref[e] + ((total_ref[e] + MOE_TILE - 1) // MOE_TILE) * MOE_TILE
        out = []
        for z in range(usem.shape[0]):
            zrow = pl.multiple_of(used + z * ROW_BLOCK, ROW_BLOCK)
            zc = pltpu.make_async_copy(zero_ref, xs_ref.at[pl.ds(zrow, ROW_BLOCK)], usem.at[z])
            out.append((zrow < xs_ref.shape[0], zc))
        return out

    def all_tail_copies():
        return [cc for e in range(N_EXPERTS) for cc in tail_copies(e)] + unused_copies()

    @pl.when(last)
    def _():
        for cond, cp in all_tail_copies():
            @pl.when(cond)
            def _():
                cp.start()

    for e in range(N_EXPERTS):
        for j in range(MAX_SPAN):
            g, n, k, touched = _chunk_block(before_ref, cnt_ref, c, e, j)

            @pl.when(touched & (g + n >= (k + 1) * ROW_BLOCK))
            def _():
                block_copy(e, j, k).wait()

    @pl.when(last)
    def _():
        for cond, cp in all_tail_copies():
            @pl.when(cond)
            def _():
                cp.wait()


def _dispatch(meta, x2, sc, sh, rankt, n_tiles, seq):
    before, cnt, base, total = meta
    t, d = x2.shape
    cs = MOE_CHUNK
    cps = seq // cs
    mod = pl.BlockSpec((1, 1, d), lambda i, *_: (i // cps, 0, 0))
    grid_spec = pltpu.PrefetchScalarGridSpec(
        num_scalar_prefetch=4,
        grid=(t // cs,),
        in_specs=[pl.BlockSpec((cs, d), lambda i, *_: (i, 0)), mod, mod,
                  pl.BlockSpec((N_EXPERTS, cs), lambda i, *_: (0, i))],
        out_specs=pl.BlockSpec(memory_space=pl.ANY),
        scratch_shapes=[pltpu.VMEM((N_EXPERTS, ROW_BLOCK, d), BF16),
                        pltpu.VMEM((N_EXPERTS, MAX_SPAN, ROW_BLOCK, d), BF16),
                        pltpu.VMEM((ROW_BLOCK, d), BF16),
                        pltpu.SemaphoreType.DMA((N_EXPERTS, MAX_SPAN)),
                        pltpu.SemaphoreType.DMA((N_EXPERTS,)),
                        pltpu.SemaphoreType.DMA((N_EXPERTS, MOE_TILE // ROW_BLOCK - 1)),
                        pltpu.SemaphoreType.DMA(((n_tiles * MOE_TILE - 2 * t) // ROW_BLOCK,))],
    )
    return pl.pallas_call(
        _dispatch_kernel,
        out_shape=jax.ShapeDtypeStruct((n_tiles * MOE_TILE, d), BF16),
        grid_spec=grid_spec,
        compiler_params=_params(("arbitrary",)),
        name="moe_dispatch",
    )(before, cnt, base, total, x2, sc, sh, rankt)


def _experts_kernel(te_ref, nu_ref, x_ref, wgu_ref, wd_ref, y_ref, acc_ref):
    @pl.when(pl.program_id(0) >= nu_ref[0])
    def _():
        y_ref[...] = jnp.zeros_like(y_ref)

    @pl.when(pl.program_id(0) < nu_ref[0])
    def _():
        x = x_ref[...]
        for c in range(D_FF // FF_SUB):
            lo = c * FF_SUB
            gt = jnp.dot(x, wgu_ref[0, :, lo:lo + FF_SUB], preferred_element_type=F32)
            up = jnp.dot(x, wgu_ref[0, :, D_FF + lo:D_FF + lo + FF_SUB], preferred_element_type=F32)
            a = (gt * jax.nn.sigmoid(gt) * up).astype(BF16)
            part = jnp.dot(a, wd_ref[0, lo:lo + FF_SUB, :], preferred_element_type=F32)
            if c == 0:
                acc_ref[...] = part
            else:
                acc_ref[...] += part
        y_ref[...] = acc_ref[...].astype(BF16)


def _experts(tile_expert, n_used, xs, wgu_bf, wd_bf):
    rows, d = xs.shape
    tm = MOE_TILE
    clamp = lambda i, nu: jnp.minimum(i, nu[0] - 1)
    grid_spec = pltpu.PrefetchScalarGridSpec(
        num_scalar_prefetch=2,
        grid=(rows // tm,),
        in_specs=[pl.BlockSpec((tm, d), lambda i, te, nu: (clamp(i, nu), 0)),
                  pl.BlockSpec((1, d, 2 * D_FF), lambda i, te, nu: (te[clamp(i, nu)], 0, 0)),
                  pl.BlockSpec((1, D_FF, d), lambda i, te, nu: (te[clamp(i, nu)], 0, 0))],
        out_specs=pl.BlockSpec((tm, d), lambda i, te, nu: (i, 0)),
        scratch_shapes=[pltpu.VMEM((tm, d), F32)],
    )
    return pl.pallas_call(
        _experts_kernel,
        out_shape=jax.ShapeDtypeStruct((rows, d), BF16),
        grid_spec=grid_spec,
        compiler_params=_params(("arbitrary",)),
        name="moe_experts",
    )(tile_expert, n_used, xs, wgu_bf, wd_bf)


def _combine_kernel(before_ref, cnt_ref, base_ref, rank_ref, comb_ref, x_ref, gate_ref, g_ref, b_ref, y_ref,
                    out_ref, ybuf, xbuf, acc_ref, sem, xsem):
    c = pl.program_id(0)
    slot = c % 2

    def window(chunk, e, w):
        g = before_ref[chunk * N_EXPERTS + e]
        n = cnt_ref[chunk * N_EXPERTS + e]
        first = (g // ROW_BLOCK) * ROW_BLOCK + w * COMBINE_WIN
        row0 = pl.multiple_of(jnp.minimum(base_ref[e] + first, y_ref.shape[0] - COMBINE_WIN), ROW_BLOCK)
        off = (base_ref[e] + g - row0).astype(F32)
        return (n > 0) & (first < g + n), row0, off

    def first_copy(chunk, sl, e):
        touched, row0, _ = window(chunk, e, 0)
        return touched, pltpu.make_async_copy(y_ref.at[pl.ds(row0, COMBINE_WIN)], ybuf.at[sl, e], sem.at[sl, e])

    def fetch(chunk, sl):
        for e in range(N_EXPERTS):
            touched, cp = first_copy(chunk, sl, e)

            @pl.when(touched)
            def _():
                cp.start()

    @pl.when(c == 0)
    def _():
        ybuf[...] = jnp.zeros_like(ybuf)
        fetch(0, 0)

    @pl.when(c + 1 < pl.num_programs(0))
    def _():
        fetch(c + 1, 1 - slot)

    lane = lax.broadcasted_iota(jnp.int32, (MOE_CHUNK, COMBINE_WIN), 1).astype(F32)
    acc_ref[...] = jnp.zeros_like(acc_ref)
    for e in range(N_EXPERTS):
        col = rank_ref[:, e:e + 1]
        weight = comb_ref[:, e:e + 1]
        touched, _, off = window(c, e, 0)

        @pl.when(touched)
        def _():
            first_copy(c, slot, e)[1].wait()

        onehot = jnp.where(col + off == lane, 1.0, 0.0).astype(BF16)
        acc_ref[...] += weight * jnp.dot(onehot, ybuf[slot, e], preferred_element_type=F32)
        for w in range(1, COMBINE_MAX_WIN):
            touched, row0, off = window(c, e, w)

            @pl.when(touched)
            def _():
                cp = pltpu.make_async_copy(y_ref.at[pl.ds(row0, COMBINE_WIN)], xbuf.at[e, w - 1], xsem.at[e, w - 1])
                cp.start()
                cp.wait()
                onehot = jnp.where(col + off == lane, 1.0, 0.0).astype(BF16)
                acc_ref[...] += weight * jnp.dot(onehot, xbuf[e, w - 1], preferred_element_type=F32)
    out_ref[...] = _deepnorm_ln(x_ref[...], acc_ref[...], gate_ref[0], g_ref[...], b_ref[...])


def _combine(meta, rank, comb, x2, gate, ln_g, ln_b, y, seq):
    before, cnt, base, _ = meta
    t, d = x2.shape
    cs = MOE_CHUNK
    cps = seq // cs
    row = lambda w: pl.BlockSpec((cs, w), lambda i, *_: (i, 0))
    vec = pl.BlockSpec((1, d), lambda i, *_: (0, 0))
    grid_spec = pltpu.PrefetchScalarGridSpec(
        num_scalar_prefetch=3,
        grid=(t // cs,),
        in_specs=[row(LANES), row(LANES), row(d), pl.BlockSpec((1, 1, d), lambda i, *_: (i // cps, 0, 0)),
                  vec, vec, pl.BlockSpec(memory_space=pl.ANY)],
        out_specs=row(d),
        scratch_shapes=[pltpu.VMEM((2, N_EXPERTS, COMBINE_WIN, d), BF16),
                        pltpu.VMEM((N_EXPERTS, COMBINE_MAX_WIN - 1, COMBINE_WIN, d), BF16),
                        pltpu.VMEM((cs, d), F32),
                        pltpu.SemaphoreType.DMA((2, N_EXPERTS)),
                        pltpu.SemaphoreType.DMA((N_EXPERTS, COMBINE_MAX_WIN - 1))],
    )
    return pl.pallas_call(
        _combine_kernel,
        out_shape=jax.ShapeDtypeStruct((t, d), F32),
        grid_spec=grid_spec,
        compiler_params=_params(("arbitrary",)),
        name="moe_combine",
    )(before, cnt, base, rank, comb, x2, gate, ln_g, ln_b, y)


def _block_diag(w_pool):
    g, c, _ = w_pool.shape
    out = jnp.zeros((g * c, g * c), w_pool.dtype)
    for i in range(g):
        out = out.at[i * c:(i + 1) * c, i * c:(i + 1) * c].set(w_pool[i])
    return out


def kernel(x, c, w_ada, b_ada, ln_g, ln_b, w_in_even, w_pool, pool_scale, w_out_even, w_ffn_gu, w_ffn_down,
           w_in_odd, b_forget, w_out_odd, w_router, w_exp_gu, w_exp_down):
    bsz, seq, d = x.shape
    t = bsz * seq
    tabs = _rope_tables(seq)
    ada = _ada(c, w_ada, b_ada)
    x2 = x.reshape(t, d)
    ones_comb = jnp.ones((t, LANES), F32)
    for l in range(DEPTH):
        i = l // 2
        sh1, sc1, g1, sh2, sc2, g2 = [ada[l, :, n * d:(n + 1) * d].reshape(bsz, 1, d) for n in range(6)]
        lng = ln_g[l].reshape(2, 1, d)
        lnb = ln_b[l].reshape(2, 1, d)
        if l % 2 == 0:
            outs = _even_in(x2, sc1, sh1, w_in_even[i].astype(BF16), tabs, seq)
            u, qkv = outs[0], outs[1:]
            os_, lses = [], []
            for g in range(len(DIL_CONFIGS)):
                o, lse = _band_attention(qkv[g], qkv[3 + g], qkv[6 + g])
                os_.append(o)
                lses.append(lse)
            x2 = _even_out(u, os_, lses, _block_diag(w_pool[i]).astype(BF16), pool_scale[i].reshape(1, -1),
                           w_out_even[i].astype(BF16), x2, g1, lng[0], lnb[0], seq)
            x2 = _ffn(x2, sc2, sh2, ones_comb, w_ffn_gu[i][None].astype(BF16), w_ffn_down[i][None].astype(BF16),
                      g2, lng[1], lnb[1], seq)
        else:
            w_in = w_in_odd[i]
            wf = jnp.zeros((d, LANES), F32).at[:, :FOX_HEADS].set(w_in[:, 3 * FOX_WIDTH:]).astype(BF16)
            bfp = jnp.zeros((1, LANES), F32).at[0, :FOX_HEADS].set(b_forget[i])
            q, k, v, fa = [a.reshape(bsz, seq, FOX_WIDTH) for a in
                           _odd_in(x2, sc1, sh1, w_in[:, :3 * FOX_WIDTH].astype(BF16), wf, bfp, seq)]
            o = _fox_attention(q, k, v, fa, bsz, seq)
            x2 = _odd_out(o.reshape(t, FOX_WIDTH), w_out_odd[i].astype(BF16), x2, g1, lng[0], lnb[0], seq)
            wrt = jnp.zeros((2 * N_EXPERTS, d), F32).at[:N_EXPERTS].set(w_router[i].T).astype(BF16)
            rankt, cnt, comb, rank = _router(x2, sc2, sh2, wrt, seq)
            n_tiles = (2 * t + N_EXPERTS * (MOE_TILE - 1) + MOE_TILE - 1) // MOE_TILE
            *meta, tile_expert, n_used = _route_meta(cnt[:, :, 0].astype(jnp.int32), n_tiles)
            xs = _dispatch(meta, x2, sc2, sh2, rankt, n_tiles, seq)
            y = _experts(tile_expert, n_used, xs, w_exp_gu[i].astype(BF16), w_exp_down[i].astype(BF16))
            x2 = _combine(meta, rank, comb, x2, g2, lng[1], lnb[1], y, seq)
    return x2.reshape(bsz, seq, d)
```

```python
import functools
import math

import numpy as np

import jax
import jax.numpy as jnp
from jax import lax
from jax.experimental import pallas as pl
from jax.experimental.pallas import tpu as pltpu

F32 = jnp.float32
BF16 = jnp.bfloat16

D_MODEL = 1024
DEPTH = 4
HEAD_DIM = 64
POOL_WINDOWS = (2, 4, 8, 16)
POOL_GROUP_DIM = 64
POOL_WIDTH = 256
DIL_CONFIGS = ((128, 1), (512, 4), (2048, 16))
DIL_HEADS = 4
DIL_GROUP_WIDTH = DIL_HEADS * HEAD_DIM
BAND_BLOCK = 128
BAND_BLOCKS_PER_STEP = 4
FOX_HEADS = 16
FOX_WIDTH = 1024
ROPE_THETA = 500000.0
ROPE_DIM = 16
D_FF = 2816
N_EXPERTS = 8
DEEPNORM_ALPHA = (2 * DEPTH) ** 0.25
LN_EPS = 1e-5
NEG = -1e30
LOG2E = 1.4426950408889634

LANES = 128
TOKEN_TILE = 512
FOX_Q_TILE = 512
FOX_K_TILE = 512
FOX_ROW_CHUNK = 64
MOE_CHUNK = 512
MOE_TILE = 512
ROW_BLOCK = 128
MAX_SPAN = MOE_CHUNK // ROW_BLOCK + 1
FF_SUB = 256
COMBINE_WIN = 2 * ROW_BLOCK
COMBINE_MAX_WIN = (ROW_BLOCK - 1 + MOE_CHUNK + COMBINE_WIN - 1) // COMBINE_WIN
NO_ROW = -1e9
VMEM_LIMIT = 56 * 1024 * 1024


def _params(sem, vmem=VMEM_LIMIT):
    return pltpu.CompilerParams(dimension_semantics=sem, vmem_limit_bytes=vmem)


def _modulate(x, sc, sh):
    return (x * (1.0 + sc) + sh).astype(BF16)


def _deepnorm_ln(x, sub, gate, g, b):
    y = DEEPNORM_ALPHA * x + gate * sub
    mu = jnp.mean(y, axis=-1, keepdims=True)
    yc = y - mu
    var = jnp.mean(yc * yc, axis=-1, keepdims=True)
    return yc * lax.rsqrt(var + LN_EPS) * g + b


def _ada_kernel(c_ref, w_ref, b_ref, o_ref):
    c = c_ref[...]
    ca = (c * jax.nn.sigmoid(c)).astype(BF16)
    o_ref[0] = jnp.dot(ca, w_ref[0].astype(BF16), preferred_element_type=F32) + b_ref[0]


def _ada(c, w_ada, b_ada):
    depth, d, n = w_ada.shape
    bsz = c.shape[0]
    tn = 1024
    return pl.pallas_call(
        _ada_kernel,
        out_shape=jax.ShapeDtypeStruct((depth, bsz, n), F32),
        grid=(depth, n // tn),
        in_specs=[
            pl.BlockSpec((bsz, d), lambda l, j: (0, 0)),
            pl.BlockSpec((1, d, tn), lambda l, j: (l, 0, j)),
            pl.BlockSpec((1, 1, tn), lambda l, j: (l, 0, j)),
        ],
        out_specs=pl.BlockSpec((1, bsz, tn), lambda l, j: (l, 0, j)),
        compiler_params=_params(("parallel", "parallel")),
        name="ada_mod",
    )(c, w_ada, b_ada.reshape(depth, 1, n))


def _even_in_kernel(x_ref, sc_ref, sh_ref, w_ref, ct_ref, sa_ref, sb_ref, u_ref, *rest):
    qkv_refs, z_ref = rest[:9], rest[9]
    tm = x_ref.shape[0]
    h = _modulate(x_ref[...], sc_ref[0], sh_ref[0])
    u_ref[...] = jnp.dot(h, w_ref[:, 0:POOL_WIDTH], preferred_element_type=F32)
    ct, sa, sb = ct_ref[...], sa_ref[...], sb_ref[...]
    for n in range(9):
        lo = POOL_WIDTH + n * DIL_GROUP_WIDTH
        z = jnp.dot(h, w_ref[:, lo:lo + DIL_GROUP_WIDTH], preferred_element_type=F32)
        if n < 6:
            z = z * ct + pltpu.roll(z, 8, 1) * sa + pltpu.roll(z, DIL_GROUP_WIDTH - 8, 1) * sb
        if n < 3:
            z = z * (HEAD_DIM ** -0.5)
        dil = DIL_CONFIGS[n % 3][1]
        if dil == 1:
            qkv_refs[n][0, 0] = z.astype(BF16)
        else:
            for half in range(2):
                z_ref[half] = z[:, half * LANES:(half + 1) * LANES]
            for r in range(dil):
                rows = [z_ref[half, pl.ds(r, tm // dil, stride=dil), :] for half in range(2)]
                qkv_refs[n][0, r] = jnp.concatenate(rows, axis=1).astype(BF16)


def _even_in(x2, sc, sh, w_bf, tabs, seq):
    t, d = x2.shape
    tm = TOKEN_TILE
    tps = seq // tm
    bsz = t // seq
    n_in = w_bf.shape[1]
    mod_spec = pl.BlockSpec((1, 1, d), lambda i: (i // tps, 0, 0))
    tab_spec = pl.BlockSpec((tm, DIL_GROUP_WIDTH), lambda i: (i % tps, 0))
    out_spec = pl.BlockSpec((tm, DIL_GROUP_WIDTH), lambda i: (i, 0))
    dils = [dil for _, dil in DIL_CONFIGS] * 3
    qkv_shapes = [jax.ShapeDtypeStruct((bsz, dil, seq // dil, DIL_GROUP_WIDTH), BF16) for dil in dils]
    qkv_specs = [pl.BlockSpec((1, dil, tm // dil, DIL_GROUP_WIDTH), lambda i: (i // tps, 0, i % tps, 0))
                 for dil in dils]
    return pl.pallas_call(
        _even_in_kernel,
        out_shape=[jax.ShapeDtypeStruct((t, POOL_WIDTH), F32)] + qkv_shapes,
        grid=(t // tm,),
        in_specs=[
            pl.BlockSpec((tm, d), lambda i: (i, 0)),
            mod_spec,
            mod_spec,
            pl.BlockSpec((d, n_in), lambda i: (0, 0)),
            tab_spec,
            tab_spec,
            tab_spec,
        ],
        out_specs=[out_spec] + qkv_specs,
        scratch_shapes=[pltpu.VMEM((DIL_GROUP_WIDTH // LANES, tm, LANES), F32)],
        compiler_params=_params(("parallel",)),
        name="even_in_proj",
    )(x2, sc, sh, w_bf, *tabs)


def _rope_tables(seq):
    pos = jnp.arange(seq, dtype=F32)
    inv = ROPE_THETA ** (-jnp.arange(0, ROPE_DIM, 2, dtype=F32) / ROPE_DIM)
    ang = pos[:, None] * inv[None, :]
    cos, sin = jnp.cos(ang), jnp.sin(ang)
    half = ROPE_DIM // 2
    zeros = jnp.zeros((seq, half), F32)
    rest = HEAD_DIM - ROPE_DIM
    ct = jnp.concatenate([cos, cos, jnp.ones((seq, rest), F32)], axis=1)
    sa = jnp.concatenate([zeros, sin, jnp.zeros((seq, rest), F32)], axis=1)
    sb = jnp.concatenate([-sin, zeros, jnp.zeros((seq, rest), F32)], axis=1)
    return tuple(jnp.tile(a, (1, DIL_HEADS)) for a in (ct, sa, sb))


def _band_kernel(q_ref, kp_ref, kc_ref, vp_ref, vc_ref, o_ref, lse_ref):
    mt = pl.program_id(2)
    blk = BAND_BLOCK
    nblk = q_ref.shape[2] // blk
    kall = jnp.concatenate([kp_ref[0, 0], kc_ref[0, 0]], axis=0)
    vall = jnp.concatenate([vp_ref[0, 0], vc_ref[0, 0]], axis=0)
    qi = lax.broadcasted_iota(jnp.int32, (blk, 2 * blk), 0)
    kj = lax.broadcasted_iota(jnp.int32, (blk, 2 * blk), 1)
    dist = qi + blk - kj
    in_band = (dist >= 0) & (dist <= blk)
    lane_q = lax.broadcasted_iota(jnp.int32, (blk, DIL_GROUP_WIDTH), 1) // HEAD_DIM
    lane_v = lax.broadcasted_iota(jnp.int32, (2 * blk, DIL_GROUP_WIDTH), 1) // HEAD_DIM
    for i in range(nblk):
        q = q_ref[0, 0, i * blk:(i + 1) * blk, :]
        kk = kall[i * blk:(i + 2) * blk]
        vv = vall[i * blk:(i + 2) * blk]
        if i == 0:
            valid = in_band & (kj >= jnp.where(mt > 0, 0, blk))
        else:
            valid = in_band
        acc = jnp.zeros((blk, DIL_GROUP_WIDTH), F32)
        den_full = jnp.ones((blk, DIL_GROUP_WIDTH), F32)
        lse_full = jnp.zeros((blk, DIL_GROUP_WIDTH), F32)
        for h in range(DIL_HEADS):
            qh = jnp.where(lane_q == h, q, jnp.zeros_like(q))
            s = lax.dot_general(qh, kk, (((1,), (1,)), ((), ())), preferred_element_type=F32)
            s = jnp.where(valid, s, NEG)
            m = jnp.max(s, axis=-1, keepdims=True)
            p = jnp.exp(s - m)
            den = jnp.sum(p, axis=-1, keepdims=True)
            vh = jnp.where(lane_v == h, vv, jnp.zeros_like(vv))
            acc = acc + jnp.dot(p.astype(BF16), vh, preferred_element_type=F32)
            den_full = jnp.where(lane_q == h, den, den_full)
            lse_full = jnp.where(lane_q == h, m + jnp.log(den), lse_full)
        o_ref[0, 0, i * blk:(i + 1) * blk, :] = acc / den_full
        lse_ref[0, 0, i * blk:(i + 1) * blk, :] = lse_full


def _band_attention(q, k, v):
    bsz, dil, ln, w = q.shape
    nblk = min(BAND_BLOCKS_PER_STEP, ln // BAND_BLOCK)
    rows = nblk * BAND_BLOCK
    cur = pl.BlockSpec((1, 1, rows, w), lambda b, r, m: (b, r, m, 0))
    prev = pl.BlockSpec((1, 1, BAND_BLOCK, w), lambda b, r, m: (b, r, jnp.maximum(m * nblk - 1, 0), 0))
    return pl.pallas_call(
        _band_kernel,
        out_shape=[jax.ShapeDtypeStruct(q.shape, F32)] * 2,
        grid=(bsz, dil, ln // rows),
        in_specs=[cur, prev, cur, prev, cur],
        out_specs=[cur, cur],
        compiler_params=_params(("parallel", "parallel", "parallel")),
        name=f"band_attn_d{dil}",
    )(q, k, k, v, v)


def _even_out_kernel(tps, u_ref, up_ref, o0, o1, o2, l0, l1, l2, wp_ref, ps_ref, wo_ref, x_ref, gate_ref,
                     g_ref, b_ref, out_ref, il_ref):
    i = pl.program_id(0)
    tm = u_ref.shape[0]
    halo = up_ref.shape[0]
    u = u_ref[...]
    prev = jnp.where(i % tps == 0, jnp.zeros_like(up_ref[...]), up_ref[...])
    ext = jnp.concatenate([prev, u], axis=0)
    lane_g = lax.broadcasted_iota(jnp.int32, (tm, POOL_WIDTH), 1) // POOL_GROUP_DIM
    row = lax.broadcasted_iota(jnp.int32, (tm, POOL_WIDTH), 0)
    pos1 = ((i % tps) * tm + row + 1).astype(F32)
    s = ext
    pooled = jnp.zeros((tm, POOL_WIDTH), F32)
    wvec = jnp.zeros((tm, POOL_WIDTH), F32)
    shift = 1
    for g, w in enumerate(POOL_WINDOWS):
        while shift < w:
            s = s + pltpu.roll(s, shift, 0)
            shift *= 2
        pooled = jnp.where(lane_g == g, s[halo:, :], pooled)
        wvec = jnp.where(lane_g == g, float(w), wvec)
    pooled = pooled / jnp.minimum(pos1, wvec) - u
    a_out = jnp.dot(pooled.astype(BF16), wp_ref[...], preferred_element_type=F32) * ps_ref[...]
    def token_order(ref, buf):
        dil = ref.shape[1]
        if dil == 1:
            return ref[0, 0]
        for r in range(dil):
            blk = ref[0, r]
            for half in range(2):
                buf[half, pl.ds(r, tm // dil, stride=dil), :] = blk[:, half * LANES:(half + 1) * LANES]
        return jnp.concatenate([buf[0], buf[1]], axis=1)

    la, lb, lc = [token_order(ref, il_ref.at[n]) for n, ref in enumerate((l0, l1, l2))]
    oa, ob, oc = [token_order(ref, il_ref.at[3 + n]) for n, ref in enumerate((o0, o1, o2))]
    mx = jnp.maximum(jnp.maximum(la, lb), lc)
    ea, eb, ec = jnp.exp(la - mx), jnp.exp(lb - mx), jnp.exp(lc - mx)
    b_out = (ea * oa + eb * ob + ec * oc) / (ea + eb + ec)
    mixed = jnp.concatenate([a_out, b_out], axis=-1).astype(BF16)
    sub = jnp.dot(mixed, wo_ref[...], preferred_element_type=F32)
    out_ref[...] = _deepnorm_ln(x_ref[...], sub, gate_ref[0], g_ref[...], b_ref[...])


def _even_out(u, os_, lses, wpool_bd, pool_scale, wo_bf, x2, gate, ln_g, ln_b, seq):
    t, d = x2.shape
    tm = TOKEN_TILE
    tps = seq // tm
    halo = 16
    row = lambda w: pl.BlockSpec((tm, w), lambda i: (i, 0))
    full = lambda a: pl.BlockSpec(a.shape, lambda i: (0,) * a.ndim)
    strided = [pl.BlockSpec((1, a.shape[1], tm // a.shape[1], DIL_GROUP_WIDTH), lambda i: (i // tps, 0, i % tps, 0))
               for a in (*os_, *lses)]
    return pl.pallas_call(
        functools.partial(_even_out_kernel, tps),
        out_shape=jax.ShapeDtypeStruct((t, d), F32),
        grid=(t // tm,),
        in_specs=[
            row(POOL_WIDTH),
            pl.BlockSpec((halo, POOL_WIDTH), lambda i: (jnp.maximum(i * (tm // halo) - 1, 0), 0)),
            *strided,
            full(wpool_bd),
            full(pool_scale),
            full(wo_bf),
            row(d),
            pl.BlockSpec((1, 1, d), lambda i: (i // tps, 0, 0)),
            full(ln_g),
            full(ln_b),
        ],
        out_specs=row(d),
        scratch_shapes=[pltpu.VMEM((6, DIL_GROUP_WIDTH // LANES, tm, LANES), F32)],
        compiler_params=_params(("parallel",)),
        name="even_out_proj",
    )(u, u, *os_, *lses, wpool_bd, pool_scale, wo_bf, x2, gate, ln_g, ln_b)


def _swiglu_rows(x, wgu_ref, wd_ref, acc_ref):
    for c in range(D_FF // FF_SUB):
        lo = c * FF_SUB
        gt = jnp.dot(x, wgu_ref[0, :, lo:lo + FF_SUB], preferred_element_type=F32)
        up = jnp.dot(x, wgu_ref[0, :, D_FF + lo:D_FF + lo + FF_SUB], preferred_element_type=F32)
        a = (gt * jax.nn.sigmoid(gt) * up).astype(BF16)
        part = jnp.dot(a, wd_ref[0, lo:lo + FF_SUB, :], preferred_element_type=F32)
        if c == 0:
            acc_ref[...] = part
        else:
            acc_ref[...] += part


def _ffn_kernel(x_ref, sc_ref, sh_ref, wgu_ref, wd_ref, gate_ref, g_ref, b_ref, out_ref, acc_ref):
    _swiglu_rows(_modulate(x_ref[...], sc_ref[0], sh_ref[0]), wgu_ref, wd_ref, acc_ref)
    out_ref[...] = _deepnorm_ln(x_ref[...], acc_ref[...], gate_ref[0], g_ref[...], b_ref[...])


def _ffn(x2, sc, sh, wgu_bf, wd_bf, gate, ln_g, ln_b, seq):
    t, d = x2.shape
    tm = TOKEN_TILE
    tps = seq // tm
    mod = pl.BlockSpec((1, 1, d), lambda i: (i // tps, 0, 0))
    row = pl.BlockSpec((tm, d), lambda i: (i, 0))
    vec = pl.BlockSpec((1, d), lambda i: (0, 0))
    full = lambda a: pl.BlockSpec(a.shape, lambda i: (0,) * a.ndim)
    return pl.pallas_call(
        _ffn_kernel,
        out_shape=jax.ShapeDtypeStruct((t, d), F32),
        grid=(t // tm,),
        in_specs=[row, mod, mod, full(wgu_bf), full(wd_bf), mod, vec, vec],
        out_specs=row,
        scratch_shapes=[pltpu.VMEM((tm, d), F32)],
        compiler_params=_params(("parallel",)),
        name="ffn_dense",
    )(x2, sc, sh, wgu_bf, wd_bf, gate, ln_g, ln_b)


def _split3(x):
    hi = x.astype(BF16)
    r = x - hi.astype(F32)
    mid = r.astype(BF16)
    lo = (r - mid.astype(F32)).astype(BF16)
    return hi, mid, lo


def _odd_in_kernel(tps, x_ref, sc_ref, sh_ref, w_ref, wf_ref, bf_ref, place_ref, q_ref, k_ref, v_ref, f_ref,
                   carry_ref):
    i = pl.program_id(0)
    tm = x_ref.shape[0]
    h = _modulate(x_ref[...], sc_ref[0], sh_ref[0])
    q = jnp.dot(h, w_ref[:, 0:FOX_WIDTH], preferred_element_type=F32)
    q_ref[...] = (q * (HEAD_DIM ** -0.5 * LOG2E)).astype(BF16)
    k_ref[...] = jnp.dot(h, w_ref[:, FOX_WIDTH:2 * FOX_WIDTH], preferred_element_type=F32).astype(BF16)
    v_ref[...] = jnp.dot(h, w_ref[:, 2 * FOX_WIDTH:3 * FOX_WIDTH], preferred_element_type=F32).astype(BF16)
    f = jnp.dot(h, wf_ref[...], preferred_element_type=F32) + bf_ref[...]
    logf = jnp.minimum(f, 0.0) - jnp.log(1.0 + jnp.exp(-jnp.abs(f)))
    r = lax.broadcasted_iota(jnp.int32, (tm, tm), 0)
    cidx = lax.broadcasted_iota(jnp.int32, (tm, tm), 1)
    tri = jnp.where(cidx <= r, 1.0, 0.0).astype(BF16)
    hi, mid, lo = _split3(logf)
    cs = (jnp.dot(tri, lo, preferred_element_type=F32) + jnp.dot(tri, mid, preferred_element_type=F32)
          + jnp.dot(tri, hi, preferred_element_type=F32))

    @pl.when(i % tps == 0)
    def _():
        carry_ref[...] = jnp.zeros_like(carry_ref)

    cs = cs + carry_ref[...]
    carry_ref[...] = cs[tm - 1:tm, :]
    parts = jnp.concatenate(_split3(cs * LOG2E), axis=1)
    f_ref[...] = jnp.dot(parts, place_ref[...], preferred_element_type=F32).astype(BF16)


def _gate_placement():
    place = np.zeros((3 * LANES, FOX_WIDTH), np.float32)
    for h in range(FOX_HEADS):
        base = (h // 2) * LANES + (HEAD_DIM if h % 2 == 0 else 0)
        for p in range(3):
            place[p * LANES + h, base + p] = 1.0
    return jnp.asarray(place, BF16)


def _odd_in(x2, sc, sh, w_bf, wf_bf, bf_pad, seq):
    t, d = x2.shape
    tm = TOKEN_TILE
    tps = seq // tm
    mod = pl.BlockSpec((1, 1, d), lambda i: (i // tps, 0, 0))
    row = lambda w: pl.BlockSpec((tm, w), lambda i: (i, 0))
    full = lambda a: pl.BlockSpec(a.shape, lambda i: (0,) * a.ndim)
    place = _gate_placement()
    return pl.pallas_call(
        functools.partial(_odd_in_kernel, tps),
        out_shape=[jax.ShapeDtypeStruct((t, FOX_WIDTH), BF16)] * 4,
        grid=(t // tm,),
        in_specs=[row(d), mod, mod, full(w_bf), full(wf_bf), full(bf_pad), full(place)],
        out_specs=[row(FOX_WIDTH)] * 4,
        scratch_shapes=[pltpu.VMEM((1, LANES), F32)],
        compiler_params=_params(("arbitrary",)),
        name="odd_in_proj",
    )(x2, sc, sh, w_bf, wf_bf, bf_pad, place)


def _fox_kernel(q_ref, k_ref, v_ref, f_ref, o_ref, acc_ref, m_ref, s0_ref, s1_ref, p0_ref, p1_ref):
    qi = pl.program_id(2)
    tq = q_ref.shape[1]
    tk = FOX_K_TILE
    half = HEAD_DIM
    q = q_ref[0]
    lane_q = lax.broadcasted_iota(jnp.int32, q.shape, 1)
    lane_k = lax.broadcasted_iota(jnp.int32, (tk, LANES), 1)
    qa = jnp.where(lane_q < half, q, jnp.where(lane_q < half + 3, -1.0, 0.0).astype(BF16))
    qb = jnp.where(lane_q >= half, q, jnp.where(lane_q < 3, -1.0, 0.0).astype(BF16))
    ones_a = jnp.where(lane_k == half, 1.0, 0.0).astype(BF16)
    ones_b = jnp.where(lane_k == 0, 1.0, 0.0).astype(BF16)
    acc_ref[...] = jnp.zeros_like(acc_ref)
    m_ref[...] = jnp.full(m_ref.shape, NEG, F32)

    def scores(j, s_ref):
        start = pl.multiple_of(j * tk, tk)
        k = k_ref[0, pl.ds(start, tk), :]
        fa = f_ref[0, pl.ds(start, tk), :]
        ka = jnp.where(lane_k < half, k, fa)
        kb = jnp.where(lane_k >= half, k, fa)
        for hh, (qh, kh) in enumerate(((qa, ka), (qb, kb))):
            s_ref[hh] = lax.dot_general(qh, kh, (((1,), (1,)), ((), ())), preferred_element_type=F32)

    def attend(j, s_ref, p_ref, masked):
        start = pl.multiple_of(j * tk, tk)
        v = v_ref[0, pl.ds(start, tk), :]
        va = jnp.where(lane_k < half, v, ones_a)
        vb = jnp.where(lane_k >= half, v, ones_b)
        for hh, vh in enumerate((va, vb)):
            for r0 in range(0, tq, FOX_ROW_CHUNK):
                rows = pl.ds(r0, FOX_ROW_CHUNK)
                s = s_ref[hh, rows, :]
                if masked:
                    rr = lax.broadcasted_iota(jnp.int32, s.shape, 0) + (qi * tq + r0)
                    cc = lax.broadcasted_iota(jnp.int32, s.shape, 1) + start
                    s = jnp.where(cc <= rr, s, NEG)
                m_old = m_ref[hh, rows, :]
                m_new = jnp.maximum(m_old, jnp.max(s, axis=-1, keepdims=True))
                p_ref[hh, rows, :] = jnp.exp2(s - jnp.tile(m_new, (1, tk // LANES))).astype(BF16)
                acc_ref[hh, rows, :] = acc_ref[hh, rows, :] * jnp.exp2(m_old - m_new)
                m_ref[hh, rows, :] = m_new
            acc_ref[hh] += jnp.dot(p_ref[hh], vh, preferred_element_type=F32)

    scores(0, s0_ref)

    def body(jp, carry):
        scores(2 * jp + 1, s1_ref)
        attend(2 * jp, s0_ref, p0_ref, False)
        scores(2 * jp + 2, s0_ref)
        attend(2 * jp + 1, s1_ref, p1_ref, False)
        return carry

    lax.fori_loop(0, qi // 2, body, 0)

    @pl.when(qi % 2 == 0)
    def _():
        attend(qi, s0_ref, p0_ref, True)

    @pl.when(qi % 2 == 1)
    def _():
        scores(qi, s1_ref)
        attend(qi - 1, s0_ref, p0_ref, False)
        attend(qi, s1_ref, p1_ref, True)

    acc_a, acc_b = acc_ref[0], acc_ref[1]
    lane_o = lax.broadcasted_iota(jnp.int32, (tq, LANES), 1)
    o = jnp.where(lane_o < half, acc_a / acc_a[:, half:half + 1], acc_b / acc_b[:, 0:1])
    o_ref[0] = o.astype(BF16)


def _fox_attention(q, k, v, fa, bsz, seq):
    tq = FOX_Q_TILE
    npairs = FOX_HEADS // 2
    tile = pl.BlockSpec((1, tq, LANES), lambda b, j, i: (b, i, j))
    whole = pl.BlockSpec((1, seq, LANES), lambda b, j, i: (b, 0, j))
    return pl.pallas_call(
        _fox_kernel,
        out_shape=jax.ShapeDtypeStruct((bsz, seq, FOX_WIDTH), BF16),
        grid=(bsz, npairs, seq // tq),
        in_specs=[tile, whole, whole, whole],
        out_specs=tile,
        scratch_shapes=[pltpu.VMEM((2, tq, LANES), F32), pltpu.VMEM((2, tq, LANES), F32),
                        pltpu.VMEM((2, tq, FOX_K_TILE), F32), pltpu.VMEM((2, tq, FOX_K_TILE), F32),
                        pltpu.VMEM((2, tq, FOX_K_TILE), BF16), pltpu.VMEM((2, tq, FOX_K_TILE), BF16)],
        compiler_params=_params(("parallel", "parallel", "arbitrary")),
        name="fox_attn",
    )(q, k, v, fa)


def _odd_out_kernel(o_ref, wo_ref, x_ref, gate_ref, g_ref, b_ref, out_ref):
    sub = jnp.dot(o_ref[...], wo_ref[...], preferred_element_type=F32)
    out_ref[...] = _deepnorm_ln(x_ref[...], sub, gate_ref[0], g_ref[...], b_ref[...])


def _odd_out(o2, wo_bf, x2, gate, ln_g, ln_b, seq):
    t, d = x2.shape
    tm = TOKEN_TILE
    tps = seq // tm
    row = lambda w: pl.BlockSpec((tm, w), lambda i: (i, 0))
    full = lambda a: pl.BlockSpec(a.shape, lambda i: (0,) * a.ndim)
    return pl.pallas_call(
        _odd_out_kernel,
        out_shape=jax.ShapeDtypeStruct((t, d), F32),
        grid=(t // tm,),
        in_specs=[row(FOX_WIDTH), full(wo_bf), row(d), pl.BlockSpec((1, 1, d), lambda i: (i // tps, 0, 0)),
                  full(ln_g), full(ln_b)],
        out_specs=row(d),
        compiler_params=_params(("parallel",)),
        name="odd_out_proj",
    )(o2, wo_bf, x2, gate, ln_g, ln_b)


def _router_kernel(x_ref, sc_ref, sh_ref, wrt_ref, tri_ref, rankt_ref, cnt_ref, comb_ref, rank_ref):
    c = x_ref.shape[0]
    h = _modulate(x_ref[...], sc_ref[0], sh_ref[0])
    lt = lax.dot_general(wrt_ref[...], h, (((1,), (1,)), ((), ())), preferred_element_type=F32)[0:N_EXPERTS]
    row = lax.broadcasted_iota(jnp.int32, lt.shape, 0)
    m1 = jnp.max(lt, axis=0, keepdims=True)
    i1 = jnp.min(jnp.where(lt == m1, row, N_EXPERTS), axis=0, keepdims=True)
    rest = jnp.where(row == i1, -jnp.inf, lt)
    m2 = jnp.max(rest, axis=0, keepdims=True)
    i2 = jnp.min(jnp.where(rest == m2, row, N_EXPERTS), axis=0, keepdims=True)
    e2 = jnp.exp(m2 - m1)
    g1 = 1.0 / (1.0 + e2)
    g2 = e2 / (1.0 + e2)
    combt = jnp.where(row == i1, g1, jnp.where(row == i2, g2, 0.0))
    sel = jnp.where((row == i1) | (row == i2), 1.0, 0.0)
    sel16 = jnp.concatenate([sel, jnp.zeros_like(sel)], axis=0).astype(BF16)
    earlier = jnp.dot(sel16, tri_ref[...], preferred_element_type=F32)[0:N_EXPERTS]
    rankt = jnp.where(sel > 0.0, earlier, NO_ROW)
    rankt_ref[...] = rankt
    cnt_ref[0] = jnp.broadcast_to(jnp.sum(sel, axis=1, keepdims=True), (N_EXPERTS, LANES))
    pad = LANES - N_EXPERTS
    comb_ref[...] = jnp.concatenate([combt, jnp.zeros((pad, c), F32)], axis=0).T
    rank_ref[...] = jnp.concatenate([rankt, jnp.full((pad, c), NO_ROW, F32)], axis=0).T


def _router(x2, sc, sh, wrt, seq):
    t, d = x2.shape
    cs = MOE_CHUNK
    cps = seq // cs
    mod = pl.BlockSpec((1, 1, d), lambda i: (i // cps, 0, 0))
    tri = jnp.asarray(np.triu(np.ones((cs, cs), np.float32), 1), BF16)
    return pl.pallas_call(
        _router_kernel,
        out_shape=[jax.ShapeDtypeStruct((N_EXPERTS, t), F32),
                   jax.ShapeDtypeStruct((t // cs, N_EXPERTS, LANES), F32),
                   jax.ShapeDtypeStruct((t, LANES), F32),
                   jax.ShapeDtypeStruct((t, LANES), F32)],
        grid=(t // cs,),
        in_specs=[pl.BlockSpec((cs, d), lambda i: (i, 0)), mod, mod,
                  pl.BlockSpec(wrt.shape, lambda i: (0, 0)), pl.BlockSpec(tri.shape, lambda i: (0, 0))],
        out_specs=[pl.BlockSpec((N_EXPERTS, cs), lambda i: (0, i)),
                   pl.BlockSpec((1, N_EXPERTS, LANES), lambda i: (i, 0, 0)),
                   pl.BlockSpec((cs, LANES), lambda i: (i, 0)),
                   pl.BlockSpec((cs, LANES), lambda i: (i, 0))],
        compiler_params=_params(("parallel",)),
        name="router",
    )(x2, sc, sh, wrt, tri)


def _route_meta(cnt, n_tiles):
    before = jnp.cumsum(cnt, axis=0) - cnt
    total = jnp.sum(cnt, axis=0)
    region = ((total + MOE_TILE - 1) // MOE_TILE) * MOE_TILE
    base = jnp.cumsum(region) - region
    n_used = (jnp.sum(region) // MOE_TILE).reshape(1)
    starts = jnp.arange(n_tiles, dtype=jnp.int32) * MOE_TILE
    tile_expert = jnp.minimum(jnp.sum(starts[:, None] >= (base + region)[None, :], axis=1), N_EXPERTS - 1)
    i32 = lambda a: a.astype(jnp.int32)
    return i32(before.reshape(-1)), i32(cnt.reshape(-1)), i32(base), i32(total), i32(tile_expert), i32(n_used)


def _chunk_block(before_ref, cnt_ref, chunk, e, j):
    g = before_ref[chunk * N_EXPERTS + e]
    n = cnt_ref[chunk * N_EXPERTS + e]
    k = g // ROW_BLOCK + j
    return g, n, k, (n > 0) & (k * ROW_BLOCK < g + n)


def _dispatch_kernel(before_ref, cnt_ref, base_ref, total_ref, x_ref, sc_ref, sh_ref, rankt_ref, xs_ref,
                     stage_ref, slot_ref, zero_ref, sem, fsem, zsem, usem):
    c = pl.program_id(0)
    last = c == pl.num_programs(0) - 1
    h = _modulate(x_ref[...], sc_ref[0], sh_ref[0])

    @pl.when(c == 0)
    def _():
        stage_ref[...] = jnp.zeros_like(stage_ref)
        zero_ref[...] = jnp.zeros_like(zero_ref)

    block_row = lax.broadcasted_iota(jnp.int32, (ROW_BLOCK, MOE_CHUNK), 0).astype(F32)
    pair_row = lax.broadcasted_iota(jnp.int32, (2 * ROW_BLOCK, MOE_CHUNK), 0).astype(F32)

    def block_copy(e, j, k):
        row0 = pl.multiple_of(base_ref[e] + k * ROW_BLOCK, ROW_BLOCK)
        return pltpu.make_async_copy(slot_ref.at[e, j], xs_ref.at[pl.ds(row0, ROW_BLOCK)], sem.at[e, j])

    def wait_block_copies(chunk):
        for e in range(N_EXPERTS):
            for j in range(MAX_SPAN):
                g, n, k, touched = _chunk_block(before_ref, cnt_ref, chunk, e, j)

                @pl.when(touched & (g + n >= (k + 1) * ROW_BLOCK))
                def _():
                    block_copy(e, j, k).wait()

    selectors = []
    for e in range(N_EXPERTS):
        g = before_ref[c * N_EXPERTS + e]
        off = (g % ROW_BLOCK).astype(F32)
        selectors.append(jnp.where(rankt_ref[e:e + 1, :] + off == pair_row, 1.0, 0.0).astype(BF16))
    first_two = jnp.dot(jnp.concatenate(selectors, axis=0), h, preferred_element_type=F32)

    @pl.when(c > 0)
    def _():
        wait_block_copies(c - 1)

    for e in range(N_EXPERTS):
        rel = rankt_ref[e:e + 1, :]
        for j in range(MAX_SPAN):
            g, n, k, touched = _chunk_block(before_ref, cnt_ref, c, e, j)
            complete = g + n >= (k + 1) * ROW_BLOCK

            @pl.when(touched)
            def _():
                if j < 2:
                    r0 = (2 * e + j) * ROW_BLOCK
                    blk = first_two[r0:r0 + ROW_BLOCK]
                else:
                    off = (g - k * ROW_BLOCK).astype(F32)
                    onehot = jnp.where(rel + off == block_row, 1.0, 0.0).astype(BF16)
                    blk = jnp.dot(onehot, h, preferred_element_type=F32)
                if j == 0:
                    blk = blk + stage_ref[e].astype(F32)
                blk = blk.astype(BF16)
                slot_ref[e, j] = blk
                stage_ref[e] = jnp.where(complete, jnp.zeros_like(blk), blk)

                @pl.when(complete)
                def _():
                    block_copy(e, j, k).start()

    def tail_copies(e):
        total = total_ref[e]
        n_data = (total + ROW_BLOCK - 1) // ROW_BLOCK
        n_region = ((total + MOE_TILE - 1) // MOE_TILE) * (MOE_TILE // ROW_BLOCK)
        part_row = pl.multiple_of(base_ref[e] + (total // ROW_BLOCK) * ROW_BLOCK, ROW_BLOCK)
        part = pltpu.make_async_copy(stage_ref.at[e], xs_ref.at[pl.ds(part_row, ROW_BLOCK)], fsem.at[e])
        out = [(total % ROW_BLOCK != 0, part)]
        for z in range(MOE_TILE // ROW_BLOCK - 1):
            zrow = pl.multiple_of(base_ref[e] + (n_data + z) * ROW_BLOCK, ROW_BLOCK)
            zc = pltpu.make_async_copy(zero_ref, xs_ref.at[pl.ds(zrow, ROW_BLOCK)], zsem.at[e, z])
            out.append((n_data + z < n_region, zc))
        return out

    def unused_copies():
        e = N_EXPERTS - 1
        used = base_ref[e] + ((total_ref[e] + MOE_TILE - 1) // MOE_TILE) * MOE_TILE
        out = []
        for z in range(usem.shape[0]):
            zrow = pl.multiple_of(used + z * ROW_BLOCK, ROW_BLOCK)
            zc = pltpu.make_async_copy(zero_ref, xs_ref.at[pl.ds(zrow, ROW_BLOCK)], usem.at[z])
            out.append((zrow < xs_ref.shape[0], zc))
        return out

    def all_tail_copies():
        return [cc for e in range(N_EXPERTS) for cc in tail_copies(e)] + unused_copies()

    @pl.when(last)
    def _():
        for cond, cp in all_tail_copies():
            @pl.when(cond)
            def _():
                cp.start()

    @pl.when(last)
    def _():
        wait_block_copies(c)

    @pl.when(last)
    def _():
        for cond, cp in all_tail_copies():
            @pl.when(cond)
            def _():
                cp.wait()


def _dispatch(meta, x2, sc, sh, rankt, n_tiles, seq):
    before, cnt, base, total = meta
    t, d = x2.shape
    cs = MOE_CHUNK
    cps = seq // cs
    mod = pl.BlockSpec((1, 1, d), lambda i, *_: (i // cps, 0, 0))
    grid_spec = pltpu.PrefetchScalarGridSpec(
        num_scalar_prefetch=4,
        grid=(t // cs,),
        in_specs=[pl.BlockSpec((cs, d), lambda i, *_: (i, 0)), mod, mod,
                  pl.BlockSpec((N_EXPERTS, cs), lambda i, *_: (0, i))],
        out_specs=pl.BlockSpec(memory_space=pl.ANY),
        scratch_shapes=[pltpu.VMEM((N_EXPERTS, ROW_BLOCK, d), BF16),
                        pltpu.VMEM((N_EXPERTS, MAX_SPAN, ROW_BLOCK, d), BF16),
                        pltpu.VMEM((ROW_BLOCK, d), BF16),
                        pltpu.SemaphoreType.DMA((N_EXPERTS, MAX_SPAN)),
                        pltpu.SemaphoreType.DMA((N_EXPERTS,)),
                        pltpu.SemaphoreType.DMA((N_EXPERTS, MOE_TILE // ROW_BLOCK - 1)),
                        pltpu.SemaphoreType.DMA(((n_tiles * MOE_TILE - 2 * t) // ROW_BLOCK,))],
    )
    return pl.pallas_call(
        _dispatch_kernel,
        out_shape=jax.ShapeDtypeStruct((n_tiles * MOE_TILE, d), BF16),
        grid_spec=grid_spec,
        compiler_params=_params(("arbitrary",)),
        name="moe_dispatch",
    )(before, cnt, base, total, x2, sc, sh, rankt)


def _experts_kernel(te_ref, nu_ref, x_ref, wgu_ref, wd_ref, y_ref, acc_ref):
    @pl.when(pl.program_id(0) >= nu_ref[0])
    def _():
        y_ref[...] = jnp.zeros_like(y_ref)

    @pl.when(pl.program_id(0) < nu_ref[0])
    def _():
        _swiglu_rows(x_ref[...], wgu_ref, wd_ref, acc_ref)
        y_ref[...] = acc_ref[...].astype(BF16)


def _experts(tile_expert, n_used, xs, wgu_bf, wd_bf):
    rows, d = xs.shape
    tm = MOE_TILE
    clamp = lambda i, nu: jnp.minimum(i, nu[0] - 1)
    grid_spec = pltpu.PrefetchScalarGridSpec(
        num_scalar_prefetch=2,
        grid=(rows // tm,),
        in_specs=[pl.BlockSpec((tm, d), lambda i, te, nu: (clamp(i, nu), 0)),
                  pl.BlockSpec((1, d, 2 * D_FF), lambda i, te, nu: (te[clamp(i, nu)], 0, 0)),
                  pl.BlockSpec((1, D_FF, d), lambda i, te, nu: (te[clamp(i, nu)], 0, 0))],
        out_specs=pl.BlockSpec((tm, d), lambda i, te, nu: (i, 0)),
        scratch_shapes=[pltpu.VMEM((tm, d), F32)],
    )
    return pl.pallas_call(
        _experts_kernel,
        out_shape=jax.ShapeDtypeStruct((rows, d), BF16),
        grid_spec=grid_spec,
        compiler_params=_params(("arbitrary",)),
        name="moe_experts",
    )(tile_expert, n_used, xs, wgu_bf, wd_bf)


def _combine_kernel(before_ref, cnt_ref, base_ref, rank_ref, comb_ref, x_ref, gate_ref, g_ref, b_ref, y_ref,
                    out_ref, ybuf, xbuf, acc_ref, sem, xsem):
    c = pl.program_id(0)
    slot = c % 2

    def window(chunk, e, w):
        g = before_ref[chunk * N_EXPERTS + e]
        n = cnt_ref[chunk * N_EXPERTS + e]
        first = (g // ROW_BLOCK) * ROW_BLOCK + w * COMBINE_WIN
        row0 = pl.multiple_of(jnp.minimum(base_ref[e] + first, y_ref.shape[0] - COMBINE_WIN), ROW_BLOCK)
        off = (base_ref[e] + g - row0).astype(F32)
        return (n > 0) & (first < g + n), row0, off

    def first_copy(chunk, sl, e):
        touched, row0, _ = window(chunk, e, 0)
        return touched, pltpu.make_async_copy(y_ref.at[pl.ds(row0, COMBINE_WIN)], ybuf.at[sl, e], sem.at[sl, e])

    def fetch(chunk, sl):
        for e in range(N_EXPERTS):
            touched, cp = first_copy(chunk, sl, e)

            @pl.when(touched)
            def _():
                cp.start()

    @pl.when(c == 0)
    def _():
        ybuf[...] = jnp.zeros_like(ybuf)
        fetch(0, 0)

    @pl.when(c + 1 < pl.num_programs(0))
    def _():
        fetch(c + 1, 1 - slot)

    lane = lax.broadcasted_iota(jnp.int32, (MOE_CHUNK, COMBINE_WIN), 1).astype(F32)
    acc_ref[...] = jnp.zeros_like(acc_ref)
    for e in range(N_EXPERTS):
        col = rank_ref[:, e:e + 1]
        weight = comb_ref[:, e:e + 1]
        touched, _, off = window(c, e, 0)

        @pl.when(touched)
        def _():
            first_copy(c, slot, e)[1].wait()

        onehot = jnp.where(col + off == lane, 1.0, 0.0).astype(BF16)
        acc_ref[...] += weight * jnp.dot(onehot, ybuf[slot, e], preferred_element_type=F32)
        for w in range(1, COMBINE_MAX_WIN):
            touched, row0, off = window(c, e, w)

            @pl.when(touched)
            def _():
                cp = pltpu.make_async_copy(y_ref.at[pl.ds(row0, COMBINE_WIN)], xbuf.at[e, w - 1], xsem.at[e, w - 1])
                cp.start()
                cp.wait()
                onehot = jnp.where(col + off == lane, 1.0, 0.0).astype(BF16)
                acc_ref[...] += weight * jnp.dot(onehot, xbuf[e, w - 1], preferred_element_type=F32)
    out_ref[...] = _deepnorm_ln(x_ref[...], acc_ref[...], gate_ref[0], g_ref[...], b_ref[...])


def _combine(meta, rank, comb, x2, gate, ln_g, ln_b, y, seq):
    before, cnt, base, _ = meta
    t, d = x2.shape
    cs = MOE_CHUNK
    cps = seq // cs
    row = lambda w: pl.BlockSpec((cs, w), lambda i, *_: (i, 0))
    vec = pl.BlockSpec((1, d), lambda i, *_: (0, 0))
    grid_spec = pltpu.PrefetchScalarGridSpec(
        num_scalar_prefetch=3,
        grid=(t // cs,),
        in_specs=[row(LANES), row(LANES), row(d), pl.BlockSpec((1, 1, d), lambda i, *_: (i // cps, 0, 0)),
                  vec, vec, pl.BlockSpec(memory_space=pl.ANY)],
        out_specs=row(d),
        scratch_shapes=[pltpu.VMEM((2, N_EXPERTS, COMBINE_WIN, d), BF16),
                        pltpu.VMEM((N_EXPERTS, COMBINE_MAX_WIN - 1, COMBINE_WIN, d), BF16),
                        pltpu.VMEM((cs, d), F32),
                        pltpu.SemaphoreType.DMA((2, N_EXPERTS)),
                        pltpu.SemaphoreType.DMA((N_EXPERTS, COMBINE_MAX_WIN - 1))],
    )
    return pl.pallas_call(
        _combine_kernel,
        out_shape=jax.ShapeDtypeStruct((t, d), F32),
        grid_spec=grid_spec,
        compiler_params=_params(("arbitrary",)),
        name="moe_combine",
    )(before, cnt, base, rank, comb, x2, gate, ln_g, ln_b, y)


def _block_diag(w_pool):
    g, c, _ = w_pool.shape
    out = jnp.zeros((g * c, g * c), w_pool.dtype)
    for i in range(g):
        out = out.at[i * c:(i + 1) * c, i * c:(i + 1) * c].set(w_pool[i])
    return out


def kernel(x, c, w_ada, b_ada, ln_g, ln_b, w_in_even, w_pool, pool_scale, w_out_even, w_ffn_gu, w_ffn_down,
           w_in_odd, b_forget, w_out_odd, w_router, w_exp_gu, w_exp_down):
    bsz, seq, d = x.shape
    t = bsz * seq
    tabs = _rope_tables(seq)
    ada = _ada(c, w_ada, b_ada)
    x2 = x.reshape(t, d)
    for l in range(DEPTH):
        i = l // 2
        sh1, sc1, g1, sh2, sc2, g2 = [ada[l, :, n * d:(n + 1) * d].reshape(bsz, 1, d) for n in range(6)]
        lng = ln_g[l].reshape(2, 1, d)
        lnb = ln_b[l].reshape(2, 1, d)
        if l % 2 == 0:
            outs = _even_in(x2, sc1, sh1, w_in_even[i].astype(BF16), tabs, seq)
            u, qkv = outs[0], outs[1:]
            os_, lses = [], []
            for g in range(len(DIL_CONFIGS)):
                o, lse = _band_attention(qkv[g], qkv[3 + g], qkv[6 + g])
                os_.append(o)
                lses.append(lse)
            x2 = _even_out(u, os_, lses, _block_diag(w_pool[i]).astype(BF16), pool_scale[i].reshape(1, -1),
                           w_out_even[i].astype(BF16), x2, g1, lng[0], lnb[0], seq)
            x2 = _ffn(x2, sc2, sh2, w_ffn_gu[i][None].astype(BF16), w_ffn_down[i][None].astype(BF16), g2, lng[1],
                      lnb[1], seq)
        else:
            w_in = w_in_odd[i]
            wf = jnp.zeros((d, LANES), F32).at[:, :FOX_HEADS].set(w_in[:, 3 * FOX_WIDTH:]).astype(BF16)
            bfp = jnp.zeros((1, LANES), F32).at[0, :FOX_HEADS].set(b_forget[i])
            q, k, v, fa = [a.reshape(bsz, seq, FOX_WIDTH) for a in
                           _odd_in(x2, sc1, sh1, w_in[:, :3 * FOX_WIDTH].astype(BF16), wf, bfp, seq)]
            o = _fox_attention(q, k, v, fa, bsz, seq)
            x2 = _odd_out(o.reshape(t, FOX_WIDTH), w_out_odd[i].astype(BF16), x2, g1, lng[0], lnb[0], seq)
            wrt = jnp.zeros((2 * N_EXPERTS, d), F32).at[:N_EXPERTS].set(w_router[i].T).astype(BF16)
            rankt, cnt, comb, rank = _router(x2, sc2, sh2, wrt, seq)
            n_tiles = (2 * t + N_EXPERTS * (MOE_TILE - 1) + MOE_TILE - 1) // MOE_TILE
            *meta, tile_expert, n_used = _route_meta(cnt[:, :, 0].astype(jnp.int32), n_tiles)
            xs = _dispatch(meta, x2, sc2, sh2, rankt, n_tiles, seq)
            y = _experts(tile_expert, n_used, xs, w_exp_gu[i].astype(BF16), w_exp_down[i].astype(BF16))
            x2 = _combine(meta, rank, comb, x2, g2, lng[1], lnb[1], y, seq)
    return x2.reshape(bsz, seq, d)
```

```python
import functools
import math

import numpy as np

import jax
import jax.numpy as jnp
from jax import lax
from jax.experimental import pallas as pl
from jax.experimental.pallas import tpu as pltpu

F32 = jnp.float32
BF16 = jnp.bfloat16

D_MODEL = 1024
DEPTH = 4
HEAD_DIM = 64
POOL_WINDOWS = (2, 4, 8, 16)
POOL_GROUP_DIM = 64
POOL_WIDTH = 256
DIL_CONFIGS = ((128, 1), (512, 4), (2048, 16))
DIL_HEADS = 4
DIL_GROUP_WIDTH = DIL_HEADS * HEAD_DIM
BAND_BLOCK = 128
BAND_BLOCKS_PER_STEP = 4
FOX_HEADS = 16
FOX_WIDTH = 1024
ROPE_THETA = 500000.0
ROPE_DIM = 16
D_FF = 2816
N_EXPERTS = 8
DEEPNORM_ALPHA = (2 * DEPTH) ** 0.25
LN_EPS = 1e-5
NEG = -1e30
LOG2E = 1.4426950408889634

LANES = 128
TOKEN_TILE = 512
FOX_Q_TILE = 512
FOX_K_TILE = 512
FOX_ROW_CHUNK = 64
MOE_CHUNK = 512
MOE_TILE = 512
ROW_BLOCK = 128
MAX_SPAN = MOE_CHUNK // ROW_BLOCK + 1
FF_SUB = 256
COMBINE_WIN = 2 * ROW_BLOCK
COMBINE_MAX_WIN = (ROW_BLOCK - 1 + MOE_CHUNK + COMBINE_WIN - 1) // COMBINE_WIN
NO_ROW = -1e9
VMEM_LIMIT = 56 * 1024 * 1024


def _params(sem, vmem=VMEM_LIMIT):
    return pltpu.CompilerParams(dimension_semantics=sem, vmem_limit_bytes=vmem)


def _modulate(x, sc, sh):
    return (x * (1.0 + sc) + sh).astype(BF16)


def _deepnorm_ln(x, sub, gate, g, b):
    y = DEEPNORM_ALPHA * x + gate * sub
    mu = jnp.mean(y, axis=-1, keepdims=True)
    yc = y - mu
    var = jnp.mean(yc * yc, axis=-1, keepdims=True)
    return yc * lax.rsqrt(var + LN_EPS) * g + b


def _ada_kernel(c_ref, w_ref, b_ref, o_ref):
    c = c_ref[...]
    ca = (c * jax.nn.sigmoid(c)).astype(BF16)
    o_ref[0] = jnp.dot(ca, w_ref[0].astype(BF16), preferred_element_type=F32) + b_ref[0]


def _ada(c, w_ada, b_ada):
    depth, d, n = w_ada.shape
    bsz = c.shape[0]
    tn = 1024
    return pl.pallas_call(
        _ada_kernel,
        out_shape=jax.ShapeDtypeStruct((depth, bsz, n), F32),
        grid=(depth, n // tn),
        in_specs=[
            pl.BlockSpec((bsz, d), lambda l, j: (0, 0)),
            pl.BlockSpec((1, d, tn), lambda l, j: (l, 0, j)),
            pl.BlockSpec((1, 1, tn), lambda l, j: (l, 0, j)),
        ],
        out_specs=pl.BlockSpec((1, bsz, tn), lambda l, j: (l, 0, j)),
        compiler_params=_params(("parallel", "parallel")),
        name="ada_mod",
    )(c, w_ada, b_ada.reshape(depth, 1, n))


def _even_in_kernel(x_ref, sc_ref, sh_ref, w_ref, ct_ref, sa_ref, sb_ref, u_ref, *rest):
    qkv_refs, z_ref = rest[:9], rest[9]
    tm = x_ref.shape[0]
    h = _modulate(x_ref[...], sc_ref[0], sh_ref[0])
    u_ref[...] = jnp.dot(h, w_ref[:, 0:POOL_WIDTH], preferred_element_type=F32)
    ct, sa, sb = ct_ref[...], sa_ref[...], sb_ref[...]
    for n in range(9):
        lo = POOL_WIDTH + n * DIL_GROUP_WIDTH
        z = jnp.dot(h, w_ref[:, lo:lo + DIL_GROUP_WIDTH], preferred_element_type=F32)
        if n < 6:
            z = z * ct + pltpu.roll(z, 8, 1) * sa + pltpu.roll(z, DIL_GROUP_WIDTH - 8, 1) * sb
        if n < 3:
            z = z * (HEAD_DIM ** -0.5)
        dil = DIL_CONFIGS[n % 3][1]
        if dil == 1:
            qkv_refs[n][0, 0] = z.astype(BF16)
        else:
            for half in range(2):
                z_ref[half] = z[:, half * LANES:(half + 1) * LANES]
            for r in range(dil):
                rows = [z_ref[half, pl.ds(r, tm // dil, stride=dil), :] for half in range(2)]
                qkv_refs[n][0, r] = jnp.concatenate(rows, axis=1).astype(BF16)


def _even_in(x2, sc, sh, w_bf, tabs, seq):
    t, d = x2.shape
    tm = TOKEN_TILE
    tps = seq // tm
    bsz = t // seq
    n_in = w_bf.shape[1]
    mod_spec = pl.BlockSpec((1, 1, d), lambda i: (i // tps, 0, 0))
    tab_spec = pl.BlockSpec((tm, DIL_GROUP_WIDTH), lambda i: (i % tps, 0))
    out_spec = pl.BlockSpec((tm, DIL_GROUP_WIDTH), lambda i: (i, 0))
    dils = [dil for _, dil in DIL_CONFIGS] * 3
    qkv_shapes = [jax.ShapeDtypeStruct((bsz, dil, seq // dil, DIL_GROUP_WIDTH), BF16) for dil in dils]
    qkv_specs = [pl.BlockSpec((1, dil, tm // dil, DIL_GROUP_WIDTH), lambda i: (i // tps, 0, i % tps, 0))
                 for dil in dils]
    return pl.pallas_call(
        _even_in_kernel,
        out_shape=[jax.ShapeDtypeStruct((t, POOL_WIDTH), F32)] + qkv_shapes,
        grid=(t // tm,),
        in_specs=[
            pl.BlockSpec((tm, d), lambda i: (i, 0)),
            mod_spec,
            mod_spec,
            pl.BlockSpec((d, n_in), lambda i: (0, 0)),
            tab_spec,
            tab_spec,
            tab_spec,
        ],
        out_specs=[out_spec] + qkv_specs,
        scratch_shapes=[pltpu.VMEM((DIL_GROUP_WIDTH // LANES, tm, LANES), F32)],
        compiler_params=_params(("parallel",)),
        name="even_in_proj",
    )(x2, sc, sh, w_bf, *tabs)


def _rope_tables(seq):
    pos = jnp.arange(seq, dtype=F32)
    inv = ROPE_THETA ** (-jnp.arange(0, ROPE_DIM, 2, dtype=F32) / ROPE_DIM)
    ang = pos[:, None] * inv[None, :]
    cos, sin = jnp.cos(ang), jnp.sin(ang)
    half = ROPE_DIM // 2
    zeros = jnp.zeros((seq, half), F32)
    rest = HEAD_DIM - ROPE_DIM
    ct = jnp.concatenate([cos, cos, jnp.ones((seq, rest), F32)], axis=1)
    sa = jnp.concatenate([zeros, sin, jnp.zeros((seq, rest), F32)], axis=1)
    sb = jnp.concatenate([-sin, zeros, jnp.zeros((seq, rest), F32)], axis=1)
    return tuple(jnp.tile(a, (1, DIL_HEADS)) for a in (ct, sa, sb))


def _band_kernel(q_ref, kp_ref, kc_ref, vp_ref, vc_ref, o_ref, lse_ref):
    mt = pl.program_id(2)
    blk = BAND_BLOCK
    nblk = q_ref.shape[2] // blk
    kall = jnp.concatenate([kp_ref[0, 0], kc_ref[0, 0]], axis=0)
    vall = jnp.concatenate([vp_ref[0, 0], vc_ref[0, 0]], axis=0)
    qi = lax.broadcasted_iota(jnp.int32, (blk, 2 * blk), 0)
    kj = lax.broadcasted_iota(jnp.int32, (blk, 2 * blk), 1)
    dist = qi + blk - kj
    in_band = (dist >= 0) & (dist <= blk)
    lane_q = lax.broadcasted_iota(jnp.int32, (blk, DIL_GROUP_WIDTH), 1) // HEAD_DIM
    lane_v = lax.broadcasted_iota(jnp.int32, (2 * blk, DIL_GROUP_WIDTH), 1) // HEAD_DIM
    for i in range(nblk):
        q = q_ref[0, 0, i * blk:(i + 1) * blk, :]
        kk = kall[i * blk:(i + 2) * blk]
        vv = vall[i * blk:(i + 2) * blk]
        if i == 0:
            valid = in_band & (kj >= jnp.where(mt > 0, 0, blk))
        else:
            valid = in_band
        acc = jnp.zeros((blk, DIL_GROUP_WIDTH), F32)
        den_full = jnp.ones((blk, DIL_GROUP_WIDTH), F32)
        lse_full = jnp.zeros((blk, DIL_GROUP_WIDTH), F32)
        for h in range(DIL_HEADS):
            qh = jnp.where(lane_q == h, q, jnp.zeros_like(q))
            s = lax.dot_general(qh, kk, (((1,), (1,)), ((), ())), preferred_element_type=F32)
            s = jnp.where(valid, s, NEG)
            m = jnp.max(s, axis=-1, keepdims=True)
            p = jnp.exp(s - m)
            den = jnp.sum(p, axis=-1, keepdims=True)
            vh = jnp.where(lane_v == h, vv, jnp.zeros_like(vv))
            acc = acc + jnp.dot(p.astype(BF16), vh, preferred_element_type=F32)
            den_full = jnp.where(lane_q == h, den, den_full)
            lse_full = jnp.where(lane_q == h, m + jnp.log(den), lse_full)
        o_ref[0, 0, i * blk:(i + 1) * blk, :] = acc / den_full
        lse_ref[0, 0, i * blk:(i + 1) * blk, :] = lse_full


def _band_attention(q, k, v):
    bsz, dil, ln, w = q.shape
    nblk = min(BAND_BLOCKS_PER_STEP, ln // BAND_BLOCK)
    rows = nblk * BAND_BLOCK
    cur = pl.BlockSpec((1, 1, rows, w), lambda b, r, m: (b, r, m, 0))
    prev = pl.BlockSpec((1, 1, BAND_BLOCK, w), lambda b, r, m: (b, r, jnp.maximum(m * nblk - 1, 0), 0))
    return pl.pallas_call(
        _band_kernel,
        out_shape=[jax.ShapeDtypeStruct(q.shape, F32)] * 2,
        grid=(bsz, dil, ln // rows),
        in_specs=[cur, prev, cur, prev, cur],
        out_specs=[cur, cur],
        compiler_params=_params(("parallel", "parallel", "parallel")),
        name=f"band_attn_d{dil}",
    )(q, k, k, v, v)


def _even_out_kernel(tps, u_ref, up_ref, o0, o1, o2, l0, l1, l2, wp_ref, ps_ref, wo_ref, x_ref, gate_ref,
                     g_ref, b_ref, out_ref, il_ref):
    i = pl.program_id(0)
    tm = u_ref.shape[0]
    halo = up_ref.shape[0]
    u = u_ref[...]
    prev = jnp.where(i % tps == 0, jnp.zeros_like(up_ref[...]), up_ref[...])
    ext = jnp.concatenate([prev, u], axis=0)
    lane_g = lax.broadcasted_iota(jnp.int32, (tm, POOL_WIDTH), 1) // POOL_GROUP_DIM
    row = lax.broadcasted_iota(jnp.int32, (tm, POOL_WIDTH), 0)
    pos1 = ((i % tps) * tm + row + 1).astype(F32)
    s = ext
    pooled = jnp.zeros((tm, POOL_WIDTH), F32)
    wvec = jnp.zeros((tm, POOL_WIDTH), F32)
    shift = 1
    for g, w in enumerate(POOL_WINDOWS):
        while shift < w:
            s = s + pltpu.roll(s, shift, 0)
            shift *= 2
        pooled = jnp.where(lane_g == g, s[halo:, :], pooled)
        wvec = jnp.where(lane_g == g, float(w), wvec)
    pooled = pooled / jnp.minimum(pos1, wvec) - u
    a_out = jnp.dot(pooled.astype(BF16), wp_ref[...], preferred_element_type=F32) * ps_ref[...]
    def token_order(ref, buf):
        dil = ref.shape[1]
        if dil == 1:
            return ref[0, 0]
        for r in range(dil):
            blk = ref[0, r]
            for half in range(2):
                buf[half, pl.ds(r, tm // dil, stride=dil), :] = blk[:, half * LANES:(half + 1) * LANES]
        return jnp.concatenate([buf[0], buf[1]], axis=1)

    la, lb, lc = [token_order(ref, il_ref.at[n]) for n, ref in enumerate((l0, l1, l2))]
    oa, ob, oc = [token_order(ref, il_ref.at[3 + n]) for n, ref in enumerate((o0, o1, o2))]
    mx = jnp.maximum(jnp.maximum(la, lb), lc)
    ea, eb, ec = jnp.exp(la - mx), jnp.exp(lb - mx), jnp.exp(lc - mx)
    b_out = (ea * oa + eb * ob + ec * oc) / (ea + eb + ec)
    mixed = jnp.concatenate([a_out, b_out], axis=-1).astype(BF16)
    sub = jnp.dot(mixed, wo_ref[...], preferred_element_type=F32)
    out_ref[...] = _deepnorm_ln(x_ref[...], sub, gate_ref[0], g_ref[...], b_ref[...])


def _even_out(u, os_, lses, wpool_bd, pool_scale, wo_bf, x2, gate, ln_g, ln_b, seq):
    t, d = x2.shape
    tm = TOKEN_TILE
    tps = seq // tm
    halo = 16
    row = lambda w: pl.BlockSpec((tm, w), lambda i: (i, 0))
    full = lambda a: pl.BlockSpec(a.shape, lambda i: (0,) * a.ndim)
    strided = [pl.BlockSpec((1, a.shape[1], tm // a.shape[1], DIL_GROUP_WIDTH), lambda i: (i // tps, 0, i % tps, 0))
               for a in (*os_, *lses)]
    return pl.pallas_call(
        functools.partial(_even_out_kernel, tps),
        out_shape=jax.ShapeDtypeStruct((t, d), F32),
        grid=(t // tm,),
        in_specs=[
            row(POOL_WIDTH),
            pl.BlockSpec((halo, POOL_WIDTH), lambda i: (jnp.maximum(i * (tm // halo) - 1, 0), 0)),
            *strided,
            full(wpool_bd),
            full(pool_scale),
            full(wo_bf),
            row(d),
            pl.BlockSpec((1, 1, d), lambda i: (i // tps, 0, 0)),
            full(ln_g),
            full(ln_b),
        ],
        out_specs=row(d),
        scratch_shapes=[pltpu.VMEM((6, DIL_GROUP_WIDTH // LANES, tm, LANES), F32)],
        compiler_params=_params(("parallel",)),
        name="even_out_proj",
    )(u, u, *os_, *lses, wpool_bd, pool_scale, wo_bf, x2, gate, ln_g, ln_b)


def _swiglu_rows(x, wgu_ref, wd_ref, acc_ref):
    for c in range(D_FF // FF_SUB):
        lo = c * FF_SUB
        gt = jnp.dot(x, wgu_ref[0, :, lo:lo + FF_SUB], preferred_element_type=F32)
        up = jnp.dot(x, wgu_ref[0, :, D_FF + lo:D_FF + lo + FF_SUB], preferred_element_type=F32)
        a = (gt * jax.nn.sigmoid(gt) * up).astype(BF16)
        part = jnp.dot(a, wd_ref[0, lo:lo + FF_SUB, :], preferred_element_type=F32)
        if c == 0:
            acc_ref[...] = part
        else:
            acc_ref[...] += part


def _ffn_kernel(x_ref, sc_ref, sh_ref, wgu_ref, wd_ref, gate_ref, g_ref, b_ref, out_ref, acc_ref):
    _swiglu_rows(_modulate(x_ref[...], sc_ref[0], sh_ref[0]), wgu_ref, wd_ref, acc_ref)
    out_ref[...] = _deepnorm_ln(x_ref[...], acc_ref[...], gate_ref[0], g_ref[...], b_ref[...])


def _ffn(x2, sc, sh, wgu_bf, wd_bf, gate, ln_g, ln_b, seq):
    t, d = x2.shape
    tm = TOKEN_TILE
    tps = seq // tm
    mod = pl.BlockSpec((1, 1, d), lambda i: (i // tps, 0, 0))
    row = pl.BlockSpec((tm, d), lambda i: (i, 0))
    vec = pl.BlockSpec((1, d), lambda i: (0, 0))
    full = lambda a: pl.BlockSpec(a.shape, lambda i: (0,) * a.ndim)
    return pl.pallas_call(
        _ffn_kernel,
        out_shape=jax.ShapeDtypeStruct((t, d), F32),
        grid=(t // tm,),
        in_specs=[row, mod, mod, full(wgu_bf), full(wd_bf), mod, vec, vec],
        out_specs=row,
        scratch_shapes=[pltpu.VMEM((tm, d), F32)],
        compiler_params=_params(("parallel",)),
        name="ffn_dense",
    )(x2, sc, sh, wgu_bf, wd_bf, gate, ln_g, ln_b)


def _split3(x):
    hi = x.astype(BF16)
    r = x - hi.astype(F32)
    mid = r.astype(BF16)
    lo = (r - mid.astype(F32)).astype(BF16)
    return hi, mid, lo


def _odd_in_kernel(tps, x_ref, sc_ref, sh_ref, w_ref, wf_ref, bf_ref, place_ref, q_ref, k_ref, v_ref, f_ref,
                   carry_ref):
    i = pl.program_id(0)
    tm = x_ref.shape[0]
    h = _modulate(x_ref[...], sc_ref[0], sh_ref[0])
    q = jnp.dot(h, w_ref[:, 0:FOX_WIDTH], preferred_element_type=F32)
    q_ref[...] = (q * (HEAD_DIM ** -0.5 * LOG2E)).astype(BF16)
    k_ref[...] = jnp.dot(h, w_ref[:, FOX_WIDTH:2 * FOX_WIDTH], preferred_element_type=F32).astype(BF16)
    v_ref[...] = jnp.dot(h, w_ref[:, 2 * FOX_WIDTH:3 * FOX_WIDTH], preferred_element_type=F32).astype(BF16)
    f = jnp.dot(h, wf_ref[...], preferred_element_type=F32) + bf_ref[...]
    logf = jnp.minimum(f, 0.0) - jnp.log(1.0 + jnp.exp(-jnp.abs(f)))
    r = lax.broadcasted_iota(jnp.int32, (tm, tm), 0)
    cidx = lax.broadcasted_iota(jnp.int32, (tm, tm), 1)
    tri = jnp.where(cidx <= r, 1.0, 0.0).astype(BF16)
    hi, mid, lo = _split3(logf)
    cs = (jnp.dot(tri, lo, preferred_element_type=F32) + jnp.dot(tri, mid, preferred_element_type=F32)
          + jnp.dot(tri, hi, preferred_element_type=F32))

    @pl.when(i % tps == 0)
    def _():
        carry_ref[...] = jnp.zeros_like(carry_ref)

    cs = cs + carry_ref[...]
    carry_ref[...] = cs[tm - 1:tm, :]
    parts = jnp.concatenate(_split3(cs * LOG2E), axis=1)
    f_ref[...] = jnp.dot(parts, place_ref[...], preferred_element_type=F32).astype(BF16)


def _gate_placement():
    place = np.zeros((3 * LANES, FOX_WIDTH), np.float32)
    for h in range(FOX_HEADS):
        base = (h // 2) * LANES + (HEAD_DIM if h % 2 == 0 else 0)
        for p in range(3):
            place[p * LANES + h, base + p] = 1.0
    return jnp.asarray(place, BF16)


def _odd_in(x2, sc, sh, w_bf, wf_bf, bf_pad, seq):
    t, d = x2.shape
    tm = TOKEN_TILE
    tps = seq // tm
    mod = pl.BlockSpec((1, 1, d), lambda i: (i // tps, 0, 0))
    row = lambda w: pl.BlockSpec((tm, w), lambda i: (i, 0))
    full = lambda a: pl.BlockSpec(a.shape, lambda i: (0,) * a.ndim)
    place = _gate_placement()
    return pl.pallas_call(
        functools.partial(_odd_in_kernel, tps),
        out_shape=[jax.ShapeDtypeStruct((t, FOX_WIDTH), BF16)] * 4,
        grid=(t // tm,),
        in_specs=[row(d), mod, mod, full(w_bf), full(wf_bf), full(bf_pad), full(place)],
        out_specs=[row(FOX_WIDTH)] * 4,
        scratch_shapes=[pltpu.VMEM((1, LANES), F32)],
        compiler_params=_params(("arbitrary",)),
        name="odd_in_proj",
    )(x2, sc, sh, w_bf, wf_bf, bf_pad, place)


def _fox_kernel(q_ref, k_ref, v_ref, f_ref, o_ref, acc_ref, m_ref, s0_ref, s1_ref, p0_ref, p1_ref):
    qi = pl.program_id(2)
    tq = q_ref.shape[1]
    tk = FOX_K_TILE
    half = HEAD_DIM
    q = q_ref[0]
    lane_q = lax.broadcasted_iota(jnp.int32, q.shape, 1)
    lane_k = lax.broadcasted_iota(jnp.int32, (tk, LANES), 1)
    qa = jnp.where(lane_q < half, q, jnp.where(lane_q < half + 3, -1.0, 0.0).astype(BF16))
    qb = jnp.where(lane_q >= half, q, jnp.where(lane_q < 3, -1.0, 0.0).astype(BF16))
    ones_a = jnp.where(lane_k == half, 1.0, 0.0).astype(BF16)
    ones_b = jnp.where(lane_k == 0, 1.0, 0.0).astype(BF16)
    acc_ref[...] = jnp.zeros_like(acc_ref)
    m_ref[...] = jnp.full(m_ref.shape, NEG, F32)

    def scores(j, s_ref):
        start = pl.multiple_of(j * tk, tk)
        k = k_ref[0, pl.ds(start, tk), :]
        fa = f_ref[0, pl.ds(start, tk), :]
        ka = jnp.where(lane_k < half, k, fa)
        kb = jnp.where(lane_k >= half, k, fa)
        for hh, (qh, kh) in enumerate(((qa, ka), (qb, kb))):
            s_ref[hh] = lax.dot_general(qh, kh, (((1,), (1,)), ((), ())), preferred_element_type=F32)

    def attend(j, s_ref, p_ref, masked):
        start = pl.multiple_of(j * tk, tk)
        v = v_ref[0, pl.ds(start, tk), :]
        va = jnp.where(lane_k < half, v, ones_a)
        vb = jnp.where(lane_k >= half, v, ones_b)
        for hh, vh in enumerate((va, vb)):
            for r0 in range(0, tq, FOX_ROW_CHUNK):
                rows = pl.ds(r0, FOX_ROW_CHUNK)
                s = s_ref[hh, rows, :]
                if masked:
                    rr = lax.broadcasted_iota(jnp.int32, s.shape, 0) + (qi * tq + r0)
                    cc = lax.broadcasted_iota(jnp.int32, s.shape, 1) + start
                    s = jnp.where(cc <= rr, s, NEG)
                m_old = m_ref[hh, rows, :]
                m_new = jnp.maximum(m_old, jnp.max(s, axis=-1, keepdims=True))
                p_ref[hh, rows, :] = jnp.exp2(s - jnp.tile(m_new, (1, tk // LANES))).astype(BF16)
                acc_ref[hh, rows, :] = acc_ref[hh, rows, :] * jnp.exp2(m_old - m_new)
                m_ref[hh, rows, :] = m_new
            acc_ref[hh] += jnp.dot(p_ref[hh], vh, preferred_element_type=F32)

    scores(0, s0_ref)

    def body(jp, carry):
        scores(2 * jp + 1, s1_ref)
        attend(2 * jp, s0_ref, p0_ref, False)
        scores(2 * jp + 2, s0_ref)
        attend(2 * jp + 1, s1_ref, p1_ref, False)
        return carry

    lax.fori_loop(0, qi // 2, body, 0)

    @pl.when(qi % 2 == 0)
    def _():
        attend(qi, s0_ref, p0_ref, True)

    @pl.when(qi % 2 == 1)
    def _():
        scores(qi, s1_ref)
        attend(qi - 1, s0_ref, p0_ref, False)
        attend(qi, s1_ref, p1_ref, True)

    acc_a, acc_b = acc_ref[0], acc_ref[1]
    lane_o = lax.broadcasted_iota(jnp.int32, (tq, LANES), 1)
    o = jnp.where(lane_o < half, acc_a / acc_a[:, half:half + 1], acc_b / acc_b[:, 0:1])
    o_ref[0] = o.astype(BF16)


def _fox_attention(q, k, v, fa, bsz, seq):
    tq = FOX_Q_TILE
    npairs = FOX_HEADS // 2
    tile = pl.BlockSpec((1, tq, LANES), lambda b, j, i: (b, i, j))
    whole = pl.BlockSpec((1, seq, LANES), lambda b, j, i: (b, 0, j))
    return pl.pallas_call(
        _fox_kernel,
        out_shape=jax.ShapeDtypeStruct((bsz, seq, FOX_WIDTH), BF16),
        grid=(bsz, npairs, seq // tq),
        in_specs=[tile, whole, whole, whole],
        out_specs=tile,
        scratch_shapes=[pltpu.VMEM((2, tq, LANES), F32), pltpu.VMEM((2, tq, LANES), F32),
                        pltpu.VMEM((2, tq, FOX_K_TILE), F32), pltpu.VMEM((2, tq, FOX_K_TILE), F32),
                        pltpu.VMEM((2, tq, FOX_K_TILE), BF16), pltpu.VMEM((2, tq, FOX_K_TILE), BF16)],
        compiler_params=_params(("parallel", "parallel", "arbitrary")),
        name="fox_attn",
    )(q, k, v, fa)


def _odd_out_kernel(o_ref, wo_ref, x_ref, gate_ref, g_ref, b_ref, out_ref):
    sub = jnp.dot(o_ref[...], wo_ref[...], preferred_element_type=F32)
    out_ref[...] = _deepnorm_ln(x_ref[...], sub, gate_ref[0], g_ref[...], b_ref[...])


def _odd_out(o2, wo_bf, x2, gate, ln_g, ln_b, seq):
    t, d = x2.shape
    tm = TOKEN_TILE
    tps = seq // tm
    row = lambda w: pl.BlockSpec((tm, w), lambda i: (i, 0))
    full = lambda a: pl.BlockSpec(a.shape, lambda i: (0,) * a.ndim)
    return pl.pallas_call(
        _odd_out_kernel,
        out_shape=jax.ShapeDtypeStruct((t, d), F32),
        grid=(t // tm,),
        in_specs=[row(FOX_WIDTH), full(wo_bf), row(d), pl.BlockSpec((1, 1, d), lambda i: (i // tps, 0, 0)),
                  full(ln_g), full(ln_b)],
        out_specs=row(d),
        compiler_params=_params(("parallel",)),
        name="odd_out_proj",
    )(o2, wo_bf, x2, gate, ln_g, ln_b)


def _router_kernel(x_ref, sc_ref, sh_ref, wrt_ref, tri_ref, rankt_ref, cnt_ref, comb_ref, rank_ref):
    c = x_ref.shape[0]
    h = _modulate(x_ref[...], sc_ref[0], sh_ref[0])
    lt = lax.dot_general(wrt_ref[...], h, (((1,), (1,)), ((), ())), preferred_element_type=F32)[0:N_EXPERTS]
    row = lax.broadcasted_iota(jnp.int32, lt.shape, 0)
    m1 = jnp.max(lt, axis=0, keepdims=True)
    i1 = jnp.min(jnp.where(lt == m1, row, N_EXPERTS), axis=0, keepdims=True)
    rest = jnp.where(row == i1, -jnp.inf, lt)
    m2 = jnp.max(rest, axis=0, keepdims=True)
    i2 = jnp.min(jnp.where(rest == m2, row, N_EXPERTS), axis=0, keepdims=True)
    e2 = jnp.exp(m2 - m1)
    g1 = 1.0 / (1.0 + e2)
    g2 = e2 / (1.0 + e2)
    combt = jnp.where(row == i1, g1, jnp.where(row == i2, g2, 0.0))
    sel = jnp.where((row == i1) | (row == i2), 1.0, 0.0)
    sel16 = jnp.concatenate([sel, jnp.zeros_like(sel)], axis=0).astype(BF16)
    earlier = jnp.dot(sel16, tri_ref[...], preferred_element_type=F32)[0:N_EXPERTS]
    rankt = jnp.where(sel > 0.0, earlier, NO_ROW)
    rankt_ref[...] = rankt
    cnt_ref[0] = jnp.broadcast_to(jnp.sum(sel, axis=1, keepdims=True), (N_EXPERTS, LANES))
    pad = LANES - N_EXPERTS
    comb_ref[...] = jnp.concatenate([combt, jnp.zeros((pad, c), F32)], axis=0).T
    rank_ref[...] = jnp.concatenate([rankt, jnp.full((pad, c), NO_ROW, F32)], axis=0).T


def _router(x2, sc, sh, wrt, seq):
    t, d = x2.shape
    cs = MOE_CHUNK
    cps = seq // cs
    mod = pl.BlockSpec((1, 1, d), lambda i: (i // cps, 0, 0))
    tri = jnp.asarray(np.triu(np.ones((cs, cs), np.float32), 1), BF16)
    return pl.pallas_call(
        _router_kernel,
        out_shape=[jax.ShapeDtypeStruct((N_EXPERTS, t), F32),
                   jax.ShapeDtypeStruct((t // cs, N_EXPERTS, LANES), F32),
                   jax.ShapeDtypeStruct((t, LANES), F32),
                   jax.ShapeDtypeStruct((t, LANES), F32)],
        grid=(t // cs,),
        in_specs=[pl.BlockSpec((cs, d), lambda i: (i, 0)), mod, mod,
                  pl.BlockSpec(wrt.shape, lambda i: (0, 0)), pl.BlockSpec(tri.shape, lambda i: (0, 0))],
        out_specs=[pl.BlockSpec((N_EXPERTS, cs), lambda i: (0, i)),
                   pl.BlockSpec((1, N_EXPERTS, LANES), lambda i: (i, 0, 0)),
                   pl.BlockSpec((cs, LANES), lambda i: (i, 0)),
                   pl.BlockSpec((cs, LANES), lambda i: (i, 0))],
        compiler_params=_params(("parallel",)),
        name="router",
    )(x2, sc, sh, wrt, tri)


def _route_meta(cnt, n_tiles):
    before = jnp.cumsum(cnt, axis=0) - cnt
    total = jnp.sum(cnt, axis=0)
    region = ((total + MOE_TILE - 1) // MOE_TILE) * MOE_TILE
    base = jnp.cumsum(region) - region
    n_used = (jnp.sum(region) // MOE_TILE).reshape(1)
    starts = jnp.arange(n_tiles, dtype=jnp.int32) * MOE_TILE
    tile_expert = jnp.minimum(jnp.sum(starts[:, None] >= (base + region)[None, :], axis=1), N_EXPERTS - 1)
    i32 = lambda a: a.astype(jnp.int32)
    return i32(before.reshape(-1)), i32(cnt.reshape(-1)), i32(base), i32(total), i32(tile_expert), i32(n_used)


def _chunk_block(before_ref, cnt_ref, chunk, e, j):
    g = before_ref[chunk * N_EXPERTS + e]
    n = cnt_ref[chunk * N_EXPERTS + e]
    k = g // ROW_BLOCK + j
    return g, n, k, (n > 0) & (k * ROW_BLOCK < g + n)


def _dispatch_kernel(before_ref, cnt_ref, base_ref, total_ref, x_ref, sc_ref, sh_ref, rankt_ref, xs_ref,
                     stage_ref, slot_ref, zero_ref, sem, fsem, zsem, usem):
    c = pl.program_id(0)
    last = c == pl.num_programs(0) - 1
    h = _modulate(x_ref[...], sc_ref[0], sh_ref[0])

    @pl.when(c == 0)
    def _():
        stage_ref[...] = jnp.zeros_like(stage_ref)
        zero_ref[...] = jnp.zeros_like(zero_ref)

    block_row = lax.broadcasted_iota(jnp.int32, (ROW_BLOCK, MOE_CHUNK), 0).astype(F32)
    pair_row = lax.broadcasted_iota(jnp.int32, (2 * ROW_BLOCK, MOE_CHUNK), 0).astype(F32)

    def block_copy(e, j, k):
        row0 = pl.multiple_of(base_ref[e] + k * ROW_BLOCK, ROW_BLOCK)
        return pltpu.make_async_copy(slot_ref.at[e, j], xs_ref.at[pl.ds(row0, ROW_BLOCK)], sem.at[e, j])

    def any_beyond_two(chunk):
        flag = _chunk_block(before_ref, cnt_ref, chunk, 0, 2)[3]
        for e in range(1, N_EXPERTS):
            flag = flag | _chunk_block(before_ref, cnt_ref, chunk, e, 2)[3]
        return flag

    def for_blocks(chunk, fn):
        for e in range(N_EXPERTS):
            for j in range(2):
                fn(e, j)

        @pl.when(any_beyond_two(chunk))
        def _():
            for e in range(N_EXPERTS):
                for j in range(2, MAX_SPAN):
                    fn(e, j)

    def wait_block_copies(chunk):
        def wait_one(e, j):
            g, n, k, touched = _chunk_block(before_ref, cnt_ref, chunk, e, j)

            @pl.when(touched & (g + n >= (k + 1) * ROW_BLOCK))
            def _():
                block_copy(e, j, k).wait()

        for_blocks(chunk, wait_one)

    selectors = []
    for e in range(N_EXPERTS):
        g = before_ref[c * N_EXPERTS + e]
        off = (g % ROW_BLOCK).astype(F32)
        selectors.append(jnp.where(rankt_ref[e:e + 1, :] + off == pair_row, 1.0, 0.0).astype(BF16))
    first_two = jnp.dot(jnp.concatenate(selectors, axis=0), h, preferred_element_type=F32)

    @pl.when(c > 0)
    def _():
        wait_block_copies(c - 1)

    def emit_block(e, j):
        g, n, k, touched = _chunk_block(before_ref, cnt_ref, c, e, j)
        complete = g + n >= (k + 1) * ROW_BLOCK

        @pl.when(touched)
        def _():
            if j < 2:
                r0 = (2 * e + j) * ROW_BLOCK
                blk = first_two[r0:r0 + ROW_BLOCK]
            else:
                off = (g - k * ROW_BLOCK).astype(F32)
                onehot = jnp.where(rankt_ref[e:e + 1, :] + off == block_row, 1.0, 0.0).astype(BF16)
                blk = jnp.dot(onehot, h, preferred_element_type=F32)
            if j == 0:
                blk = blk + stage_ref[e].astype(F32)
            blk = blk.astype(BF16)
            slot_ref[e, j] = blk
            stage_ref[e] = jnp.where(complete, jnp.zeros_like(blk), blk)

            @pl.when(complete)
            def _():
                block_copy(e, j, k).start()

    for_blocks(c, emit_block)

    def tail_copies(e):
        total = total_ref[e]
        n_data = (total + ROW_BLOCK - 1) // ROW_BLOCK
        n_region = ((total + MOE_TILE - 1) // MOE_TILE) * (MOE_TILE // ROW_BLOCK)
        part_row = pl.multiple_of(base_ref[e] + (total // ROW_BLOCK) * ROW_BLOCK, ROW_BLOCK)
        part = pltpu.make_async_copy(stage_ref.at[e], xs_ref.at[pl.ds(part_row, ROW_BLOCK)], fsem.at[e])
        out = [(total % ROW_BLOCK != 0, part)]
        for z in range(MOE_TILE // ROW_BLOCK - 1):
            zrow = pl.multiple_of(base_ref[e] + (n_data + z) * ROW_BLOCK, ROW_BLOCK)
            zc = pltpu.make_async_copy(zero_ref, xs_ref.at[pl.ds(zrow, ROW_BLOCK)], zsem.at[e, z])
            out.append((n_data + z < n_region, zc))
        return out

    def unused_copies():
        e = N_EXPERTS - 1
        used = base_ref[e] + ((total_ref[e] + MOE_TILE - 1) // MOE_TILE) * MOE_TILE
        out = []
        for z in range(usem.shape[0]):
            zrow = pl.multiple_of(used + z * ROW_BLOCK, ROW_BLOCK)
            zc = pltpu.make_async_copy(zero_ref, xs_ref.at[pl.ds(zrow, ROW_BLOCK)], usem.at[z])
            out.append((zrow < xs_ref.shape[0], zc))
        return out

    def all_tail_copies():
        return [cc for e in range(N_EXPERTS) for cc in tail_copies(e)] + unused_copies()

    @pl.when(last)
    def _():
        for cond, cp in all_tail_copies():
            @pl.when(cond)
            def _():
                cp.start()

    @pl.when(last)
    def _():
        wait_block_copies(c)

    @pl.when(last)
    def _():
        for cond, cp in all_tail_copies():
            @pl.when(cond)
            def _():
                cp.wait()


def _dispatch(meta, x2, sc, sh, rankt, n_tiles, seq):
    before, cnt, base, total = meta
    t, d = x2.shape
    cs = MOE_CHUNK
    cps = seq // cs
    mod = pl.BlockSpec((1, 1, d), lambda i, *_: (i // cps, 0, 0))
    grid_spec = pltpu.PrefetchScalarGridSpec(
        num_scalar_prefetch=4,
        grid=(t // cs,),
        in_specs=[pl.BlockSpec((cs, d), lambda i, *_: (i, 0)), mod, mod,
                  pl.BlockSpec((N_EXPERTS, cs), lambda i, *_: (0, i))],
        out_specs=pl.BlockSpec(memory_space=pl.ANY),
        scratch_shapes=[pltpu.VMEM((N_EXPERTS, ROW_BLOCK, d), BF16),
                        pltpu.VMEM((N_EXPERTS, MAX_SPAN, ROW_BLOCK, d), BF16),
                        pltpu.VMEM((ROW_BLOCK, d), BF16),
                        pltpu.SemaphoreType.DMA((N_EXPERTS, MAX_SPAN)),
                        pltpu.SemaphoreType.DMA((N_EXPERTS,)),
                        pltpu.SemaphoreType.DMA((N_EXPERTS, MOE_TILE // ROW_BLOCK - 1)),
                        pltpu.SemaphoreType.DMA(((n_tiles * MOE_TILE - 2 * t) // ROW_BLOCK,))],
    )
    return pl.pallas_call(
        _dispatch_kernel,
        out_shape=jax.ShapeDtypeStruct((n_tiles * MOE_TILE, d), BF16),
        grid_spec=grid_spec,
        compiler_params=_params(("arbitrary",)),
        name="moe_dispatch",
    )(before, cnt, base, total, x2, sc, sh, rankt)


def _experts_kernel(te_ref, nu_ref, x_ref, wgu_ref, wd_ref, y_ref, acc_ref):
    @pl.when(pl.program_id(0) >= nu_ref[0])
    def _():
        y_ref[...] = jnp.zeros_like(y_ref)

    @pl.when(pl.program_id(0) < nu_ref[0])
    def _():
        _swiglu_rows(x_ref[...], wgu_ref, wd_ref, acc_ref)
        y_ref[...] = acc_ref[...].astype(BF16)


def _experts(tile_expert, n_used, xs, wgu_bf, wd_bf):
    rows, d = xs.shape
    tm = MOE_TILE
    clamp = lambda i, nu: jnp.minimum(i, nu[0] - 1)
    grid_spec = pltpu.PrefetchScalarGridSpec(
        num_scalar_prefetch=2,
        grid=(rows // tm,),
        in_specs=[pl.BlockSpec((tm, d), lambda i, te, nu: (clamp(i, nu), 0)),
                  pl.BlockSpec((1, d, 2 * D_FF), lambda i, te, nu: (te[clamp(i, nu)], 0, 0)),
                  pl.BlockSpec((1, D_FF, d), lambda i, te, nu: (te[clamp(i, nu)], 0, 0))],
        out_specs=pl.BlockSpec((tm, d), lambda i, te, nu: (i, 0)),
        scratch_shapes=[pltpu.VMEM((tm, d), F32)],
    )
    return pl.pallas_call(
        _experts_kernel,
        out_shape=jax.ShapeDtypeStruct((rows, d), BF16),
        grid_spec=grid_spec,
        compiler_params=_params(("arbitrary",)),
        name="moe_experts",
    )(tile_expert, n_used, xs, wgu_bf, wd_bf)


def _combine_kernel(before_ref, cnt_ref, base_ref, rank_ref, comb_ref, x_ref, gate_ref, g_ref, b_ref, y_ref,
                    out_ref, ybuf, xbuf, acc_ref, sem, xsem):
    c = pl.program_id(0)
    slot = c % 2

    def window(chunk, e, w):
        g = before_ref[chunk * N_EXPERTS + e]
        n = cnt_ref[chunk * N_EXPERTS + e]
        first = (g // ROW_BLOCK) * ROW_BLOCK + w * COMBINE_WIN
        row0 = pl.multiple_of(jnp.minimum(base_ref[e] + first, y_ref.shape[0] - COMBINE_WIN), ROW_BLOCK)
        off = (base_ref[e] + g - row0).astype(F32)
        return (n > 0) & (first < g + n), row0, off

    def first_copy(chunk, sl, e):
        touched, row0, _ = window(chunk, e, 0)
        return touched, pltpu.make_async_copy(y_ref.at[pl.ds(row0, COMBINE_WIN)], ybuf.at[sl, e], sem.at[sl, e])

    def fetch(chunk, sl):
        for e in range(N_EXPERTS):
            touched, cp = first_copy(chunk, sl, e)

            @pl.when(touched)
            def _():
                cp.start()

    @pl.when(c == 0)
    def _():
        ybuf[...] = jnp.zeros_like(ybuf)
        fetch(0, 0)

    @pl.when(c + 1 < pl.num_programs(0))
    def _():
        fetch(c + 1, 1 - slot)

    lane = lax.broadcasted_iota(jnp.int32, (MOE_CHUNK, COMBINE_WIN), 1).astype(F32)
    acc_ref[...] = jnp.zeros_like(acc_ref)
    for e in range(N_EXPERTS):
        col = rank_ref[:, e:e + 1]
        weight = comb_ref[:, e:e + 1]
        touched, _, off = window(c, e, 0)

        @pl.when(touched)
        def _():
            first_copy(c, slot, e)[1].wait()

        onehot = jnp.where(col + off == lane, 1.0, 0.0).astype(BF16)
        acc_ref[...] += weight * jnp.dot(onehot, ybuf[slot, e], preferred_element_type=F32)

    any_more = window(c, 0, 1)[0]
    for e in range(1, N_EXPERTS):
        any_more = any_more | window(c, e, 1)[0]

    @pl.when(any_more)
    def _():
        for e in range(N_EXPERTS):
            for w in range(1, COMBINE_MAX_WIN):
                touched, row0, off = window(c, e, w)

                @pl.when(touched)
                def _():
                    cp = pltpu.make_async_copy(y_ref.at[pl.ds(row0, COMBINE_WIN)], xbuf, xsem.at[0])
                    cp.start()
                    cp.wait()
                    onehot = jnp.where(rank_ref[:, e:e + 1] + off == lane, 1.0, 0.0).astype(BF16)
                    acc_ref[...] += comb_ref[:, e:e + 1] * jnp.dot(onehot, xbuf[...], preferred_element_type=F32)

    out_ref[...] = _deepnorm_ln(x_ref[...], acc_ref[...], gate_ref[0], g_ref[...], b_ref[...])


def _combine(meta, rank, comb, x2, gate, ln_g, ln_b, y, seq):
    before, cnt, base, _ = meta
    t, d = x2.shape
    cs = MOE_CHUNK
    cps = seq // cs
    row = lambda w: pl.BlockSpec((cs, w), lambda i, *_: (i, 0))
    vec = pl.BlockSpec((1, d), lambda i, *_: (0, 0))
    grid_spec = pltpu.PrefetchScalarGridSpec(
        num_scalar_prefetch=3,
        grid=(t // cs,),
        in_specs=[row(LANES), row(LANES), row(d), pl.BlockSpec((1, 1, d), lambda i, *_: (i // cps, 0, 0)),
                  vec, vec, pl.BlockSpec(memory_space=pl.ANY)],
        out_specs=row(d),
        scratch_shapes=[pltpu.VMEM((2, N_EXPERTS, COMBINE_WIN, d), BF16),
                        pltpu.VMEM((COMBINE_WIN, d), BF16),
                        pltpu.VMEM((cs, d), F32),
                        pltpu.SemaphoreType.DMA((2, N_EXPERTS)),
                        pltpu.SemaphoreType.DMA((1,))],
    )
    return pl.pallas_call(
        _combine_kernel,
        out_shape=jax.ShapeDtypeStruct((t, d), F32),
        grid_spec=grid_spec,
        compiler_params=_params(("arbitrary",)),
        name="moe_combine",
    )(before, cnt, base, rank, comb, x2, gate, ln_g, ln_b, y)


def _block_diag(w_pool):
    g, c, _ = w_pool.shape
    out = jnp.zeros((g * c, g * c), w_pool.dtype)
    for i in range(g):
        out = out.at[i * c:(i + 1) * c, i * c:(i + 1) * c].set(w_pool[i])
    return out


def kernel(x, c, w_ada, b_ada, ln_g, ln_b, w_in_even, w_pool, pool_scale, w_out_even, w_ffn_gu, w_ffn_down,
           w_in_odd, b_forget, w_out_odd, w_router, w_exp_gu, w_exp_down):
    bsz, seq, d = x.shape
    t = bsz * seq
    tabs = _rope_tables(seq)
    ada = _ada(c, w_ada, b_ada)
    x2 = x.reshape(t, d)
    for l in range(DEPTH):
        i = l // 2
        sh1, sc1, g1, sh2, sc2, g2 = [ada[l, :, n * d:(n + 1) * d].reshape(bsz, 1, d) for n in range(6)]
        lng = ln_g[l].reshape(2, 1, d)
        lnb = ln_b[l].reshape(2, 1, d)
        if l % 2 == 0:
            outs = _even_in(x2, sc1, sh1, w_in_even[i].astype(BF16), tabs, seq)
            u, qkv = outs[0], outs[1:]
            os_, lses = [], []
            for g in range(len(DIL_CONFIGS)):
                o, lse = _band_attention(qkv[g], qkv[3 + g], qkv[6 + g])
                os_.append(o)
                lses.append(lse)
            x2 = _even_out(u, os_, lses, _block_diag(w_pool[i]).astype(BF16), pool_scale[i].reshape(1, -1),
                           w_out_even[i].astype(BF16), x2, g1, lng[0], lnb[0], seq)
            x2 = _ffn(x2, sc2, sh2, w_ffn_gu[i][None].astype(BF16), w_ffn_down[i][None].astype(BF16), g2, lng[1],
                      lnb[1], seq)
        else:
            w_in = w_in_odd[i]
            wf = jnp.zeros((d, LANES), F32).at[:, :FOX_HEADS].set(w_in[:, 3 * FOX_WIDTH:]).astype(BF16)
            bfp = jnp.zeros((1, LANES), F32).at[0, :FOX_HEADS].set(b_forget[i])
            q, k, v, fa = [a.reshape(bsz, seq, FOX_WIDTH) for a in
                           _odd_in(x2, sc1, sh1, w_in[:, :3 * FOX_WIDTH].astype(BF16), wf, bfp, seq)]
            o = _fox_attention(q, k, v, fa, bsz, seq)
            x2 = _odd_out(o.reshape(t, FOX_WIDTH), w_out_odd[i].astype(BF16), x2, g1, lng[0], lnb[0], seq)
            wrt = jnp.zeros((2 * N_EXPERTS, d), F32).at[:N_EXPERTS].set(w_router[i].T).astype(BF16)
            rankt, cnt, comb, rank = _router(x2, sc2, sh2, wrt, seq)
            n_tiles = (2 * t + N_EXPERTS * (MOE_TILE - 1) + MOE_TILE - 1) // MOE_TILE
            *meta, tile_expert, n_used = _route_meta(cnt[:, :, 0].astype(jnp.int32), n_tiles)
            xs = _dispatch(meta, x2, sc2, sh2, rankt, n_tiles, seq)
            y = _experts(tile_expert, n_used, xs, w_exp_gu[i].astype(BF16), w_exp_down[i].astype(BF16))
            x2 = _combine(meta, rank, comb, x2, g2, lng[1], lnb[1], y, seq)
    return x2.reshape(bsz, seq, d)
```

```python
import functools
import math

import numpy as np

import jax
import jax.numpy as jnp
from jax import lax
from jax.experimental import pallas as pl
from jax.experimental.pallas import tpu as pltpu

F32 = jnp.float32
BF16 = jnp.bfloat16

D_MODEL = 1024
DEPTH = 4
HEAD_DIM = 64
POOL_WINDOWS = (2, 4, 8, 16)
POOL_GROUP_DIM = 64
POOL_WIDTH = 256
DIL_CONFIGS = ((128, 1), (512, 4), (2048, 16))
DIL_HEADS = 4
DIL_GROUP_WIDTH = DIL_HEADS * HEAD_DIM
BAND_BLOCK = 128
BAND_BLOCKS_PER_STEP = 4
FOX_HEADS = 16
FOX_WIDTH = 1024
ROPE_THETA = 500000.0
ROPE_DIM = 16
D_FF = 2816
N_EXPERTS = 8
DEEPNORM_ALPHA = (2 * DEPTH) ** 0.25
LN_EPS = 1e-5
NEG = -1e30
LOG2E = 1.4426950408889634

LANES = 128
TOKEN_TILE = 512
FOX_Q_TILE = 512
FOX_K_TILE = 512
FOX_ROW_CHUNK = 64
MOE_CHUNK = 512
MOE_TILE = 512
ROW_BLOCK = 128
MAX_SPAN = MOE_CHUNK // ROW_BLOCK + 1
FF_SUB = 256
COMBINE_WIN = 2 * ROW_BLOCK
COMBINE_MAX_WIN = (ROW_BLOCK - 1 + MOE_CHUNK + COMBINE_WIN - 1) // COMBINE_WIN
NO_ROW = -1e9
VMEM_LIMIT = 56 * 1024 * 1024


def _params(sem, vmem=VMEM_LIMIT):
    return pltpu.CompilerParams(dimension_semantics=sem, vmem_limit_bytes=vmem)


def _modulate(x, sc, sh):
    return (x * (1.0 + sc) + sh).astype(BF16)


def _deepnorm_ln(x, sub, gate, g, b):
    y = DEEPNORM_ALPHA * x + gate * sub
    mu = jnp.mean(y, axis=-1, keepdims=True)
    yc = y - mu
    var = jnp.mean(yc * yc, axis=-1, keepdims=True)
    return yc * lax.rsqrt(var + LN_EPS) * g + b


def _ada_kernel(c_ref, w_ref, b_ref, o_ref):
    c = c_ref[...]
    ca = (c * jax.nn.sigmoid(c)).astype(BF16)
    o_ref[0] = jnp.dot(ca, w_ref[0].astype(BF16), preferred_element_type=F32) + b_ref[0]


def _ada(c, w_ada, b_ada):
    depth, d, n = w_ada.shape
    bsz = c.shape[0]
    tn = 1024
    return pl.pallas_call(
        _ada_kernel,
        out_shape=jax.ShapeDtypeStruct((depth, bsz, n), F32),
        grid=(depth, n // tn),
        in_specs=[
            pl.BlockSpec((bsz, d), lambda l, j: (0, 0)),
            pl.BlockSpec((1, d, tn), lambda l, j: (l, 0, j)),
            pl.BlockSpec((1, 1, tn), lambda l, j: (l, 0, j)),
        ],
        out_specs=pl.BlockSpec((1, bsz, tn), lambda l, j: (l, 0, j)),
        compiler_params=_params(("parallel", "parallel")),
        name="ada_mod",
    )(c, w_ada, b_ada.reshape(depth, 1, n))


def _even_in_kernel(x_ref, sc_ref, sh_ref, w_ref, ct_ref, sa_ref, sb_ref, u_ref, *rest):
    qkv_refs, z_ref = rest[:9], rest[9]
    tm = x_ref.shape[0]
    h = _modulate(x_ref[...], sc_ref[0], sh_ref[0])
    u_ref[...] = jnp.dot(h, w_ref[:, 0:POOL_WIDTH], preferred_element_type=F32)
    ct, sa, sb = ct_ref[...], sa_ref[...], sb_ref[...]
    for n in range(9):
        lo = POOL_WIDTH + n * DIL_GROUP_WIDTH
        z = jnp.dot(h, w_ref[:, lo:lo + DIL_GROUP_WIDTH], preferred_element_type=F32)
        if n < 6:
            z = z * ct + pltpu.roll(z, 8, 1) * sa + pltpu.roll(z, DIL_GROUP_WIDTH - 8, 1) * sb
        if n < 3:
            z = z * (HEAD_DIM ** -0.5)
        dil = DIL_CONFIGS[n % 3][1]
        if dil == 1:
            qkv_refs[n][0, 0] = z.astype(BF16)
        else:
            for half in range(2):
                z_ref[half] = z[:, half * LANES:(half + 1) * LANES]
            for r in range(dil):
                rows = [z_ref[half, pl.ds(r, tm // dil, stride=dil), :] for half in range(2)]
                qkv_refs[n][0, r] = jnp.concatenate(rows, axis=1).astype(BF16)


def _even_in(x2, sc, sh, w_bf, tabs, seq):
    t, d = x2.shape
    tm = TOKEN_TILE
    tps = seq // tm
    bsz = t // seq
    n_in = w_bf.shape[1]
    mod_spec = pl.BlockSpec((1, 1, d), lambda i: (i // tps, 0, 0))
    tab_spec = pl.BlockSpec((tm, DIL_GROUP_WIDTH), lambda i: (i % tps, 0))
    out_spec = pl.BlockSpec((tm, DIL_GROUP_WIDTH), lambda i: (i, 0))
    dils = [dil for _, dil in DIL_CONFIGS] * 3
    qkv_shapes = [jax.ShapeDtypeStruct((bsz, dil, seq // dil, DIL_GROUP_WIDTH), BF16) for dil in dils]
    qkv_specs = [pl.BlockSpec((1, dil, tm // dil, DIL_GROUP_WIDTH), lambda i: (i // tps, 0, i % tps, 0))
                 for dil in dils]
    return pl.pallas_call(
        _even_in_kernel,
        out_shape=[jax.ShapeDtypeStruct((t, POOL_WIDTH), F32)] + qkv_shapes,
        grid=(t // tm,),
        in_specs=[
            pl.BlockSpec((tm, d), lambda i: (i, 0)),
            mod_spec,
            mod_spec,
            pl.BlockSpec((d, n_in), lambda i: (0, 0)),
            tab_spec,
            tab_spec,
            tab_spec,
        ],
        out_specs=[out_spec] + qkv_specs,
        scratch_shapes=[pltpu.VMEM((DIL_GROUP_WIDTH // LANES, tm, LANES), F32)],
        compiler_params=_params(("parallel",)),
        name="even_in_proj",
    )(x2, sc, sh, w_bf, *tabs)


def _rope_tables(seq):
    pos = jnp.arange(seq, dtype=F32)
    inv = ROPE_THETA ** (-jnp.arange(0, ROPE_DIM, 2, dtype=F32) / ROPE_DIM)
    ang = pos[:, None] * inv[None, :]
    cos, sin = jnp.cos(ang), jnp.sin(ang)
    half = ROPE_DIM // 2
    zeros = jnp.zeros((seq, half), F32)
    rest = HEAD_DIM - ROPE_DIM
    ct = jnp.concatenate([cos, cos, jnp.ones((seq, rest), F32)], axis=1)
    sa = jnp.concatenate([zeros, sin, jnp.zeros((seq, rest), F32)], axis=1)
    sb = jnp.concatenate([-sin, zeros, jnp.zeros((seq, rest), F32)], axis=1)
    return tuple(jnp.tile(a, (1, DIL_HEADS)) for a in (ct, sa, sb))


def _band_kernel(q_ref, kp_ref, kc_ref, vp_ref, vc_ref, o_ref, lse_ref):
    mt = pl.program_id(2)
    blk = BAND_BLOCK
    nblk = q_ref.shape[2] // blk
    kall = jnp.concatenate([kp_ref[0, 0], kc_ref[0, 0]], axis=0)
    vall = jnp.concatenate([vp_ref[0, 0], vc_ref[0, 0]], axis=0)
    qi = lax.broadcasted_iota(jnp.int32, (blk, 2 * blk), 0)
    kj = lax.broadcasted_iota(jnp.int32, (blk, 2 * blk), 1)
    dist = qi + blk - kj
    in_band = (dist >= 0) & (dist <= blk)
    lane_q = lax.broadcasted_iota(jnp.int32, (blk, DIL_GROUP_WIDTH), 1) // HEAD_DIM
    lane_v = lax.broadcasted_iota(jnp.int32, (2 * blk, DIL_GROUP_WIDTH), 1) // HEAD_DIM
    for i in range(nblk):
        q = q_ref[0, 0, i * blk:(i + 1) * blk, :]
        kk = kall[i * blk:(i + 2) * blk]
        vv = vall[i * blk:(i + 2) * blk]
        if i == 0:
            valid = in_band & (kj >= jnp.where(mt > 0, 0, blk))
        else:
            valid = in_band
        acc = jnp.zeros((blk, DIL_GROUP_WIDTH), F32)
        den_full = jnp.ones((blk, DIL_GROUP_WIDTH), F32)
        lse_full = jnp.zeros((blk, DIL_GROUP_WIDTH), F32)
        for h in range(DIL_HEADS):
            qh = jnp.where(lane_q == h, q, jnp.zeros_like(q))
            s = lax.dot_general(qh, kk, (((1,), (1,)), ((), ())), preferred_element_type=F32)
            s = jnp.where(valid, s, NEG)
            m = jnp.max(s, axis=-1, keepdims=True)
            p = jnp.exp(s - m)
            den = jnp.sum(p, axis=-1, keepdims=True)
            vh = jnp.where(lane_v == h, vv, jnp.zeros_like(vv))
            acc = acc + jnp.dot(p.astype(BF16), vh, preferred_element_type=F32)
            den_full = jnp.where(lane_q == h, den, den_full)
            lse_full = jnp.where(lane_q == h, m + jnp.log(den), lse_full)
        o_ref[0, 0, i * blk:(i + 1) * blk, :] = acc / den_full
        lse_ref[0, 0, i * blk:(i + 1) * blk, :] = lse_full


def _band_attention(q, k, v):
    bsz, dil, ln, w = q.shape
    nblk = min(BAND_BLOCKS_PER_STEP, ln // BAND_BLOCK)
    rows = nblk * BAND_BLOCK
    cur = pl.BlockSpec((1, 1, rows, w), lambda b, r, m: (b, r, m, 0))
    prev = pl.BlockSpec((1, 1, BAND_BLOCK, w), lambda b, r, m: (b, r, jnp.maximum(m * nblk - 1, 0), 0))
    return pl.pallas_call(
        _band_kernel,
        out_shape=[jax.ShapeDtypeStruct(q.shape, F32)] * 2,
        grid=(bsz, dil, ln // rows),
        in_specs=[cur, prev, cur, prev, cur],
        out_specs=[cur, cur],
        compiler_params=_params(("parallel", "parallel", "parallel")),
        name=f"band_attn_d{dil}",
    )(q, k, k, v, v)


def _even_out_kernel(tps, u_ref, up_ref, o0, o1, o2, l0, l1, l2, wp_ref, ps_ref, wo_ref, x_ref, gate_ref,
                     g_ref, b_ref, out_ref, il_ref):
    i = pl.program_id(0)
    tm = u_ref.shape[0]
    halo = up_ref.shape[0]
    u = u_ref[...]
    prev = jnp.where(i % tps == 0, jnp.zeros_like(up_ref[...]), up_ref[...])
    ext = jnp.concatenate([prev, u], axis=0)
    lane_g = lax.broadcasted_iota(jnp.int32, (tm, POOL_WIDTH), 1) // POOL_GROUP_DIM
    row = lax.broadcasted_iota(jnp.int32, (tm, POOL_WIDTH), 0)
    pos1 = ((i % tps) * tm + row + 1).astype(F32)
    s = ext
    pooled = jnp.zeros((tm, POOL_WIDTH), F32)
    wvec = jnp.zeros((tm, POOL_WIDTH), F32)
    shift = 1
    for g, w in enumerate(POOL_WINDOWS):
        while shift < w:
            s = s + pltpu.roll(s, shift, 0)
            shift *= 2
        pooled = jnp.where(lane_g == g, s[halo:, :], pooled)
        wvec = jnp.where(lane_g == g, float(w), wvec)
    pooled = pooled / jnp.minimum(pos1, wvec) - u
    a_out = jnp.dot(pooled.astype(BF16), wp_ref[...], preferred_element_type=F32) * ps_ref[...]
    def token_order(ref, buf):
        dil = ref.shape[1]
        if dil == 1:
            return ref[0, 0]
        for r in range(dil):
            blk = ref[0, r]
            for half in range(2):
                buf[half, pl.ds(r, tm // dil, stride=dil), :] = blk[:, half * LANES:(half + 1) * LANES]
        return jnp.concatenate([buf[0], buf[1]], axis=1)

    la, lb, lc = [token_order(ref, il_ref.at[n]) for n, ref in enumerate((l0, l1, l2))]
    oa, ob, oc = [token_order(ref, il_ref.at[3 + n]) for n, ref in enumerate((o0, o1, o2))]
    mx = jnp.maximum(jnp.maximum(la, lb), lc)
    ea, eb, ec = jnp.exp(la - mx), jnp.exp(lb - mx), jnp.exp(lc - mx)
    b_out = (ea * oa + eb * ob + ec * oc) / (ea + eb + ec)
    mixed = jnp.concatenate([a_out, b_out], axis=-1).astype(BF16)
    sub = jnp.dot(mixed, wo_ref[...], preferred_element_type=F32)
    out_ref[...] = _deepnorm_ln(x_ref[...], sub, gate_ref[0], g_ref[...], b_ref[...])


def _even_out(u, os_, lses, wpool_bd, pool_scale, wo_bf, x2, gate, ln_g, ln_b, seq):
    t, d = x2.shape
    tm = TOKEN_TILE
    tps = seq // tm
    halo = 16
    row = lambda w: pl.BlockSpec((tm, w), lambda i: (i, 0))
    full = lambda a: pl.BlockSpec(a.shape, lambda i: (0,) * a.ndim)
    strided = [pl.BlockSpec((1, a.shape[1], tm // a.shape[1], DIL_GROUP_WIDTH), lambda i: (i // tps, 0, i % tps, 0))
               for a in (*os_, *lses)]
    return pl.pallas_call(
        functools.partial(_even_out_kernel, tps),
        out_shape=jax.ShapeDtypeStruct((t, d), F32),
        grid=(t // tm,),
        in_specs=[
            row(POOL_WIDTH),
            pl.BlockSpec((halo, POOL_WIDTH), lambda i: (jnp.maximum(i * (tm // halo) - 1, 0), 0)),
            *strided,
            full(wpool_bd),
            full(pool_scale),
            full(wo_bf),
            row(d),
            pl.BlockSpec((1, 1, d), lambda i: (i // tps, 0, 0)),
            full(ln_g),
            full(ln_b),
        ],
        out_specs=row(d),
        scratch_shapes=[pltpu.VMEM((6, DIL_GROUP_WIDTH // LANES, tm, LANES), F32)],
        compiler_params=_params(("parallel",)),
        name="even_out_proj",
    )(u, u, *os_, *lses, wpool_bd, pool_scale, wo_bf, x2, gate, ln_g, ln_b)


def _swiglu_rows(x, wgu_ref, wd_ref, acc_ref):
    for c in range(D_FF // FF_SUB):
        lo = c * FF_SUB
        gt = jnp.dot(x, wgu_ref[0, :, lo:lo + FF_SUB], preferred_element_type=F32)
        up = jnp.dot(x, wgu_ref[0, :, D_FF + lo:D_FF + lo + FF_SUB], preferred_element_type=F32)
        a = (gt * jax.nn.sigmoid(gt) * up).astype(BF16)
        part = jnp.dot(a, wd_ref[0, lo:lo + FF_SUB, :], preferred_element_type=F32)
        if c == 0:
            acc_ref[...] = part
        else:
            acc_ref[...] += part


def _ffn_kernel(x_ref, sc_ref, sh_ref, wgu_ref, wd_ref, gate_ref, g_ref, b_ref, out_ref, acc_ref):
    _swiglu_rows(_modulate(x_ref[...], sc_ref[0], sh_ref[0]), wgu_ref, wd_ref, acc_ref)
    out_ref[...] = _deepnorm_ln(x_ref[...], acc_ref[...], gate_ref[0], g_ref[...], b_ref[...])


def _ffn(x2, sc, sh, wgu_bf, wd_bf, gate, ln_g, ln_b, seq):
    t, d = x2.shape
    tm = TOKEN_TILE
    tps = seq // tm
    mod = pl.BlockSpec((1, 1, d), lambda i: (i // tps, 0, 0))
    row = pl.BlockSpec((tm, d), lambda i: (i, 0))
    vec = pl.BlockSpec((1, d), lambda i: (0, 0))
    full = lambda a: pl.BlockSpec(a.shape, lambda i: (0,) * a.ndim)
    return pl.pallas_call(
        _ffn_kernel,
        out_shape=jax.ShapeDtypeStruct((t, d), F32),
        grid=(t // tm,),
        in_specs=[row, mod, mod, full(wgu_bf), full(wd_bf), mod, vec, vec],
        out_specs=row,
        scratch_shapes=[pltpu.VMEM((tm, d), F32)],
        compiler_params=_params(("parallel",)),
        name="ffn_dense",
    )(x2, sc, sh, wgu_bf, wd_bf, gate, ln_g, ln_b)


def _split3(x):
    hi = x.astype(BF16)
    r = x - hi.astype(F32)
    mid = r.astype(BF16)
    lo = (r - mid.astype(F32)).astype(BF16)
    return hi, mid, lo


def _odd_in_kernel(tps, x_ref, sc_ref, sh_ref, w_ref, wf_ref, bf_ref, place_ref, q_ref, k_ref, v_ref, f_ref,
                   carry_ref):
    i = pl.program_id(0)
    tm = x_ref.shape[0]
    h = _modulate(x_ref[...], sc_ref[0], sh_ref[0])
    q = jnp.dot(h, w_ref[:, 0:FOX_WIDTH], preferred_element_type=F32)
    q_ref[...] = (q * (HEAD_DIM ** -0.5 * LOG2E)).astype(BF16)
    k_ref[...] = jnp.dot(h, w_ref[:, FOX_WIDTH:2 * FOX_WIDTH], preferred_element_type=F32).astype(BF16)
    v_ref[...] = jnp.dot(h, w_ref[:, 2 * FOX_WIDTH:3 * FOX_WIDTH], preferred_element_type=F32).astype(BF16)
    f = jnp.dot(h, wf_ref[...], preferred_element_type=F32) + bf_ref[...]
    logf = jnp.minimum(f, 0.0) - jnp.log(1.0 + jnp.exp(-jnp.abs(f)))
    r = lax.broadcasted_iota(jnp.int32, (tm, tm), 0)
    cidx = lax.broadcasted_iota(jnp.int32, (tm, tm), 1)
    tri = jnp.where(cidx <= r, 1.0, 0.0).astype(BF16)
    hi, mid, lo = _split3(logf)
    cs = (jnp.dot(tri, lo, preferred_element_type=F32) + jnp.dot(tri, mid, preferred_element_type=F32)
          + jnp.dot(tri, hi, preferred_element_type=F32))

    @pl.when(i % tps == 0)
    def _():
        carry_ref[...] = jnp.zeros_like(carry_ref)

    cs = cs + carry_ref[...]
    carry_ref[...] = cs[tm - 1:tm, :]
    parts = jnp.concatenate(_split3(cs * LOG2E), axis=1)
    f_ref[...] = jnp.dot(parts, place_ref[...], preferred_element_type=F32).astype(BF16)


def _gate_placement():
    place = np.zeros((3 * LANES, FOX_WIDTH), np.float32)
    for h in range(FOX_HEADS):
        base = (h // 2) * LANES + (HEAD_DIM if h % 2 == 0 else 0)
        for p in range(3):
            place[p * LANES + h, base + p] = 1.0
    return jnp.asarray(place, BF16)


def _odd_in(x2, sc, sh, w_bf, wf_bf, bf_pad, seq):
    t, d = x2.shape
    tm = TOKEN_TILE
    tps = seq // tm
    mod = pl.BlockSpec((1, 1, d), lambda i: (i // tps, 0, 0))
    row = lambda w: pl.BlockSpec((tm, w), lambda i: (i, 0))
    full = lambda a: pl.BlockSpec(a.shape, lambda i: (0,) * a.ndim)
    place = _gate_placement()
    return pl.pallas_call(
        functools.partial(_odd_in_kernel, tps),
        out_shape=[jax.ShapeDtypeStruct((t, FOX_WIDTH), BF16)] * 4,
        grid=(t // tm,),
        in_specs=[row(d), mod, mod, full(w_bf), full(wf_bf), full(bf_pad), full(place)],
        out_specs=[row(FOX_WIDTH)] * 4,
        scratch_shapes=[pltpu.VMEM((1, LANES), F32)],
        compiler_params=_params(("arbitrary",)),
        name="odd_in_proj",
    )(x2, sc, sh, w_bf, wf_bf, bf_pad, place)


def _fox_kernel(q_ref, k_ref, v_ref, f_ref, o_ref, acc_ref, m_ref, s0_ref, s1_ref, p0_ref, p1_ref):
    qi = pl.program_id(2)
    tq = q_ref.shape[1]
    tk = FOX_K_TILE
    half = HEAD_DIM
    q = q_ref[0]
    lane_q = lax.broadcasted_iota(jnp.int32, q.shape, 1)
    lane_k = lax.broadcasted_iota(jnp.int32, (tk, LANES), 1)
    qa = jnp.where(lane_q < half, q, jnp.where(lane_q < half + 3, -1.0, 0.0).astype(BF16))
    qb = jnp.where(lane_q >= half, q, jnp.where(lane_q < 3, -1.0, 0.0).astype(BF16))
    ones_a = jnp.where(lane_k == half, 1.0, 0.0).astype(BF16)
    ones_b = jnp.where(lane_k == 0, 1.0, 0.0).astype(BF16)
    acc_ref[...] = jnp.zeros_like(acc_ref)
    m_ref[...] = jnp.full(m_ref.shape, NEG, F32)

    def scores(j, s_ref):
        start = pl.multiple_of(j * tk, tk)
        k = k_ref[0, pl.ds(start, tk), :]
        fa = f_ref[0, pl.ds(start, tk), :]
        ka = jnp.where(lane_k < half, k, fa)
        kb = jnp.where(lane_k >= half, k, fa)
        for hh, (qh, kh) in enumerate(((qa, ka), (qb, kb))):
            s_ref[hh] = lax.dot_general(qh, kh, (((1,), (1,)), ((), ())), preferred_element_type=F32)

    def attend(j, s_ref, p_ref, masked):
        start = pl.multiple_of(j * tk, tk)
        v = v_ref[0, pl.ds(start, tk), :]
        va = jnp.where(lane_k < half, v, ones_a)
        vb = jnp.where(lane_k >= half, v, ones_b)
        for hh, vh in enumerate((va, vb)):
            for r0 in range(0, tq, FOX_ROW_CHUNK):
                rows = pl.ds(r0, FOX_ROW_CHUNK)
                s = s_ref[hh, rows, :]
                if masked:
                    rr = lax.broadcasted_iota(jnp.int32, s.shape, 0) + (qi * tq + r0)
                    cc = lax.broadcasted_iota(jnp.int32, s.shape, 1) + start
                    s = jnp.where(cc <= rr, s, NEG)
                m_old = m_ref[hh, rows, :]
                m_new = jnp.maximum(m_old, jnp.max(s, axis=-1, keepdims=True))
                p_ref[hh, rows, :] = jnp.exp2(s - jnp.tile(m_new, (1, tk // LANES))).astype(BF16)
                acc_ref[hh, rows, :] = acc_ref[hh, rows, :] * jnp.exp2(m_old - m_new)
                m_ref[hh, rows, :] = m_new
            acc_ref[hh] += jnp.dot(p_ref[hh], vh, preferred_element_type=F32)

    scores(0, s0_ref)

    def body(jp, carry):
        scores(2 * jp + 1, s1_ref)
        attend(2 * jp, s0_ref, p0_ref, False)
        scores(2 * jp + 2, s0_ref)
        attend(2 * jp + 1, s1_ref, p1_ref, False)
        return carry

    lax.fori_loop(0, qi // 2, body, 0)

    @pl.when(qi % 2 == 0)
    def _():
        attend(qi, s0_ref, p0_ref, True)

    @pl.when(qi % 2 == 1)
    def _():
        scores(qi, s1_ref)
        attend(qi - 1, s0_ref, p0_ref, False)
        attend(qi, s1_ref, p1_ref, True)

    acc_a, acc_b = acc_ref[0], acc_ref[1]
    lane_o = lax.broadcasted_iota(jnp.int32, (tq, LANES), 1)
    o = jnp.where(lane_o < half, acc_a / acc_a[:, half:half + 1], acc_b / acc_b[:, 0:1])
    o_ref[0] = o.astype(BF16)


def _fox_attention(q, k, v, fa, bsz, seq):
    tq = FOX_Q_TILE
    npairs = FOX_HEADS // 2
    tile = pl.BlockSpec((1, tq, LANES), lambda b, j, i: (b, i, j))
    whole = pl.BlockSpec((1, seq, LANES), lambda b, j, i: (b, 0, j))
    return pl.pallas_call(
        _fox_kernel,
        out_shape=jax.ShapeDtypeStruct((bsz, seq, FOX_WIDTH), BF16),
        grid=(bsz, npairs, seq // tq),
        in_specs=[tile, whole, whole, whole],
        out_specs=tile,
        scratch_shapes=[pltpu.VMEM((2, tq, LANES), F32), pltpu.VMEM((2, tq, LANES), F32),
                        pltpu.VMEM((2, tq, FOX_K_TILE), F32), pltpu.VMEM((2, tq, FOX_K_TILE), F32),
                        pltpu.VMEM((2, tq, FOX_K_TILE), BF16), pltpu.VMEM((2, tq, FOX_K_TILE), BF16)],
        compiler_params=_params(("parallel", "parallel", "arbitrary")),
        name="fox_attn",
    )(q, k, v, fa)


def _odd_out_kernel(o_ref, wo_ref, x_ref, gate_ref, g_ref, b_ref, out_ref):
    sub = jnp.dot(o_ref[...], wo_ref[...], preferred_element_type=F32)
    out_ref[...] = _deepnorm_ln(x_ref[...], sub, gate_ref[0], g_ref[...], b_ref[...])


def _odd_out(o2, wo_bf, x2, gate, ln_g, ln_b, seq):
    t, d = x2.shape
    tm = TOKEN_TILE
    tps = seq // tm
    row = lambda w: pl.BlockSpec((tm, w), lambda i: (i, 0))
    full = lambda a: pl.BlockSpec(a.shape, lambda i: (0,) * a.ndim)
    return pl.pallas_call(
        _odd_out_kernel,
        out_shape=jax.ShapeDtypeStruct((t, d), F32),
        grid=(t // tm,),
        in_specs=[row(FOX_WIDTH), full(wo_bf), row(d), pl.BlockSpec((1, 1, d), lambda i: (i // tps, 0, 0)),
                  full(ln_g), full(ln_b)],
        out_specs=row(d),
        compiler_params=_params(("parallel",)),
        name="odd_out_proj",
    )(o2, wo_bf, x2, gate, ln_g, ln_b)


def _router_kernel(x_ref, sc_ref, sh_ref, wrt_ref, tri_ref, rankt_ref, cnt_ref, comb_ref, rank_ref):
    c = x_ref.shape[0]
    h = _modulate(x_ref[...], sc_ref[0], sh_ref[0])
    lt = lax.dot_general(wrt_ref[...], h, (((1,), (1,)), ((), ())), preferred_element_type=F32)[0:N_EXPERTS]
    row = lax.broadcasted_iota(jnp.int32, lt.shape, 0)
    m1 = jnp.max(lt, axis=0, keepdims=True)
    i1 = jnp.min(jnp.where(lt == m1, row, N_EXPERTS), axis=0, keepdims=True)
    rest = jnp.where(row == i1, -jnp.inf, lt)
    m2 = jnp.max(rest, axis=0, keepdims=True)
    i2 = jnp.min(jnp.where(rest == m2, row, N_EXPERTS), axis=0, keepdims=True)
    e2 = jnp.exp(m2 - m1)
    g1 = 1.0 / (1.0 + e2)
    g2 = e2 / (1.0 + e2)
    combt = jnp.where(row == i1, g1, jnp.where(row == i2, g2, 0.0))
    sel = jnp.where((row == i1) | (row == i2), 1.0, 0.0)
    sel16 = jnp.concatenate([sel, jnp.zeros_like(sel)], axis=0).astype(BF16)
    earlier = jnp.dot(sel16, tri_ref[...], preferred_element_type=F32)[0:N_EXPERTS]
    rankt = jnp.where(sel > 0.0, earlier, NO_ROW)
    rankt_ref[...] = rankt
    cnt_ref[0] = jnp.broadcast_to(jnp.sum(sel, axis=1, keepdims=True), (N_EXPERTS, LANES))
    pad = LANES - N_EXPERTS
    comb_ref[...] = jnp.concatenate([combt, jnp.zeros((pad, c), F32)], axis=0).T
    rank_ref[...] = jnp.concatenate([rankt, jnp.full((pad, c), NO_ROW, F32)], axis=0).T


def _router(x2, sc, sh, wrt, seq):
    t, d = x2.shape
    cs = MOE_CHUNK
    cps = seq // cs
    mod = pl.BlockSpec((1, 1, d), lambda i: (i // cps, 0, 0))
    tri = jnp.asarray(np.triu(np.ones((cs, cs), np.float32), 1), BF16)
    return pl.pallas_call(
        _router_kernel,
        out_shape=[jax.ShapeDtypeStruct((N_EXPERTS, t), F32),
                   jax.ShapeDtypeStruct((t // cs, N_EXPERTS, LANES), F32),
                   jax.ShapeDtypeStruct((t, LANES), F32),
                   jax.ShapeDtypeStruct((t, LANES), F32)],
        grid=(t // cs,),
        in_specs=[pl.BlockSpec((cs, d), lambda i: (i, 0)), mod, mod,
                  pl.BlockSpec(wrt.shape, lambda i: (0, 0)), pl.BlockSpec(tri.shape, lambda i: (0, 0))],
        out_specs=[pl.BlockSpec((N_EXPERTS, cs), lambda i: (0, i)),
                   pl.BlockSpec((1, N_EXPERTS, LANES), lambda i: (i, 0, 0)),
                   pl.BlockSpec((cs, LANES), lambda i: (i, 0)),
                   pl.BlockSpec((cs, LANES), lambda i: (i, 0))],
        compiler_params=_params(("parallel",)),
        name="router",
    )(x2, sc, sh, wrt, tri)


def _route_meta(cnt, n_tiles):
    before = jnp.cumsum(cnt, axis=0) - cnt
    total = jnp.sum(cnt, axis=0)
    region = ((total + MOE_TILE - 1) // MOE_TILE) * MOE_TILE
    base = jnp.cumsum(region) - region
    n_used = (jnp.sum(region) // MOE_TILE).reshape(1)
    starts = jnp.arange(n_tiles, dtype=jnp.int32) * MOE_TILE
    tile_expert = jnp.minimum(jnp.sum(starts[:, None] >= (base + region)[None, :], axis=1), N_EXPERTS - 1)
    i32 = lambda a: a.astype(jnp.int32)
    return i32(before.reshape(-1)), i32(cnt.reshape(-1)), i32(base), i32(total), i32(tile_expert), i32(n_used)


def _chunk_block(before_ref, cnt_ref, chunk, e, j):
    g = before_ref[chunk * N_EXPERTS + e]
    n = cnt_ref[chunk * N_EXPERTS + e]
    k = g // ROW_BLOCK + j
    return g, n, k, (n > 0) & (k * ROW_BLOCK < g + n)


def _dispatch_kernel(before_ref, cnt_ref, base_ref, total_ref, x_ref, sc_ref, sh_ref, rankt_ref, xs_ref,
                     stage_ref, slot_ref, zero_ref, sem, fsem, zsem, usem):
    c = pl.program_id(0)
    last = c == pl.num_programs(0) - 1
    h = _modulate(x_ref[...], sc_ref[0], sh_ref[0])

    @pl.when(c == 0)
    def _():
        stage_ref[...] = jnp.zeros_like(stage_ref)
        zero_ref[...] = jnp.zeros_like(zero_ref)

    block_row = lax.broadcasted_iota(jnp.int32, (ROW_BLOCK, MOE_CHUNK), 0).astype(F32)
    pair_row = lax.broadcasted_iota(jnp.int32, (2 * ROW_BLOCK, MOE_CHUNK), 0).astype(F32)

    def block_copy(e, j, k):
        row0 = pl.multiple_of(base_ref[e] + k * ROW_BLOCK, ROW_BLOCK)
        return pltpu.make_async_copy(slot_ref.at[e, j], xs_ref.at[pl.ds(row0, ROW_BLOCK)], sem.at[e, j])

    def any_beyond_two(chunk):
        flag = _chunk_block(before_ref, cnt_ref, chunk, 0, 2)[3]
        for e in range(1, N_EXPERTS):
            flag = flag | _chunk_block(before_ref, cnt_ref, chunk, e, 2)[3]
        return flag

    def for_blocks(chunk, fn):
        for e in range(N_EXPERTS):
            for j in range(2):
                fn(e, j)

        @pl.when(any_beyond_two(chunk))
        def _():
            for e in range(N_EXPERTS):
                for j in range(2, MAX_SPAN):
                    fn(e, j)

    def wait_block_copies(chunk):
        def wait_one(e, j):
            g, n, k, touched = _chunk_block(before_ref, cnt_ref, chunk, e, j)

            @pl.when(touched & (g + n >= (k + 1) * ROW_BLOCK))
            def _():
                block_copy(e, j, k).wait()

        for_blocks(chunk, wait_one)

    selectors = []
    for e in range(N_EXPERTS):
        g = before_ref[c * N_EXPERTS + e]
        off = (g % ROW_BLOCK).astype(F32)
        selectors.append(jnp.where(rankt_ref[e:e + 1, :] + off == pair_row, 1.0, 0.0).astype(BF16))
    first_two = jnp.dot(jnp.concatenate(selectors, axis=0), h, preferred_element_type=F32).astype(BF16)

    @pl.when(c > 0)
    def _():
        wait_block_copies(c - 1)

    def emit_block(e, j):
        g, n, k, touched = _chunk_block(before_ref, cnt_ref, c, e, j)
        complete = g + n >= (k + 1) * ROW_BLOCK

        @pl.when(touched)
        def _():
            if j < 2:
                r0 = (2 * e + j) * ROW_BLOCK
                blk = first_two[r0:r0 + ROW_BLOCK]
            else:
                off = (g - k * ROW_BLOCK).astype(F32)
                onehot = jnp.where(rankt_ref[e:e + 1, :] + off == block_row, 1.0, 0.0).astype(BF16)
                blk = jnp.dot(onehot, h, preferred_element_type=F32).astype(BF16)
            if j == 0:
                blk = (blk.astype(F32) + stage_ref[e].astype(F32)).astype(BF16)
            slot_ref[e, j] = blk
            stage_ref[e] = jnp.where(complete, jnp.zeros_like(blk), blk)

            @pl.when(complete)
            def _():
                block_copy(e, j, k).start()

    for_blocks(c, emit_block)

    def tail_copies(e):
        total = total_ref[e]
        n_data = (total + ROW_BLOCK - 1) // ROW_BLOCK
        n_region = ((total + MOE_TILE - 1) // MOE_TILE) * (MOE_TILE // ROW_BLOCK)
        part_row = pl.multiple_of(base_ref[e] + (total // ROW_BLOCK) * ROW_BLOCK, ROW_BLOCK)
        part = pltpu.make_async_copy(stage_ref.at[e], xs_ref.at[pl.ds(part_row, ROW_BLOCK)], fsem.at[e])
        out = [(total % ROW_BLOCK != 0, part)]
        for z in range(MOE_TILE // ROW_BLOCK - 1):
            zrow = pl.multiple_of(base_ref[e] + (n_data + z) * ROW_BLOCK, ROW_BLOCK)
            zc = pltpu.make_async_copy(zero_ref, xs_ref.at[pl.ds(zrow, ROW_BLOCK)], zsem.at[e, z])
            out.append((n_data + z < n_region, zc))
        return out

    def unused_copies():
        e = N_EXPERTS - 1
        used = base_ref[e] + ((total_ref[e] + MOE_TILE - 1) // MOE_TILE) * MOE_TILE
        out = []
        for z in range(usem.shape[0]):
            zrow = pl.multiple_of(used + z * ROW_BLOCK, ROW_BLOCK)
            zc = pltpu.make_async_copy(zero_ref, xs_ref.at[pl.ds(zrow, ROW_BLOCK)], usem.at[z])
            out.append((zrow < xs_ref.shape[0], zc))
        return out

    def all_tail_copies():
        return [cc for e in range(N_EXPERTS) for cc in tail_copies(e)] + unused_copies()

    @pl.when(last)
    def _():
        for cond, cp in all_tail_copies():
            @pl.when(cond)
            def _():
                cp.start()

    @pl.when(last)
    def _():
        wait_block_copies(c)

    @pl.when(last)
    def _():
        for cond, cp in all_tail_copies():
            @pl.when(cond)
            def _():
                cp.wait()


def _dispatch(meta, x2, sc, sh, rankt, n_tiles, seq):
    before, cnt, base, total = meta
    t, d = x2.shape
    cs = MOE_CHUNK
    cps = seq // cs
    mod = pl.BlockSpec((1, 1, d), lambda i, *_: (i // cps, 0, 0))
    grid_spec = pltpu.PrefetchScalarGridSpec(
        num_scalar_prefetch=4,
        grid=(t // cs,),
        in_specs=[pl.BlockSpec((cs, d), lambda i, *_: (i, 0)), mod, mod,
                  pl.BlockSpec((N_EXPERTS, cs), lambda i, *_: (0, i))],
        out_specs=pl.BlockSpec(memory_space=pl.ANY),
        scratch_shapes=[pltpu.VMEM((N_EXPERTS, ROW_BLOCK, d), BF16),
                        pltpu.VMEM((N_EXPERTS, MAX_SPAN, ROW_BLOCK, d), BF16),
                        pltpu.VMEM((ROW_BLOCK, d), BF16),
                        pltpu.SemaphoreType.DMA((N_EXPERTS, MAX_SPAN)),
                        pltpu.SemaphoreType.DMA((N_EXPERTS,)),
                        pltpu.SemaphoreType.DMA((N_EXPERTS, MOE_TILE // ROW_BLOCK - 1)),
                        pltpu.SemaphoreType.DMA(((n_tiles * MOE_TILE - 2 * t) // ROW_BLOCK,))],
    )
    return pl.pallas_call(
        _dispatch_kernel,
        out_shape=jax.ShapeDtypeStruct((n_tiles * MOE_TILE, d), BF16),
        grid_spec=grid_spec,
        compiler_params=_params(("arbitrary",)),
        name="moe_dispatch",
    )(before, cnt, base, total, x2, sc, sh, rankt)


def _experts_kernel(te_ref, nu_ref, x_ref, wgu_ref, wd_ref, y_ref, acc_ref):
    @pl.when(pl.program_id(0) >= nu_ref[0])
    def _():
        y_ref[...] = jnp.zeros_like(y_ref)

    @pl.when(pl.program_id(0) < nu_ref[0])
    def _():
        _swiglu_rows(x_ref[...], wgu_ref, wd_ref, acc_ref)
        y_ref[...] = acc_ref[...].astype(BF16)


def _experts(tile_expert, n_used, xs, wgu_bf, wd_bf):
    rows, d = xs.shape
    tm = MOE_TILE
    clamp = lambda i, nu: jnp.minimum(i, nu[0] - 1)
    grid_spec = pltpu.PrefetchScalarGridSpec(
        num_scalar_prefetch=2,
        grid=(rows // tm,),
        in_specs=[pl.BlockSpec((tm, d), lambda i, te, nu: (clamp(i, nu), 0)),
                  pl.BlockSpec((1, d, 2 * D_FF), lambda i, te, nu: (te[clamp(i, nu)], 0, 0)),
                  pl.BlockSpec((1, D_FF, d), lambda i, te, nu: (te[clamp(i, nu)], 0, 0))],
        out_specs=pl.BlockSpec((tm, d), lambda i, te, nu: (i, 0)),
        scratch_shapes=[pltpu.VMEM((tm, d), F32)],
    )
    return pl.pallas_call(
        _experts_kernel,
        out_shape=jax.ShapeDtypeStruct((rows, d), BF16),
        grid_spec=grid_spec,
        compiler_params=_params(("arbitrary",)),
        name="moe_experts",
    )(tile_expert, n_used, xs, wgu_bf, wd_bf)


def _combine_kernel(before_ref, cnt_ref, base_ref, rank_ref, comb_ref, x_ref, gate_ref, g_ref, b_ref, y_ref,
                    out_ref, ybuf, xbuf, acc_ref, sem, xsem):
    c = pl.program_id(0)
    slot = c % 2

    def window(chunk, e, w):
        g = before_ref[chunk * N_EXPERTS + e]
        n = cnt_ref[chunk * N_EXPERTS + e]
        first = (g // ROW_BLOCK) * ROW_BLOCK + w * COMBINE_WIN
        row0 = pl.multiple_of(jnp.minimum(base_ref[e] + first, y_ref.shape[0] - COMBINE_WIN), ROW_BLOCK)
        off = (base_ref[e] + g - row0).astype(F32)
        return (n > 0) & (first < g + n), row0, off

    def first_copy(chunk, sl, e):
        touched, row0, _ = window(chunk, e, 0)
        return touched, pltpu.make_async_copy(y_ref.at[pl.ds(row0, COMBINE_WIN)], ybuf.at[sl, e], sem.at[sl, e])

    def later_copy(chunk, sl, e, w):
        touched, row0, _ = window(chunk, e, w)
        return touched, pltpu.make_async_copy(y_ref.at[pl.ds(row0, COMBINE_WIN)], xbuf.at[sl, e, w - 1],
                                              xsem.at[sl, e, w - 1])

    def any_later(chunk):
        flag = window(chunk, 0, 1)[0]
        for e in range(1, N_EXPERTS):
            flag = flag | window(chunk, e, 1)[0]
        return flag

    def fetch(chunk, sl):
        for e in range(N_EXPERTS):
            touched, cp = first_copy(chunk, sl, e)

            @pl.when(touched)
            def _():
                cp.start()

        @pl.when(any_later(chunk))
        def _():
            for e in range(N_EXPERTS):
                for w in range(1, COMBINE_MAX_WIN):
                    touched, cp = later_copy(chunk, sl, e, w)

                    @pl.when(touched)
                    def _():
                        cp.start()

    @pl.when(c == 0)
    def _():
        ybuf[...] = jnp.zeros_like(ybuf)
        fetch(0, 0)

    @pl.when(c + 1 < pl.num_programs(0))
    def _():
        fetch(c + 1, 1 - slot)

    lane = lax.broadcasted_iota(jnp.int32, (MOE_CHUNK, COMBINE_WIN), 1).astype(F32)
    acc_ref[...] = jnp.zeros_like(acc_ref)
    for e in range(N_EXPERTS):
        col = rank_ref[:, e:e + 1]
        weight = comb_ref[:, e:e + 1]
        touched, _, off = window(c, e, 0)

        @pl.when(touched)
        def _():
            first_copy(c, slot, e)[1].wait()

        onehot = jnp.where(col + off == lane, 1.0, 0.0).astype(BF16)
        acc_ref[...] += weight * jnp.dot(onehot, ybuf[slot, e], preferred_element_type=F32)

    @pl.when(any_later(c))
    def _():
        for e in range(N_EXPERTS):
            for w in range(1, COMBINE_MAX_WIN):
                touched, _, off = window(c, e, w)

                @pl.when(touched)
                def _():
                    later_copy(c, slot, e, w)[1].wait()
                    onehot = jnp.where(rank_ref[:, e:e + 1] + off == lane, 1.0, 0.0).astype(BF16)
                    acc_ref[...] += comb_ref[:, e:e + 1] * jnp.dot(onehot, xbuf[slot, e, w - 1],
                                                                   preferred_element_type=F32)

    out_ref[...] = _deepnorm_ln(x_ref[...], acc_ref[...], gate_ref[0], g_ref[...], b_ref[...])


def _combine(meta, rank, comb, x2, gate, ln_g, ln_b, y, seq):
    before, cnt, base, _ = meta
    t, d = x2.shape
    cs = MOE_CHUNK
    cps = seq // cs
    row = lambda w: pl.BlockSpec((cs, w), lambda i, *_: (i, 0))
    vec = pl.BlockSpec((1, d), lambda i, *_: (0, 0))
    grid_spec = pltpu.PrefetchScalarGridSpec(
        num_scalar_prefetch=3,
        grid=(t // cs,),
        in_specs=[row(LANES), row(LANES), row(d), pl.BlockSpec((1, 1, d), lambda i, *_: (i // cps, 0, 0)),
                  vec, vec, pl.BlockSpec(memory_space=pl.ANY)],
        out_specs=row(d),
        scratch_shapes=[pltpu.VMEM((2, N_EXPERTS, COMBINE_WIN, d), BF16),
                        pltpu.VMEM((2, N_EXPERTS, COMBINE_MAX_WIN - 1, COMBINE_WIN, d), BF16),
                        pltpu.VMEM((cs, d), F32),
                        pltpu.SemaphoreType.DMA((2, N_EXPERTS)),
                        pltpu.SemaphoreType.DMA((2, N_EXPERTS, COMBINE_MAX_WIN - 1))],
    )
    return pl.pallas_call(
        _combine_kernel,
        out_shape=jax.ShapeDtypeStruct((t, d), F32),
        grid_spec=grid_spec,
        compiler_params=_params(("arbitrary",)),
        name="moe_combine",
    )(before, cnt, base, rank, comb, x2, gate, ln_g, ln_b, y)


def _block_diag(w_pool):
    g, c, _ = w_pool.shape
    out = jnp.zeros((g * c, g * c), w_pool.dtype)
    for i in range(g):
        out = out.at[i * c:(i + 1) * c, i * c:(i + 1) * c].set(w_pool[i])
    return out


def kernel(x, c, w_ada, b_ada, ln_g, ln_b, w_in_even, w_pool, pool_scale, w_out_even, w_ffn_gu, w_ffn_down,
           w_in_odd, b_forget, w_out_odd, w_router, w_exp_gu, w_exp_down):
    bsz, seq, d = x.shape
    t = bsz * seq
    tabs = _rope_tables(seq)
    ada = _ada(c, w_ada, b_ada)
    x2 = x.reshape(t, d)
    for l in range(DEPTH):
        i = l // 2
        sh1, sc1, g1, sh2, sc2, g2 = [ada[l, :, n * d:(n + 1) * d].reshape(bsz, 1, d) for n in range(6)]
        lng = ln_g[l].reshape(2, 1, d)
        lnb = ln_b[l].reshape(2, 1, d)
        if l % 2 == 0:
            outs = _even_in(x2, sc1, sh1, w_in_even[i].astype(BF16), tabs, seq)
            u, qkv = outs[0], outs[1:]
            os_, lses = [], []
            for g in range(len(DIL_CONFIGS)):
                o, lse = _band_attention(qkv[g], qkv[3 + g], qkv[6 + g])
                os_.append(o)
                lses.append(lse)
            x2 = _even_out(u, os_, lses, _block_diag(w_pool[i]).astype(BF16), pool_scale[i].reshape(1, -1),
                           w_out_even[i].astype(BF16), x2, g1, lng[0], lnb[0], seq)
            x2 = _ffn(x2, sc2, sh2, w_ffn_gu[i][None].astype(BF16), w_ffn_down[i][None].astype(BF16), g2, lng[1],
                      lnb[1], seq)
        else:
            w_in = w_in_odd[i]
            wf = jnp.zeros((d, LANES), F32).at[:, :FOX_HEADS].set(w_in[:, 3 * FOX_WIDTH:]).astype(BF16)
            bfp = jnp.zeros((1, LANES), F32).at[0, :FOX_HEADS].set(b_forget[i])
            q, k, v, fa = [a.reshape(bsz, seq, FOX_WIDTH) for a in
                           _odd_in(x2, sc1, sh1, w_in[:, :3 * FOX_WIDTH].astype(BF16), wf, bfp, seq)]
            o = _fox_attention(q, k, v, fa, bsz, seq)
            x2 = _odd_out(o.reshape(t, FOX_WIDTH), w_out_odd[i].astype(BF16), x2, g1, lng[0], lnb[0], seq)
            wrt = jnp.zeros((2 * N_EXPERTS, d), F32).at[:N_EXPERTS].set(w_router[i].T).astype(BF16)
            rankt, cnt, comb, rank = _router(x2, sc2, sh2, wrt, seq)
            n_tiles = (2 * t + N_EXPERTS * (MOE_TILE - 1) + MOE_TILE - 1) // MOE_TILE
            *meta, tile_expert, n_used = _route_meta(cnt[:, :, 0].astype(jnp.int32), n_tiles)
            xs = _dispatch(meta, x2, sc2, sh2, rankt, n_tiles, seq)
            y = _experts(tile_expert, n_used, xs, w_exp_gu[i].astype(BF16), w_exp_down[i].astype(BF16))
            x2 = _combine(meta, rank, comb, x2, g2, lng[1], lnb[1], y, seq)
    return x2.reshape(bsz, seq, d)
```

```python
import functools
import math

import numpy as np

import jax
import jax.numpy as jnp
from jax import lax
from jax.experimental import pallas as pl
from jax.experimental.pallas import tpu as pltpu

F32 = jnp.float32
BF16 = jnp.bfloat16

D_MODEL = 1024
DEPTH = 4
HEAD_DIM = 64
POOL_WINDOWS = (2, 4, 8, 16)
POOL_GROUP_DIM = 64
POOL_WIDTH = 256
DIL_CONFIGS = ((128, 1), (512, 4), (2048, 16))
DIL_HEADS = 4
DIL_GROUP_WIDTH = DIL_HEADS * HEAD_DIM
BAND_BLOCK = 128
BAND_BLOCKS_PER_STEP = 4
FOX_HEADS = 16
FOX_WIDTH = 1024
ROPE_THETA = 500000.0
ROPE_DIM = 16
D_FF = 2816
N_EXPERTS = 8
DEEPNORM_ALPHA = (2 * DEPTH) ** 0.25
LN_EPS = 1e-5
NEG = -1e30
LOG2E = 1.4426950408889634

LANES = 128
TOKEN_TILE = 512
FOX_K_TILE = 512
FOX_Q_TILE = 2 * FOX_K_TILE
FOX_ROW_CHUNK = 64
MOE_CHUNK = 512
MOE_TILE = 512
ROW_BLOCK = 128
MAX_SPAN = MOE_CHUNK // ROW_BLOCK + 1
FF_SUB = 256
COMBINE_WIN = 2 * ROW_BLOCK
COMBINE_MAX_WIN = (ROW_BLOCK - 1 + MOE_CHUNK + COMBINE_WIN - 1) // COMBINE_WIN
NO_ROW = -1e9
VMEM_LIMIT = 56 * 1024 * 1024


def _params(sem, vmem=VMEM_LIMIT):
    return pltpu.CompilerParams(dimension_semantics=sem, vmem_limit_bytes=vmem)


def _modulate(x, sc, sh):
    return (x * (1.0 + sc) + sh).astype(BF16)


def _deepnorm_ln(x, sub, gate, g, b):
    y = DEEPNORM_ALPHA * x + gate * sub
    mu = jnp.mean(y, axis=-1, keepdims=True)
    yc = y - mu
    var = jnp.mean(yc * yc, axis=-1, keepdims=True)
    return yc * lax.rsqrt(var + LN_EPS) * g + b


def _ada_kernel(c_ref, w_ref, b_ref, o_ref):
    c = c_ref[...]
    ca = (c * jax.nn.sigmoid(c)).astype(BF16)
    o_ref[0] = jnp.dot(ca, w_ref[0].astype(BF16), preferred_element_type=F32) + b_ref[0]


def _ada(c, w_ada, b_ada):
    depth, d, n = w_ada.shape
    bsz = c.shape[0]
    tn = 1024
    return pl.pallas_call(
        _ada_kernel,
        out_shape=jax.ShapeDtypeStruct((depth, bsz, n), F32),
        grid=(depth, n // tn),
        in_specs=[
            pl.BlockSpec((bsz, d), lambda l, j: (0, 0)),
            pl.BlockSpec((1, d, tn), lambda l, j: (l, 0, j)),
            pl.BlockSpec((1, 1, tn), lambda l, j: (l, 0, j)),
        ],
        out_specs=pl.BlockSpec((1, bsz, tn), lambda l, j: (l, 0, j)),
        compiler_params=_params(("parallel", "parallel")),
        name="ada_mod",
    )(c, w_ada, b_ada.reshape(depth, 1, n))


def _even_in_kernel(x_ref, sc_ref, sh_ref, w_ref, ct_ref, sa_ref, sb_ref, u_ref, *rest):
    qkv_refs, z_ref = rest[:9], rest[9]
    tm = x_ref.shape[0]
    h = _modulate(x_ref[...], sc_ref[0], sh_ref[0])
    u_ref[...] = jnp.dot(h, w_ref[:, 0:POOL_WIDTH], preferred_element_type=F32)
    ct, sa, sb = ct_ref[...], sa_ref[...], sb_ref[...]
    for n in range(9):
        lo = POOL_WIDTH + n * DIL_GROUP_WIDTH
        z = jnp.dot(h, w_ref[:, lo:lo + DIL_GROUP_WIDTH], preferred_element_type=F32)
        if n < 6:
            z = z * ct + pltpu.roll(z, 8, 1) * sa + pltpu.roll(z, DIL_GROUP_WIDTH - 8, 1) * sb
        if n < 3:
            z = z * (HEAD_DIM ** -0.5)
        dil = DIL_CONFIGS[n % 3][1]
        if dil == 1:
            qkv_refs[n][0, 0] = z.astype(BF16)
        else:
            for half in range(2):
                z_ref[half] = z[:, half * LANES:(half + 1) * LANES]
            for r in range(dil):
                rows = [z_ref[half, pl.ds(r, tm // dil, stride=dil), :] for half in range(2)]
                qkv_refs[n][0, r] = jnp.concatenate(rows, axis=1).astype(BF16)


def _even_in(x2, sc, sh, w_bf, tabs, seq):
    t, d = x2.shape
    tm = TOKEN_TILE
    tps = seq // tm
    bsz = t // seq
    n_in = w_bf.shape[1]
    mod_spec = pl.BlockSpec((1, 1, d), lambda i: (i // tps, 0, 0))
    tab_spec = pl.BlockSpec((tm, DIL_GROUP_WIDTH), lambda i: (i % tps, 0))
    out_spec = pl.BlockSpec((tm, DIL_GROUP_WIDTH), lambda i: (i, 0))
    dils = [dil for _, dil in DIL_CONFIGS] * 3
    qkv_shapes = [jax.ShapeDtypeStruct((bsz, dil, seq // dil, DIL_GROUP_WIDTH), BF16) for dil in dils]
    qkv_specs = [pl.BlockSpec((1, dil, tm // dil, DIL_GROUP_WIDTH), lambda i: (i // tps, 0, i % tps, 0))
                 for dil in dils]
    return pl.pallas_call(
        _even_in_kernel,
        out_shape=[jax.ShapeDtypeStruct((t, POOL_WIDTH), F32)] + qkv_shapes,
        grid=(t // tm,),
        in_specs=[
            pl.BlockSpec((tm, d), lambda i: (i, 0)),
            mod_spec,
            mod_spec,
            pl.BlockSpec((d, n_in), lambda i: (0, 0)),
            tab_spec,
            tab_spec,
            tab_spec,
        ],
        out_specs=[out_spec] + qkv_specs,
        scratch_shapes=[pltpu.VMEM((DIL_GROUP_WIDTH // LANES, tm, LANES), F32)],
        compiler_params=_params(("parallel",)),
        name="even_in_proj",
    )(x2, sc, sh, w_bf, *tabs)


def _rope_tables(seq):
    pos = jnp.arange(seq, dtype=F32)
    inv = ROPE_THETA ** (-jnp.arange(0, ROPE_DIM, 2, dtype=F32) / ROPE_DIM)
    ang = pos[:, None] * inv[None, :]
    cos, sin = jnp.cos(ang), jnp.sin(ang)
    half = ROPE_DIM // 2
    zeros = jnp.zeros((seq, half), F32)
    rest = HEAD_DIM - ROPE_DIM
    ct = jnp.concatenate([cos, cos, jnp.ones((seq, rest), F32)], axis=1)
    sa = jnp.concatenate([zeros, sin, jnp.zeros((seq, rest), F32)], axis=1)
    sb = jnp.concatenate([-sin, zeros, jnp.zeros((seq, rest), F32)], axis=1)
    return tuple(jnp.tile(a, (1, DIL_HEADS)) for a in (ct, sa, sb))


def _band_kernel(q_ref, kp_ref, kc_ref, vp_ref, vc_ref, o_ref, lse_ref):
    mt = pl.program_id(2)
    blk = BAND_BLOCK
    nblk = q_ref.shape[2] // blk
    kall = jnp.concatenate([kp_ref[0, 0], kc_ref[0, 0]], axis=0)
    vall = jnp.concatenate([vp_ref[0, 0], vc_ref[0, 0]], axis=0)
    qi = lax.broadcasted_iota(jnp.int32, (blk, 2 * blk), 0)
    kj = lax.broadcasted_iota(jnp.int32, (blk, 2 * blk), 1)
    dist = qi + blk - kj
    in_band = (dist >= 0) & (dist <= blk)
    lane_q = lax.broadcasted_iota(jnp.int32, (blk, DIL_GROUP_WIDTH), 1) // HEAD_DIM
    lane_v = lax.broadcasted_iota(jnp.int32, (2 * blk, DIL_GROUP_WIDTH), 1) // HEAD_DIM
    for i in range(nblk):
        q = q_ref[0, 0, i * blk:(i + 1) * blk, :]
        kk = kall[i * blk:(i + 2) * blk]
        vv = vall[i * blk:(i + 2) * blk]
        if i == 0:
            valid = in_band & (kj >= jnp.where(mt > 0, 0, blk))
        else:
            valid = in_band
        acc = jnp.zeros((blk, DIL_GROUP_WIDTH), F32)
        den_full = jnp.ones((blk, DIL_GROUP_WIDTH), F32)
        lse_full = jnp.zeros((blk, DIL_GROUP_WIDTH), F32)
        for h in range(DIL_HEADS):
            qh = jnp.where(lane_q == h, q, jnp.zeros_like(q))
            s = lax.dot_general(qh, kk, (((1,), (1,)), ((), ())), preferred_element_type=F32)
            s = jnp.where(valid, s, NEG)
            m = jnp.max(s, axis=-1, keepdims=True)
            p = jnp.exp(s - m)
            den = jnp.sum(p, axis=-1, keepdims=True)
            vh = jnp.where(lane_v == h, vv, jnp.zeros_like(vv))
            acc = acc + jnp.dot(p.astype(BF16), vh, preferred_element_type=F32)
            den_full = jnp.where(lane_q == h, den, den_full)
            lse_full = jnp.where(lane_q == h, m + jnp.log(den), lse_full)
        o_ref[0, 0, i * blk:(i + 1) * blk, :] = acc / den_full
        lse_ref[0, 0, i * blk:(i + 1) * blk, :] = lse_full


def _band_attention(q, k, v):
    bsz, dil, ln, w = q.shape
    nblk = min(BAND_BLOCKS_PER_STEP, ln // BAND_BLOCK)
    rows = nblk * BAND_BLOCK
    cur = pl.BlockSpec((1, 1, rows, w), lambda b, r, m: (b, r, m, 0))
    prev = pl.BlockSpec((1, 1, BAND_BLOCK, w), lambda b, r, m: (b, r, jnp.maximum(m * nblk - 1, 0), 0))
    return pl.pallas_call(
        _band_kernel,
        out_shape=[jax.ShapeDtypeStruct(q.shape, F32)] * 2,
        grid=(bsz, dil, ln // rows),
        in_specs=[cur, prev, cur, prev, cur],
        out_specs=[cur, cur],
        compiler_params=_params(("parallel", "parallel", "parallel")),
        name=f"band_attn_d{dil}",
    )(q, k, k, v, v)


def _even_out_kernel(tps, u_ref, up_ref, o0, o1, o2, l0, l1, l2, wp_ref, ps_ref, wo_ref, x_ref, gate_ref,
                     g_ref, b_ref, out_ref, il_ref):
    i = pl.program_id(0)
    tm = u_ref.shape[0]
    halo = up_ref.shape[0]
    u = u_ref[...]
    prev = jnp.where(i % tps == 0, jnp.zeros_like(up_ref[...]), up_ref[...])
    ext = jnp.concatenate([prev, u], axis=0)
    lane_g = lax.broadcasted_iota(jnp.int32, (tm, POOL_WIDTH), 1) // POOL_GROUP_DIM
    row = lax.broadcasted_iota(jnp.int32, (tm, POOL_WIDTH), 0)
    pos1 = ((i % tps) * tm + row + 1).astype(F32)
    s = ext
    pooled = jnp.zeros((tm, POOL_WIDTH), F32)
    wvec = jnp.zeros((tm, POOL_WIDTH), F32)
    shift = 1
    for g, w in enumerate(POOL_WINDOWS):
        while shift < w:
            s = s + pltpu.roll(s, shift, 0)
            shift *= 2
        pooled = jnp.where(lane_g == g, s[halo:, :], pooled)
        wvec = jnp.where(lane_g == g, float(w), wvec)
    pooled = pooled / jnp.minimum(pos1, wvec) - u
    a_out = jnp.dot(pooled.astype(BF16), wp_ref[...], preferred_element_type=F32) * ps_ref[...]
    def token_order(ref, buf):
        dil = ref.shape[1]
        if dil == 1:
            return ref[0, 0]
        for r in range(dil):
            blk = ref[0, r]
            for half in range(2):
                buf[half, pl.ds(r, tm // dil, stride=dil), :] = blk[:, half * LANES:(half + 1) * LANES]
        return jnp.concatenate([buf[0], buf[1]], axis=1)

    la, lb, lc = [token_order(ref, il_ref.at[n]) for n, ref in enumerate((l0, l1, l2))]
    oa, ob, oc = [token_order(ref, il_ref.at[3 + n]) for n, ref in enumerate((o0, o1, o2))]
    mx = jnp.maximum(jnp.maximum(la, lb), lc)
    ea, eb, ec = jnp.exp(la - mx), jnp.exp(lb - mx), jnp.exp(lc - mx)
    b_out = (ea * oa + eb * ob + ec * oc) / (ea + eb + ec)
    mixed = jnp.concatenate([a_out, b_out], axis=-1).astype(BF16)
    sub = jnp.dot(mixed, wo_ref[...], preferred_element_type=F32)
    out_ref[...] = _deepnorm_ln(x_ref[...], sub, gate_ref[0], g_ref[...], b_ref[...])


def _even_out(u, os_, lses, wpool_bd, pool_scale, wo_bf, x2, gate, ln_g, ln_b, seq):
    t, d = x2.shape
    tm = TOKEN_TILE
    tps = seq // tm
    halo = 16
    row = lambda w: pl.BlockSpec((tm, w), lambda i: (i, 0))
    full = lambda a: pl.BlockSpec(a.shape, lambda i: (0,) * a.ndim)
    strided = [pl.BlockSpec((1, a.shape[1], tm // a.shape[1], DIL_GROUP_WIDTH), lambda i: (i // tps, 0, i % tps, 0))
               for a in (*os_, *lses)]
    return pl.pallas_call(
        functools.partial(_even_out_kernel, tps),
        out_shape=jax.ShapeDtypeStruct((t, d), F32),
        grid=(t // tm,),
        in_specs=[
            row(POOL_WIDTH),
            pl.BlockSpec((halo, POOL_WIDTH), lambda i: (jnp.maximum(i * (tm // halo) - 1, 0), 0)),
            *strided,
            full(wpool_bd),
            full(pool_scale),
            full(wo_bf),
            row(d),
            pl.BlockSpec((1, 1, d), lambda i: (i // tps, 0, 0)),
            full(ln_g),
            full(ln_b),
        ],
        out_specs=row(d),
        scratch_shapes=[pltpu.VMEM((6, DIL_GROUP_WIDTH // LANES, tm, LANES), F32)],
        compiler_params=_params(("parallel",)),
        name="even_out_proj",
    )(u, u, *os_, *lses, wpool_bd, pool_scale, wo_bf, x2, gate, ln_g, ln_b)


def _swiglu_rows(x, wgu_ref, wd_ref, acc_ref):
    for c in range(D_FF // FF_SUB):
        lo = c * FF_SUB
        gt = jnp.dot(x, wgu_ref[0, :, lo:lo + FF_SUB], preferred_element_type=F32)
        up = jnp.dot(x, wgu_ref[0, :, D_FF + lo:D_FF + lo + FF_SUB], preferred_element_type=F32)
        a = (gt * jax.nn.sigmoid(gt) * up).astype(BF16)
        part = jnp.dot(a, wd_ref[0, lo:lo + FF_SUB, :], preferred_element_type=F32)
        if c == 0:
            acc_ref[...] = part
        else:
            acc_ref[...] += part


def _ffn_kernel(x_ref, sc_ref, sh_ref, wgu_ref, wd_ref, gate_ref, g_ref, b_ref, out_ref, acc_ref):
    _swiglu_rows(_modulate(x_ref[...], sc_ref[0], sh_ref[0]), wgu_ref, wd_ref, acc_ref)
    out_ref[...] = _deepnorm_ln(x_ref[...], acc_ref[...], gate_ref[0], g_ref[...], b_ref[...])


def _ffn(x2, sc, sh, wgu_bf, wd_bf, gate, ln_g, ln_b, seq):
    t, d = x2.shape
    tm = TOKEN_TILE
    tps = seq // tm
    mod = pl.BlockSpec((1, 1, d), lambda i: (i // tps, 0, 0))
    row = pl.BlockSpec((tm, d), lambda i: (i, 0))
    vec = pl.BlockSpec((1, d), lambda i: (0, 0))
    full = lambda a: pl.BlockSpec(a.shape, lambda i: (0,) * a.ndim)
    return pl.pallas_call(
        _ffn_kernel,
        out_shape=jax.ShapeDtypeStruct((t, d), F32),
        grid=(t // tm,),
        in_specs=[row, mod, mod, full(wgu_bf), full(wd_bf), mod, vec, vec],
        out_specs=row,
        scratch_shapes=[pltpu.VMEM((tm, d), F32)],
        compiler_params=_params(("parallel",)),
        name="ffn_dense",
    )(x2, sc, sh, wgu_bf, wd_bf, gate, ln_g, ln_b)


def _split3(x):
    hi = x.astype(BF16)
    r = x - hi.astype(F32)
    mid = r.astype(BF16)
    lo = (r - mid.astype(F32)).astype(BF16)
    return hi, mid, lo


def _odd_in_kernel(tps, x_ref, sc_ref, sh_ref, w_ref, wf_ref, bf_ref, place_ref, q_ref, k_ref, v_ref, f_ref,
                   carry_ref):
    i = pl.program_id(0)
    tm = x_ref.shape[0]
    h = _modulate(x_ref[...], sc_ref[0], sh_ref[0])
    q = jnp.dot(h, w_ref[:, 0:FOX_WIDTH], preferred_element_type=F32)
    q_ref[...] = (q * (HEAD_DIM ** -0.5 * LOG2E)).astype(BF16)
    k_ref[...] = jnp.dot(h, w_ref[:, FOX_WIDTH:2 * FOX_WIDTH], preferred_element_type=F32).astype(BF16)
    v_ref[...] = jnp.dot(h, w_ref[:, 2 * FOX_WIDTH:3 * FOX_WIDTH], preferred_element_type=F32).astype(BF16)
    f = jnp.dot(h, wf_ref[...], preferred_element_type=F32) + bf_ref[...]
    logf = jnp.minimum(f, 0.0) - jnp.log(1.0 + jnp.exp(-jnp.abs(f)))
    r = lax.broadcasted_iota(jnp.int32, (tm, tm), 0)
    cidx = lax.broadcasted_iota(jnp.int32, (tm, tm), 1)
    tri = jnp.where(cidx <= r, 1.0, 0.0).astype(BF16)
    hi, mid, lo = _split3(logf)
    cs = (jnp.dot(tri, lo, preferred_element_type=F32) + jnp.dot(tri, mid, preferred_element_type=F32)
          + jnp.dot(tri, hi, preferred_element_type=F32))

    @pl.when(i % tps == 0)
    def _():
        carry_ref[...] = jnp.zeros_like(carry_ref)

    cs = cs + carry_ref[...]
    carry_ref[...] = cs[tm - 1:tm, :]
    parts = jnp.concatenate(_split3(cs * LOG2E), axis=1)
    f_ref[...] = jnp.dot(parts, place_ref[...], preferred_element_type=F32).astype(BF16)


def _gate_placement():
    place = np.zeros((3 * LANES, FOX_WIDTH), np.float32)
    for h in range(FOX_HEADS):
        base = (h // 2) * LANES + (HEAD_DIM if h % 2 == 0 else 0)
        for p in range(3):
            place[p * LANES + h, base + p] = 1.0
    return jnp.asarray(place, BF16)


def _odd_in(x2, sc, sh, w_bf, wf_bf, bf_pad, seq):
    t, d = x2.shape
    tm = TOKEN_TILE
    tps = seq // tm
    mod = pl.BlockSpec((1, 1, d), lambda i: (i // tps, 0, 0))
    row = lambda w: pl.BlockSpec((tm, w), lambda i: (i, 0))
    full = lambda a: pl.BlockSpec(a.shape, lambda i: (0,) * a.ndim)
    place = _gate_placement()
    return pl.pallas_call(
        functools.partial(_odd_in_kernel, tps),
        out_shape=[jax.ShapeDtypeStruct((t, FOX_WIDTH), BF16)] * 4,
        grid=(t // tm,),
        in_specs=[row(d), mod, mod, full(w_bf), full(wf_bf), full(bf_pad), full(place)],
        out_specs=[row(FOX_WIDTH)] * 4,
        scratch_shapes=[pltpu.VMEM((1, LANES), F32)],
        compiler_params=_params(("arbitrary",)),
        name="odd_in_proj",
    )(x2, sc, sh, w_bf, wf_bf, bf_pad, place)


def _fox_kernel(q_ref, k_ref, v_ref, f_ref, o_ref, acc_ref, m_ref, alpha_ref, s0_ref, s1_ref, p0_ref, p1_ref):
    qi = pl.program_id(2)
    tq = q_ref.shape[1]
    tk = FOX_K_TILE
    half = HEAD_DIM
    q = q_ref[0]
    lane_q = lax.broadcasted_iota(jnp.int32, q.shape, 1)
    lane_k = lax.broadcasted_iota(jnp.int32, (tk, LANES), 1)
    qa = jnp.where(lane_q < half, q, jnp.where(lane_q < half + 3, -1.0, 0.0).astype(BF16))
    qb = jnp.where(lane_q >= half, q, jnp.where(lane_q < 3, -1.0, 0.0).astype(BF16))
    ones_a = jnp.where(lane_k == half, 1.0, 0.0).astype(BF16)
    ones_b = jnp.where(lane_k == 0, 1.0, 0.0).astype(BF16)
    acc_ref[...] = jnp.zeros_like(acc_ref)
    m_ref[...] = jnp.full(m_ref.shape, NEG, F32)

    def scores(j, s_ref, row0=0):
        start = pl.multiple_of(j * tk, tk)
        k = k_ref[0, pl.ds(start, tk), :]
        fa = f_ref[0, pl.ds(start, tk), :]
        ka = jnp.where(lane_k < half, k, fa)
        kb = jnp.where(lane_k >= half, k, fa)
        for hh, (qh, kh) in enumerate(((qa, ka), (qb, kb))):
            s_ref[hh, row0:, :] = lax.dot_general(qh[row0:], kh, (((1,), (1,)), ((), ())),
                                                  preferred_element_type=F32)

    def softmax(j, s_ref, p_ref, row0=0, masked_rows=0):
        start = j * tk
        for hh in range(2):
            for r0 in range(row0, tq, FOX_ROW_CHUNK):
                rows = pl.ds(r0, FOX_ROW_CHUNK)
                s = s_ref[hh, rows, :]
                if r0 < row0 + masked_rows:
                    rr = lax.broadcasted_iota(jnp.int32, s.shape, 0) + (qi * tq + r0)
                    cc = lax.broadcasted_iota(jnp.int32, s.shape, 1) + start
                    s = jnp.where(cc <= rr, s, NEG)
                m_old = m_ref[hh, rows, :]
                m_new = jnp.maximum(m_old, jnp.max(s, axis=-1, keepdims=True))
                p_ref[hh, rows, :] = jnp.exp2(s - jnp.tile(m_new, (1, tk // LANES))).astype(BF16)
                alpha_ref[hh, rows, :] = jnp.exp2(m_old - m_new)
                m_ref[hh, rows, :] = m_new

    def values(j, p_ref, row0=0):
        start = pl.multiple_of(jnp.maximum(j, 0) * tk, tk)
        v = v_ref[0, pl.ds(start, tk), :]
        va = jnp.where(lane_k < half, v, ones_a)
        vb = jnp.where(lane_k >= half, v, ones_b)
        return [jnp.dot(p_ref[hh, row0:, :], vh, preferred_element_type=F32) for hh, vh in enumerate((va, vb))]

    def fold(pv, scale_row0=0):
        for hh in range(2):
            if scale_row0:
                acc_ref[hh, :scale_row0, :] += pv[hh][:scale_row0]
            acc_ref[hh, scale_row0:, :] = ((acc_ref[hh, scale_row0:, :] + pv[hh][scale_row0:])
                                           * alpha_ref[hh, scale_row0:, :])

    p1_ref[...] = jnp.zeros_like(p1_ref)
    scores(0, s0_ref)

    def body(jp, carry):
        pv = values(2 * jp - 1, p1_ref)
        scores(2 * jp + 1, s1_ref)
        softmax(2 * jp, s0_ref, p0_ref)
        fold(pv)
        pv = values(2 * jp, p0_ref)
        scores(2 * jp + 2, s0_ref)
        softmax(2 * jp + 1, s1_ref, p1_ref)
        fold(pv)
        return carry

    lax.fori_loop(0, qi, body, 0)
    pv = values(2 * qi - 1, p1_ref)
    scores(2 * qi + 1, s1_ref, row0=tk)
    softmax(2 * qi, s0_ref, p0_ref, masked_rows=tk)
    fold(pv)
    pv = values(2 * qi, p0_ref)
    softmax(2 * qi + 1, s1_ref, p1_ref, row0=tk, masked_rows=tk)
    fold(pv, scale_row0=tk)
    pv = values(2 * qi + 1, p1_ref, row0=tk)
    for hh in range(2):
        acc_ref[hh, tk:, :] += pv[hh]

    acc_a, acc_b = acc_ref[0], acc_ref[1]
    lane_o = lax.broadcasted_iota(jnp.int32, (tq, LANES), 1)
    o = jnp.where(lane_o < half, acc_a / acc_a[:, half:half + 1], acc_b / acc_b[:, 0:1])
    o_ref[0] = o.astype(BF16)


def _fox_attention(q, k, v, fa, bsz, seq):
    tq = FOX_Q_TILE
    npairs = FOX_HEADS // 2
    tile = pl.BlockSpec((1, tq, LANES), lambda b, j, i: (b, i, j))
    whole = pl.BlockSpec((1, seq, LANES), lambda b, j, i: (b, 0, j))
    return pl.pallas_call(
        _fox_kernel,
        out_shape=jax.ShapeDtypeStruct((bsz, seq, FOX_WIDTH), BF16),
        grid=(bsz, npairs, seq // tq),
        in_specs=[tile, whole, whole, whole],
        out_specs=tile,
        scratch_shapes=[pltpu.VMEM((2, tq, LANES), F32), pltpu.VMEM((2, tq, LANES), F32),
                        pltpu.VMEM((2, tq, LANES), F32),
                        pltpu.VMEM((2, tq, FOX_K_TILE), F32), pltpu.VMEM((2, tq, FOX_K_TILE), F32),
                        pltpu.VMEM((2, tq, FOX_K_TILE), BF16), pltpu.VMEM((2, tq, FOX_K_TILE), BF16)],
        compiler_params=_params(("parallel", "parallel", "arbitrary")),
        name="fox_attn",
    )(q, k, v, fa)


def _odd_out_kernel(o_ref, wo_ref, x_ref, gate_ref, g_ref, b_ref, out_ref):
    sub = jnp.dot(o_ref[...], wo_ref[...], preferred_element_type=F32)
    out_ref[...] = _deepnorm_ln(x_ref[...], sub, gate_ref[0], g_ref[...], b_ref[...])


def _odd_out(o2, wo_bf, x2, gate, ln_g, ln_b, seq):
    t, d = x2.shape
    tm = TOKEN_TILE
    tps = seq // tm
    row = lambda w: pl.BlockSpec((tm, w), lambda i: (i, 0))
    full = lambda a: pl.BlockSpec(a.shape, lambda i: (0,) * a.ndim)
    return pl.pallas_call(
        _odd_out_kernel,
        out_shape=jax.ShapeDtypeStruct((t, d), F32),
        grid=(t // tm,),
        in_specs=[row(FOX_WIDTH), full(wo_bf), row(d), pl.BlockSpec((1, 1, d), lambda i: (i // tps, 0, 0)),
                  full(ln_g), full(ln_b)],
        out_specs=row(d),
        compiler_params=_params(("parallel",)),
        name="odd_out_proj",
    )(o2, wo_bf, x2, gate, ln_g, ln_b)


def _router_kernel(x_ref, sc_ref, sh_ref, wrt_ref, tri_ref, rankt_ref, cnt_ref, comb_ref, rank_ref):
    c = x_ref.shape[0]
    h = _modulate(x_ref[...], sc_ref[0], sh_ref[0])
    lt = lax.dot_general(wrt_ref[...], h, (((1,), (1,)), ((), ())), preferred_element_type=F32)[0:N_EXPERTS]
    row = lax.broadcasted_iota(jnp.int32, lt.shape, 0)
    m1 = jnp.max(lt, axis=0, keepdims=True)
    i1 = jnp.min(jnp.where(lt == m1, row, N_EXPERTS), axis=0, keepdims=True)
    rest = jnp.where(row == i1, -jnp.inf, lt)
    m2 = jnp.max(rest, axis=0, keepdims=True)
    i2 = jnp.min(jnp.where(rest == m2, row, N_EXPERTS), axis=0, keepdims=True)
    e2 = jnp.exp(m2 - m1)
    g1 = 1.0 / (1.0 + e2)
    g2 = e2 / (1.0 + e2)
    combt = jnp.where(row == i1, g1, jnp.where(row == i2, g2, 0.0))
    sel = jnp.where((row == i1) | (row == i2), 1.0, 0.0)
    sel16 = jnp.concatenate([sel, jnp.zeros_like(sel)], axis=0).astype(BF16)
    earlier = jnp.dot(sel16, tri_ref[...], preferred_element_type=F32)[0:N_EXPERTS]
    rankt = jnp.where(sel > 0.0, earlier, NO_ROW)
    rankt_ref[...] = rankt
    cnt_ref[0] = jnp.broadcast_to(jnp.sum(sel, axis=1, keepdims=True), (N_EXPERTS, LANES))
    pad = LANES - N_EXPERTS
    comb_ref[...] = jnp.concatenate([combt, jnp.zeros((pad, c), F32)], axis=0).T
    rank_ref[...] = jnp.concatenate([rankt, jnp.full((pad, c), NO_ROW, F32)], axis=0).T


def _router(x2, sc, sh, wrt, seq):
    t, d = x2.shape
    cs = MOE_CHUNK
    cps = seq // cs
    mod = pl.BlockSpec((1, 1, d), lambda i: (i // cps, 0, 0))
    tri = jnp.asarray(np.triu(np.ones((cs, cs), np.float32), 1), BF16)
    return pl.pallas_call(
        _router_kernel,
        out_shape=[jax.ShapeDtypeStruct((N_EXPERTS, t), F32),
                   jax.ShapeDtypeStruct((t // cs, N_EXPERTS, LANES), F32),
                   jax.ShapeDtypeStruct((t, LANES), F32),
                   jax.ShapeDtypeStruct((t, LANES), F32)],
        grid=(t // cs,),
        in_specs=[pl.BlockSpec((cs, d), lambda i: (i, 0)), mod, mod,
                  pl.BlockSpec(wrt.shape, lambda i: (0, 0)), pl.BlockSpec(tri.shape, lambda i: (0, 0))],
        out_specs=[pl.BlockSpec((N_EXPERTS, cs), lambda i: (0, i)),
                   pl.BlockSpec((1, N_EXPERTS, LANES), lambda i: (i, 0, 0)),
                   pl.BlockSpec((cs, LANES), lambda i: (i, 0)),
                   pl.BlockSpec((cs, LANES), lambda i: (i, 0))],
        compiler_params=_params(("parallel",)),
        name="router",
    )(x2, sc, sh, wrt, tri)


def _route_meta(cnt, n_tiles):
    before = jnp.cumsum(cnt, axis=0) - cnt
    total = jnp.sum(cnt, axis=0)
    region = ((total + MOE_TILE - 1) // MOE_TILE) * MOE_TILE
    base = jnp.cumsum(region) - region
    n_used = (jnp.sum(region) // MOE_TILE).reshape(1)
    starts = jnp.arange(n_tiles, dtype=jnp.int32) * MOE_TILE
    tile_expert = jnp.minimum(jnp.sum(starts[:, None] >= (base + region)[None, :], axis=1), N_EXPERTS - 1)
    i32 = lambda a: a.astype(jnp.int32)
    return i32(before.reshape(-1)), i32(cnt.reshape(-1)), i32(base), i32(total), i32(tile_expert), i32(n_used)


def _chunk_block(before_ref, cnt_ref, chunk, e, j):
    g = before_ref[chunk * N_EXPERTS + e]
    n = cnt_ref[chunk * N_EXPERTS + e]
    k = g // ROW_BLOCK + j
    return g, n, k, (n > 0) & (k * ROW_BLOCK < g + n)


def _dispatch_kernel(before_ref, cnt_ref, base_ref, total_ref, x_ref, sc_ref, sh_ref, rankt_ref, xs_ref,
                     stage_ref, slot_ref, zero_ref, sem, fsem, zsem, usem):
    c = pl.program_id(0)
    last = c == pl.num_programs(0) - 1
    h = _modulate(x_ref[...], sc_ref[0], sh_ref[0])

    @pl.when(c == 0)
    def _():
        stage_ref[...] = jnp.zeros_like(stage_ref)
        zero_ref[...] = jnp.zeros_like(zero_ref)

    block_row = lax.broadcasted_iota(jnp.int32, (ROW_BLOCK, MOE_CHUNK), 0).astype(F32)
    pair_row = lax.broadcasted_iota(jnp.int32, (2 * ROW_BLOCK, MOE_CHUNK), 0).astype(F32)

    def block_copy(e, j, k):
        row0 = pl.multiple_of(base_ref[e] + k * ROW_BLOCK, ROW_BLOCK)
        return pltpu.make_async_copy(slot_ref.at[e, j], xs_ref.at[pl.ds(row0, ROW_BLOCK)], sem.at[e, j])

    def any_beyond_two(chunk):
        flag = _chunk_block(before_ref, cnt_ref, chunk, 0, 2)[3]
        for e in range(1, N_EXPERTS):
            flag = flag | _chunk_block(before_ref, cnt_ref, chunk, e, 2)[3]
        return flag

    def for_blocks(chunk, fn):
        for e in range(N_EXPERTS):
            for j in range(2):
                fn(e, j)

        @pl.when(any_beyond_two(chunk))
        def _():
            for e in range(N_EXPERTS):
                for j in range(2, MAX_SPAN):
                    fn(e, j)

    def wait_block_copies(chunk):
        def wait_one(e, j):
            g, n, k, touched = _chunk_block(before_ref, cnt_ref, chunk, e, j)

            @pl.when(touched & (g + n >= (k + 1) * ROW_BLOCK))
            def _():
                block_copy(e, j, k).wait()

        for_blocks(chunk, wait_one)

    selectors = []
    for e in range(N_EXPERTS):
        g = before_ref[c * N_EXPERTS + e]
        off = (g % ROW_BLOCK).astype(F32)
        selectors.append(jnp.where(rankt_ref[e:e + 1, :] + off == pair_row, 1.0, 0.0).astype(BF16))
    first_two = jnp.dot(jnp.concatenate(selectors, axis=0), h, preferred_element_type=F32).astype(BF16)

    @pl.when(c > 0)
    def _():
        wait_block_copies(c - 1)

    def emit_block(e, j):
        g, n, k, touched = _chunk_block(before_ref, cnt_ref, c, e, j)
        complete = g + n >= (k + 1) * ROW_BLOCK

        @pl.when(touched)
        def _():
            if j < 2:
                r0 = (2 * e + j) * ROW_BLOCK
                blk = first_two[r0:r0 + ROW_BLOCK]
            else:
                off = (g - k * ROW_BLOCK).astype(F32)
                onehot = jnp.where(rankt_ref[e:e + 1, :] + off == block_row, 1.0, 0.0).astype(BF16)
                blk = jnp.dot(onehot, h, preferred_element_type=F32).astype(BF16)
            if j == 0:
                blk = (blk.astype(F32) + stage_ref[e].astype(F32)).astype(BF16)
            slot_ref[e, j] = blk
            stage_ref[e] = jnp.where(complete, jnp.zeros_like(blk), blk)

            @pl.when(complete)
            def _():
                block_copy(e, j, k).start()

    for_blocks(c, emit_block)

    def tail_copies(e):
        total = total_ref[e]
        n_data = (total + ROW_BLOCK - 1) // ROW_BLOCK
        n_region = ((total + MOE_TILE - 1) // MOE_TILE) * (MOE_TILE // ROW_BLOCK)
        part_row = pl.multiple_of(base_ref[e] + (total // ROW_BLOCK) * ROW_BLOCK, ROW_BLOCK)
        part = pltpu.make_async_copy(stage_ref.at[e], xs_ref.at[pl.ds(part_row, ROW_BLOCK)], fsem.at[e])
        out = [(total % ROW_BLOCK != 0, part)]
        for z in range(MOE_TILE // ROW_BLOCK - 1):
            zrow = pl.multiple_of(base_ref[e] + (n_data + z) * ROW_BLOCK, ROW_BLOCK)
            zc = pltpu.make_async_copy(zero_ref, xs_ref.at[pl.ds(zrow, ROW_BLOCK)], zsem.at[e, z])
            out.append((n_data + z < n_region, zc))
        return out

    def unused_copies():
        e = N_EXPERTS - 1
        used = base_ref[e] + ((total_ref[e] + MOE_TILE - 1) // MOE_TILE) * MOE_TILE
        out = []
        for z in range(usem.shape[0]):
            zrow = pl.multiple_of(used + z * ROW_BLOCK, ROW_BLOCK)
            zc = pltpu.make_async_copy(zero_ref, xs_ref.at[pl.ds(zrow, ROW_BLOCK)], usem.at[z])
            out.append((zrow < xs_ref.shape[0], zc))
        return out

    def all_tail_copies():
        return [cc for e in range(N_EXPERTS) for cc in tail_copies(e)] + unused_copies()

    @pl.when(last)
    def _():
        for cond, cp in all_tail_copies():
            @pl.when(cond)
            def _():
                cp.start()

    @pl.when(last)
    def _():
        wait_block_copies(c)

    @pl.when(last)
    def _():
        for cond, cp in all_tail_copies():
            @pl.when(cond)
            def _():
                cp.wait()


def _dispatch(meta, x2, sc, sh, rankt, n_tiles, seq):
    before, cnt, base, total = meta
    t, d = x2.shape
    cs = MOE_CHUNK
    cps = seq // cs
    mod = pl.BlockSpec((1, 1, d), lambda i, *_: (i // cps, 0, 0))
    grid_spec = pltpu.PrefetchScalarGridSpec(
        num_scalar_prefetch=4,
        grid=(t // cs,),
        in_specs=[pl.BlockSpec((cs, d), lambda i, *_: (i, 0)), mod, mod,
                  pl.BlockSpec((N_EXPERTS, cs), lambda i, *_: (0, i))],
        out_specs=pl.BlockSpec(memory_space=pl.ANY),
        scratch_shapes=[pltpu.VMEM((N_EXPERTS, ROW_BLOCK, d), BF16),
                        pltpu.VMEM((N_EXPERTS, MAX_SPAN, ROW_BLOCK, d), BF16),
                        pltpu.VMEM((ROW_BLOCK, d), BF16),
                        pltpu.SemaphoreType.DMA((N_EXPERTS, MAX_SPAN)),
                        pltpu.SemaphoreType.DMA((N_EXPERTS,)),
                        pltpu.SemaphoreType.DMA((N_EXPERTS, MOE_TILE // ROW_BLOCK - 1)),
                        pltpu.SemaphoreType.DMA(((n_tiles * MOE_TILE - 2 * t) // ROW_BLOCK,))],
    )
    return pl.pallas_call(
        _dispatch_kernel,
        out_shape=jax.ShapeDtypeStruct((n_tiles * MOE_TILE, d), BF16),
        grid_spec=grid_spec,
        compiler_params=_params(("arbitrary",)),
        name="moe_dispatch",
    )(before, cnt, base, total, x2, sc, sh, rankt)


def _experts_kernel(te_ref, nu_ref, x_ref, wgu_ref, wd_ref, y_ref, acc_ref):
    @pl.when(pl.program_id(0) >= nu_ref[0])
    def _():
        y_ref[...] = jnp.zeros_like(y_ref)

    @pl.when(pl.program_id(0) < nu_ref[0])
    def _():
        _swiglu_rows(x_ref[...], wgu_ref, wd_ref, acc_ref)
        y_ref[...] = acc_ref[...].astype(BF16)


def _experts(tile_expert, n_used, xs, wgu_bf, wd_bf):
    rows, d = xs.shape
    tm = MOE_TILE
    clamp = lambda i, nu: jnp.minimum(i, nu[0] - 1)
    grid_spec = pltpu.PrefetchScalarGridSpec(
        num_scalar_prefetch=2,
        grid=(rows // tm,),
        in_specs=[pl.BlockSpec((tm, d), lambda i, te, nu: (clamp(i, nu), 0)),
                  pl.BlockSpec((1, d, 2 * D_FF), lambda i, te, nu: (te[clamp(i, nu)], 0, 0)),
                  pl.BlockSpec((1, D_FF, d), lambda i, te, nu: (te[clamp(i, nu)], 0, 0))],
        out_specs=pl.BlockSpec((tm, d), lambda i, te, nu: (i, 0)),
        scratch_shapes=[pltpu.VMEM((tm, d), F32)],
    )
    return pl.pallas_call(
        _experts_kernel,
        out_shape=jax.ShapeDtypeStruct((rows, d), BF16),
        grid_spec=grid_spec,
        compiler_params=_params(("arbitrary",)),
        name="moe_experts",
    )(tile_expert, n_used, xs, wgu_bf, wd_bf)


def _combine_kernel(before_ref, cnt_ref, base_ref, rank_ref, comb_ref, x_ref, gate_ref, g_ref, b_ref, y_ref,
                    out_ref, ybuf, xbuf, acc_ref, sem, xsem):
    c = pl.program_id(0)
    slot = c % 2

    def window(chunk, e, w):
        g = before_ref[chunk * N_EXPERTS + e]
        n = cnt_ref[chunk * N_EXPERTS + e]
        first = (g // ROW_BLOCK) * ROW_BLOCK + w * COMBINE_WIN
        row0 = pl.multiple_of(jnp.minimum(base_ref[e] + first, y_ref.shape[0] - COMBINE_WIN), ROW_BLOCK)
        off = (base_ref[e] + g - row0).astype(F32)
        return (n > 0) & (first < g + n), row0, off

    def first_copy(chunk, sl, e):
        touched, row0, _ = window(chunk, e, 0)
        return touched, pltpu.make_async_copy(y_ref.at[pl.ds(row0, COMBINE_WIN)], ybuf.at[sl, e], sem.at[sl, e])

    def later_copy(chunk, sl, e, w):
        touched, row0, _ = window(chunk, e, w)
        return touched, pltpu.make_async_copy(y_ref.at[pl.ds(row0, COMBINE_WIN)], xbuf.at[sl, e, w - 1],
                                              xsem.at[sl, e, w - 1])

    def any_later(chunk):
        flag = window(chunk, 0, 1)[0]
        for e in range(1, N_EXPERTS):
            flag = flag | window(chunk, e, 1)[0]
        return flag

    def fetch(chunk, sl):
        for e in range(N_EXPERTS):
            touched, cp = first_copy(chunk, sl, e)

            @pl.when(touched)
            def _():
                cp.start()

        @pl.when(any_later(chunk))
        def _():
            for e in range(N_EXPERTS):
                for w in range(1, COMBINE_MAX_WIN):
                    touched, cp = later_copy(chunk, sl, e, w)

                    @pl.when(touched)
                    def _():
                        cp.start()

    @pl.when(c == 0)
    def _():
        ybuf[...] = jnp.zeros_like(ybuf)
        fetch(0, 0)

    @pl.when(c + 1 < pl.num_programs(0))
    def _():
        fetch(c + 1, 1 - slot)

    lane = lax.broadcasted_iota(jnp.int32, (MOE_CHUNK, COMBINE_WIN), 1).astype(F32)
    acc_ref[...] = jnp.zeros_like(acc_ref)
    for e in range(N_EXPERTS):
        col = rank_ref[:, e:e + 1]
        weight = comb_ref[:, e:e + 1]
        touched, _, off = window(c, e, 0)

        @pl.when(touched)
        def _():
            first_copy(c, slot, e)[1].wait()

        onehot = jnp.where(col + off == lane, 1.0, 0.0).astype(BF16)
        acc_ref[...] += weight * jnp.dot(onehot, ybuf[slot, e], preferred_element_type=F32)

    @pl.when(any_later(c))
    def _():
        for e in range(N_EXPERTS):
            for w in range(1, COMBINE_MAX_WIN):
                touched, _, off = window(c, e, w)

                @pl.when(touched)
                def _():
                    later_copy(c, slot, e, w)[1].wait()
                    onehot = jnp.where(rank_ref[:, e:e + 1] + off == lane, 1.0, 0.0).astype(BF16)
                    acc_ref[...] += comb_ref[:, e:e + 1] * jnp.dot(onehot, xbuf[slot, e, w - 1],
                                                                   preferred_element_type=F32)

    out_ref[...] = _deepnorm_ln(x_ref[...], acc_ref[...], gate_ref[0], g_ref[...], b_ref[...])


def _combine(meta, rank, comb, x2, gate, ln_g, ln_b, y, seq):
    before, cnt, base, _ = meta
    t, d = x2.shape
    cs = MOE_CHUNK
    cps = seq // cs
    row = lambda w: pl.BlockSpec((cs, w), lambda i, *_: (i, 0))
    vec = pl.BlockSpec((1, d), lambda i, *_: (0, 0))
    grid_spec = pltpu.PrefetchScalarGridSpec(
        num_scalar_prefetch=3,
        grid=(t // cs,),
        in_specs=[row(LANES), row(LANES), row(d), pl.BlockSpec((1, 1, d), lambda i, *_: (i // cps, 0, 0)),
                  vec, vec, pl.BlockSpec(memory_space=pl.ANY)],
        out_specs=row(d),
        scratch_shapes=[pltpu.VMEM((2, N_EXPERTS, COMBINE_WIN, d), BF16),
                        pltpu.VMEM((2, N_EXPERTS, COMBINE_MAX_WIN - 1, COMBINE_WIN, d), BF16),
                        pltpu.VMEM((cs, d), F32),
                        pltpu.SemaphoreType.DMA((2, N_EXPERTS)),
                        pltpu.SemaphoreType.DMA((2, N_EXPERTS, COMBINE_MAX_WIN - 1))],
    )
    return pl.pallas_call(
        _combine_kernel,
        out_shape=jax.ShapeDtypeStruct((t, d), F32),
        grid_spec=grid_spec,
        compiler_params=_params(("arbitrary",)),
        name="moe_combine",
    )(before, cnt, base, rank, comb, x2, gate, ln_g, ln_b, y)


def _block_diag(w_pool):
    g, c, _ = w_pool.shape
    out = jnp.zeros((g * c, g * c), w_pool.dtype)
    for i in range(g):
        out = out.at[i * c:(i + 1) * c, i * c:(i + 1) * c].set(w_pool[i])
    return out


def kernel(x, c, w_ada, b_ada, ln_g, ln_b, w_in_even, w_pool, pool_scale, w_out_even, w_ffn_gu, w_ffn_down,
           w_in_odd, b_forget, w_out_odd, w_router, w_exp_gu, w_exp_down):
    bsz, seq, d = x.shape
    t = bsz * seq
    tabs = _rope_tables(seq)
    ada = _ada(c, w_ada, b_ada)
    x2 = x.reshape(t, d)
    for l in range(DEPTH):
        i = l // 2
        sh1, sc1, g1, sh2, sc2, g2 = [ada[l, :, n * d:(n + 1) * d].reshape(bsz, 1, d) for n in range(6)]
        lng = ln_g[l].reshape(2, 1, d)
        lnb = ln_b[l].reshape(2, 1, d)
        if l % 2 == 0:
            outs = _even_in(x2, sc1, sh1, w_in_even[i].astype(BF16), tabs, seq)
            u, qkv = outs[0], outs[1:]
            os_, lses = [], []
            for g in range(len(DIL_CONFIGS)):
                o, lse = _band_attention(qkv[g], qkv[3 + g], qkv[6 + g])
                os_.append(o)
                lses.append(lse)
            x2 = _even_out(u, os_, lses, _block_diag(w_pool[i]).astype(BF16), pool_scale[i].reshape(1, -1),
                           w_out_even[i].astype(BF16), x2, g1, lng[0], lnb[0], seq)
            x2 = _ffn(x2, sc2, sh2, w_ffn_gu[i][None].astype(BF16), w_ffn_down[i][None].astype(BF16), g2, lng[1],
                      lnb[1], seq)
        else:
            w_in = w_in_odd[i]
            wf = jnp.zeros((d, LANES), F32).at[:, :FOX_HEADS].set(w_in[:, 3 * FOX_WIDTH:]).astype(BF16)
            bfp = jnp.zeros((1, LANES), F32).at[0, :FOX_HEADS].set(b_forget[i])
            q, k, v, fa = [a.reshape(bsz, seq, FOX_WIDTH) for a in
                           _odd_in(x2, sc1, sh1, w_in[:, :3 * FOX_WIDTH].astype(BF16), wf, bfp, seq)]
            o = _fox_attention(q, k, v, fa, bsz, seq)
            x2 = _odd_out(o.reshape(t, FOX_WIDTH), w_out_odd[i].astype(BF16), x2, g1, lng[0], lnb[0], seq)
            wrt = jnp.zeros((2 * N_EXPERTS, d), F32).at[:N_EXPERTS].set(w_router[i].T).astype(BF16)
            rankt, cnt, comb, rank = _router(x2, sc2, sh2, wrt, seq)
            n_tiles = (2 * t + N_EXPERTS * (MOE_TILE - 1) + MOE_TILE - 1) // MOE_TILE
            *meta, tile_expert, n_used = _route_meta(cnt[:, :, 0].astype(jnp.int32), n_tiles)
            xs = _dispatch(meta, x2, sc2, sh2, rankt, n_tiles, seq)
            y = _experts(tile_expert, n_used, xs, w_exp_gu[i].astype(BF16), w_exp_down[i].astype(BF16))
            x2 = _combine(meta, rank, comb, x2, g2, lng[1], lnb[1], y, seq)
    return x2.reshape(bsz, seq, d)
```

```python
import functools
import math

import numpy as np

import jax
import jax.numpy as jnp
from jax import lax
from jax.experimental import pallas as pl
from jax.experimental.pallas import tpu as pltpu

F32 = jnp.float32
BF16 = jnp.bfloat16

D_MODEL = 1024
DEPTH = 4
HEAD_DIM = 64
POOL_WINDOWS = (2, 4, 8, 16)
POOL_GROUP_DIM = 64
POOL_WIDTH = 256
DIL_CONFIGS = ((128, 1), (512, 4), (2048, 16))
DIL_HEADS = 4
DIL_GROUP_WIDTH = DIL_HEADS * HEAD_DIM
BAND_BLOCK = 128
BAND_BLOCKS_PER_STEP = 4
FOX_HEADS = 16
FOX_WIDTH = 1024
ROPE_THETA = 500000.0
ROPE_DIM = 16
D_FF = 2816
N_EXPERTS = 8
DEEPNORM_ALPHA = (2 * DEPTH) ** 0.25
LN_EPS = 1e-5
NEG = -1e30
LOG2E = 1.4426950408889634

LANES = 128
TOKEN_TILE = 512
FOX_K_TILE = 512
FOX_Q_TILE = 2 * FOX_K_TILE
FOX_ROW_CHUNK = 64
MOE_CHUNK = 512
MOE_TILE = 512
ROW_BLOCK = 128
MAX_SPAN = MOE_CHUNK // ROW_BLOCK + 1
FF_SUB = 256
COMBINE_WIN = 2 * ROW_BLOCK
COMBINE_MAX_WIN = (ROW_BLOCK - 1 + MOE_CHUNK + COMBINE_WIN - 1) // COMBINE_WIN
NO_ROW = -1e9
VMEM_LIMIT = 56 * 1024 * 1024


def _params(sem, vmem=VMEM_LIMIT):
    return pltpu.CompilerParams(dimension_semantics=sem, vmem_limit_bytes=vmem)


def _modulate(x, sc, sh):
    return (x * (1.0 + sc) + sh).astype(BF16)


def _deepnorm_ln(x, sub, gate, g, b):
    y = DEEPNORM_ALPHA * x + gate * sub
    mu = jnp.mean(y, axis=-1, keepdims=True)
    yc = y - mu
    var = jnp.mean(yc * yc, axis=-1, keepdims=True)
    return yc * lax.rsqrt(var + LN_EPS) * g + b


def _ada_kernel(c_ref, w_ref, b_ref, o_ref):
    c = c_ref[...]
    ca = (c * jax.nn.sigmoid(c)).astype(BF16)
    o_ref[0] = jnp.dot(ca, w_ref[0].astype(BF16), preferred_element_type=F32) + b_ref[0]


def _ada(c, w_ada, b_ada):
    depth, d, n = w_ada.shape
    bsz = c.shape[0]
    tn = 1024
    return pl.pallas_call(
        _ada_kernel,
        out_shape=jax.ShapeDtypeStruct((depth, bsz, n), F32),
        grid=(depth, n // tn),
        in_specs=[
            pl.BlockSpec((bsz, d), lambda l, j: (0, 0)),
            pl.BlockSpec((1, d, tn), lambda l, j: (l, 0, j)),
            pl.BlockSpec((1, 1, tn), lambda l, j: (l, 0, j)),
        ],
        out_specs=pl.BlockSpec((1, bsz, tn), lambda l, j: (l, 0, j)),
        compiler_params=_params(("parallel", "parallel")),
        name="ada_mod",
    )(c, w_ada, b_ada.reshape(depth, 1, n))


def _even_in_kernel(x_ref, sc_ref, sh_ref, w_ref, ct_ref, sa_ref, sb_ref, u_ref, *rest):
    qkv_refs, z_ref = rest[:9], rest[9]
    tm = x_ref.shape[0]
    h = _modulate(x_ref[...], sc_ref[0], sh_ref[0])
    u_ref[...] = jnp.dot(h, w_ref[:, 0:POOL_WIDTH], preferred_element_type=F32)
    ct, sa, sb = ct_ref[...], sa_ref[...], sb_ref[...]
    for n in range(9):
        lo = POOL_WIDTH + n * DIL_GROUP_WIDTH
        z = jnp.dot(h, w_ref[:, lo:lo + DIL_GROUP_WIDTH], preferred_element_type=F32)
        if n < 6:
            z = z * ct + pltpu.roll(z, 8, 1) * sa + pltpu.roll(z, DIL_GROUP_WIDTH - 8, 1) * sb
        if n < 3:
            z = z * (HEAD_DIM ** -0.5)
        dil = DIL_CONFIGS[n % 3][1]
        if dil == 1:
            qkv_refs[n][0, 0] = z.astype(BF16)
        else:
            for half in range(2):
                z_ref[half] = z[:, half * LANES:(half + 1) * LANES]
            for r in range(dil):
                rows = [z_ref[half, pl.ds(r, tm // dil, stride=dil), :] for half in range(2)]
                qkv_refs[n][0, r] = jnp.concatenate(rows, axis=1).astype(BF16)


def _even_in(x2, sc, sh, w_bf, tabs, seq):
    t, d = x2.shape
    tm = TOKEN_TILE
    tps = seq // tm
    bsz = t // seq
    n_in = w_bf.shape[1]
    mod_spec = pl.BlockSpec((1, 1, d), lambda i: (i // tps, 0, 0))
    tab_spec = pl.BlockSpec((tm, DIL_GROUP_WIDTH), lambda i: (i % tps, 0))
    out_spec = pl.BlockSpec((tm, DIL_GROUP_WIDTH), lambda i: (i, 0))
    dils = [dil for _, dil in DIL_CONFIGS] * 3
    qkv_shapes = [jax.ShapeDtypeStruct((bsz, dil, seq // dil, DIL_GROUP_WIDTH), BF16) for dil in dils]
    qkv_specs = [pl.BlockSpec((1, dil, tm // dil, DIL_GROUP_WIDTH), lambda i: (i // tps, 0, i % tps, 0))
                 for dil in dils]
    return pl.pallas_call(
        _even_in_kernel,
        out_shape=[jax.ShapeDtypeStruct((t, POOL_WIDTH), F32)] + qkv_shapes,
        grid=(t // tm,),
        in_specs=[
            pl.BlockSpec((tm, d), lambda i: (i, 0)),
            mod_spec,
            mod_spec,
            pl.BlockSpec((d, n_in), lambda i: (0, 0)),
            tab_spec,
            tab_spec,
            tab_spec,
        ],
        out_specs=[out_spec] + qkv_specs,
        scratch_shapes=[pltpu.VMEM((DIL_GROUP_WIDTH // LANES, tm, LANES), F32)],
        compiler_params=_params(("parallel",)),
        name="even_in_proj",
    )(x2, sc, sh, w_bf, *tabs)


def _rope_tables(seq):
    pos = jnp.arange(seq, dtype=F32)
    inv = ROPE_THETA ** (-jnp.arange(0, ROPE_DIM, 2, dtype=F32) / ROPE_DIM)
    ang = pos[:, None] * inv[None, :]
    cos, sin = jnp.cos(ang), jnp.sin(ang)
    half = ROPE_DIM // 2
    zeros = jnp.zeros((seq, half), F32)
    rest = HEAD_DIM - ROPE_DIM
    ct = jnp.concatenate([cos, cos, jnp.ones((seq, rest), F32)], axis=1)
    sa = jnp.concatenate([zeros, sin, jnp.zeros((seq, rest), F32)], axis=1)
    sb = jnp.concatenate([-sin, zeros, jnp.zeros((seq, rest), F32)], axis=1)
    return tuple(jnp.tile(a, (1, DIL_HEADS)) for a in (ct, sa, sb))


def _band_kernel(q_ref, kp_ref, kc_ref, vp_ref, vc_ref, o_ref, lse_ref):
    mt = pl.program_id(2)
    blk = BAND_BLOCK
    nblk = q_ref.shape[2] // blk
    kall = jnp.concatenate([kp_ref[0, 0], kc_ref[0, 0]], axis=0)
    vall = jnp.concatenate([vp_ref[0, 0], vc_ref[0, 0]], axis=0)
    qi = lax.broadcasted_iota(jnp.int32, (blk, 2 * blk), 0)
    kj = lax.broadcasted_iota(jnp.int32, (blk, 2 * blk), 1)
    dist = qi + blk - kj
    in_band = (dist >= 0) & (dist <= blk)
    lane_q = lax.broadcasted_iota(jnp.int32, (blk, DIL_GROUP_WIDTH), 1) // HEAD_DIM
    lane_v = lax.broadcasted_iota(jnp.int32, (2 * blk, DIL_GROUP_WIDTH), 1) // HEAD_DIM
    for i in range(nblk):
        q = q_ref[0, 0, i * blk:(i + 1) * blk, :]
        kk = kall[i * blk:(i + 2) * blk]
        vv = vall[i * blk:(i + 2) * blk]
        if i == 0:
            valid = in_band & (kj >= jnp.where(mt > 0, 0, blk))
        else:
            valid = in_band
        acc = jnp.zeros((blk, DIL_GROUP_WIDTH), F32)
        den_full = jnp.ones((blk, DIL_GROUP_WIDTH), F32)
        lse_full = jnp.zeros((blk, DIL_GROUP_WIDTH), F32)
        for h in range(DIL_HEADS):
            qh = jnp.where(lane_q == h, q, jnp.zeros_like(q))
            s = lax.dot_general(qh, kk, (((1,), (1,)), ((), ())), preferred_element_type=F32)
            s = jnp.where(valid, s, NEG)
            m = jnp.max(s, axis=-1, keepdims=True)
            p = jnp.exp(s - m)
            den = jnp.sum(p, axis=-1, keepdims=True)
            vh = jnp.where(lane_v == h, vv, jnp.zeros_like(vv))
            acc = acc + jnp.dot(p.astype(BF16), vh, preferred_element_type=F32)
            den_full = jnp.where(lane_q == h, den, den_full)
            lse_full = jnp.where(lane_q == h, m + jnp.log(den), lse_full)
        o_ref[0, 0, i * blk:(i + 1) * blk, :] = acc / den_full
        lse_ref[0, 0, i * blk:(i + 1) * blk, :] = lse_full


def _band_attention(q, k, v):
    bsz, dil, ln, w = q.shape
    nblk = min(BAND_BLOCKS_PER_STEP, ln // BAND_BLOCK)
    rows = nblk * BAND_BLOCK
    cur = pl.BlockSpec((1, 1, rows, w), lambda b, r, m: (b, r, m, 0))
    prev = pl.BlockSpec((1, 1, BAND_BLOCK, w), lambda b, r, m: (b, r, jnp.maximum(m * nblk - 1, 0), 0))
    return pl.pallas_call(
        _band_kernel,
        out_shape=[jax.ShapeDtypeStruct(q.shape, F32)] * 2,
        grid=(bsz, dil, ln // rows),
        in_specs=[cur, prev, cur, prev, cur],
        out_specs=[cur, cur],
        compiler_params=_params(("parallel", "parallel", "parallel")),
        name=f"band_attn_d{dil}",
    )(q, k, k, v, v)


def _even_out_kernel(tps, u_ref, up_ref, o0, o1, o2, l0, l1, l2, wp_ref, ps_ref, wo_ref, x_ref, gate_ref,
                     g_ref, b_ref, out_ref, il_ref):
    i = pl.program_id(0)
    tm = u_ref.shape[0]
    halo = up_ref.shape[0]
    u = u_ref[...]
    prev = jnp.where(i % tps == 0, jnp.zeros_like(up_ref[...]), up_ref[...])
    ext = jnp.concatenate([prev, u], axis=0)
    lane_g = lax.broadcasted_iota(jnp.int32, (tm, POOL_WIDTH), 1) // POOL_GROUP_DIM
    row = lax.broadcasted_iota(jnp.int32, (tm, POOL_WIDTH), 0)
    pos1 = ((i % tps) * tm + row + 1).astype(F32)
    s = ext
    pooled = jnp.zeros((tm, POOL_WIDTH), F32)
    wvec = jnp.zeros((tm, POOL_WIDTH), F32)
    shift = 1
    for g, w in enumerate(POOL_WINDOWS):
        while shift < w:
            s = s + pltpu.roll(s, shift, 0)
            shift *= 2
        pooled = jnp.where(lane_g == g, s[halo:, :], pooled)
        wvec = jnp.where(lane_g == g, float(w), wvec)
    pooled = pooled / jnp.minimum(pos1, wvec) - u
    a_out = jnp.dot(pooled.astype(BF16), wp_ref[...], preferred_element_type=F32) * ps_ref[...]
    def token_order(ref, buf):
        dil = ref.shape[1]
        if dil == 1:
            return ref[0, 0]
        for r in range(dil):
            blk = ref[0, r]
            for half in range(2):
                buf[half, pl.ds(r, tm // dil, stride=dil), :] = blk[:, half * LANES:(half + 1) * LANES]
        return jnp.concatenate([buf[0], buf[1]], axis=1)

    la, lb, lc = [token_order(ref, il_ref.at[n]) for n, ref in enumerate((l0, l1, l2))]
    oa, ob, oc = [token_order(ref, il_ref.at[3 + n]) for n, ref in enumerate((o0, o1, o2))]
    mx = jnp.maximum(jnp.maximum(la, lb), lc)
    ea, eb, ec = jnp.exp(la - mx), jnp.exp(lb - mx), jnp.exp(lc - mx)
    b_out = (ea * oa + eb * ob + ec * oc) / (ea + eb + ec)
    mixed = jnp.concatenate([a_out, b_out], axis=-1).astype(BF16)
    sub = jnp.dot(mixed, wo_ref[...], preferred_element_type=F32)
    out_ref[...] = _deepnorm_ln(x_ref[...], sub, gate_ref[0], g_ref[...], b_ref[...])


def _even_out(u, os_, lses, wpool_bd, pool_scale, wo_bf, x2, gate, ln_g, ln_b, seq):
    t, d = x2.shape
    tm = TOKEN_TILE
    tps = seq // tm
    halo = 16
    row = lambda w: pl.BlockSpec((tm, w), lambda i: (i, 0))
    full = lambda a: pl.BlockSpec(a.shape, lambda i: (0,) * a.ndim)
    strided = [pl.BlockSpec((1, a.shape[1], tm // a.shape[1], DIL_GROUP_WIDTH), lambda i: (i // tps, 0, i % tps, 0))
               for a in (*os_, *lses)]
    return pl.pallas_call(
        functools.partial(_even_out_kernel, tps),
        out_shape=jax.ShapeDtypeStruct((t, d), F32),
        grid=(t // tm,),
        in_specs=[
            row(POOL_WIDTH),
            pl.BlockSpec((halo, POOL_WIDTH), lambda i: (jnp.maximum(i * (tm // halo) - 1, 0), 0)),
            *strided,
            full(wpool_bd),
            full(pool_scale),
            full(wo_bf),
            row(d),
            pl.BlockSpec((1, 1, d), lambda i: (i // tps, 0, 0)),
            full(ln_g),
            full(ln_b),
        ],
        out_specs=row(d),
        scratch_shapes=[pltpu.VMEM((6, DIL_GROUP_WIDTH // LANES, tm, LANES), F32)],
        compiler_params=_params(("parallel",)),
        name="even_out_proj",
    )(u, u, *os_, *lses, wpool_bd, pool_scale, wo_bf, x2, gate, ln_g, ln_b)


def _swiglu_rows(x, wgu_ref, wd_ref, acc_ref):
    for c in range(D_FF // FF_SUB):
        lo = c * FF_SUB
        gt = jnp.dot(x, wgu_ref[0, :, lo:lo + FF_SUB], preferred_element_type=F32)
        up = jnp.dot(x, wgu_ref[0, :, D_FF + lo:D_FF + lo + FF_SUB], preferred_element_type=F32)
        a = (gt * jax.nn.sigmoid(gt) * up).astype(BF16)
        part = jnp.dot(a, wd_ref[0, lo:lo + FF_SUB, :], preferred_element_type=F32)
        if c == 0:
            acc_ref[...] = part
        else:
            acc_ref[...] += part


def _ffn_kernel(x_ref, sc_ref, sh_ref, wgu_ref, wd_ref, gate_ref, g_ref, b_ref, out_ref, acc_ref):
    _swiglu_rows(_modulate(x_ref[...], sc_ref[0], sh_ref[0]), wgu_ref, wd_ref, acc_ref)
    out_ref[...] = _deepnorm_ln(x_ref[...], acc_ref[...], gate_ref[0], g_ref[...], b_ref[...])


def _ffn(x2, sc, sh, wgu_bf, wd_bf, gate, ln_g, ln_b, seq):
    t, d = x2.shape
    tm = TOKEN_TILE
    tps = seq // tm
    mod = pl.BlockSpec((1, 1, d), lambda i: (i // tps, 0, 0))
    row = pl.BlockSpec((tm, d), lambda i: (i, 0))
    vec = pl.BlockSpec((1, d), lambda i: (0, 0))
    full = lambda a: pl.BlockSpec(a.shape, lambda i: (0,) * a.ndim)
    return pl.pallas_call(
        _ffn_kernel,
        out_shape=jax.ShapeDtypeStruct((t, d), F32),
        grid=(t // tm,),
        in_specs=[row, mod, mod, full(wgu_bf), full(wd_bf), mod, vec, vec],
        out_specs=row,
        scratch_shapes=[pltpu.VMEM((tm, d), F32)],
        compiler_params=_params(("parallel",)),
        name="ffn_dense",
    )(x2, sc, sh, wgu_bf, wd_bf, gate, ln_g, ln_b)


def _split3(x):
    hi = x.astype(BF16)
    r = x - hi.astype(F32)
    mid = r.astype(BF16)
    lo = (r - mid.astype(F32)).astype(BF16)
    return hi, mid, lo


def _odd_in_kernel(tps, x_ref, sc_ref, sh_ref, w_ref, wf_ref, bf_ref, place_ref, q_ref, k_ref, v_ref, f_ref,
                   carry_ref):
    i = pl.program_id(0)
    tm = x_ref.shape[0]
    h = _modulate(x_ref[...], sc_ref[0], sh_ref[0])
    q = jnp.dot(h, w_ref[:, 0:FOX_WIDTH], preferred_element_type=F32)
    q_ref[...] = (q * (HEAD_DIM ** -0.5 * LOG2E)).astype(BF16)
    k_ref[...] = jnp.dot(h, w_ref[:, FOX_WIDTH:2 * FOX_WIDTH], preferred_element_type=F32).astype(BF16)
    v_ref[...] = jnp.dot(h, w_ref[:, 2 * FOX_WIDTH:3 * FOX_WIDTH], preferred_element_type=F32).astype(BF16)
    f = jnp.dot(h, wf_ref[...], preferred_element_type=F32) + bf_ref[...]
    logf = jnp.minimum(f, 0.0) - jnp.log(1.0 + jnp.exp(-jnp.abs(f)))
    r = lax.broadcasted_iota(jnp.int32, (tm, tm), 0)
    cidx = lax.broadcasted_iota(jnp.int32, (tm, tm), 1)
    tri = jnp.where(cidx <= r, 1.0, 0.0).astype(BF16)
    hi, mid, lo = _split3(logf)
    cs = (jnp.dot(tri, lo, preferred_element_type=F32) + jnp.dot(tri, mid, preferred_element_type=F32)
          + jnp.dot(tri, hi, preferred_element_type=F32))

    @pl.when(i % tps == 0)
    def _():
        carry_ref[...] = jnp.zeros_like(carry_ref)

    cs = cs + carry_ref[...]
    carry_ref[...] = cs[tm - 1:tm, :]
    parts = jnp.concatenate(_split3(cs * LOG2E), axis=1)
    f_ref[...] = jnp.dot(parts, place_ref[...], preferred_element_type=F32).astype(BF16)


def _gate_placement():
    place = np.zeros((3 * LANES, FOX_WIDTH), np.float32)
    for h in range(FOX_HEADS):
        base = (h // 2) * LANES + (HEAD_DIM if h % 2 == 0 else 0)
        for p in range(3):
            place[p * LANES + h, base + p] = 1.0
    return jnp.asarray(place, BF16)


def _odd_in(x2, sc, sh, w_bf, wf_bf, bf_pad, seq):
    t, d = x2.shape
    tm = TOKEN_TILE
    tps = seq // tm
    mod = pl.BlockSpec((1, 1, d), lambda i: (i // tps, 0, 0))
    row = lambda w: pl.BlockSpec((tm, w), lambda i: (i, 0))
    full = lambda a: pl.BlockSpec(a.shape, lambda i: (0,) * a.ndim)
    place = _gate_placement()
    return pl.pallas_call(
        functools.partial(_odd_in_kernel, tps),
        out_shape=[jax.ShapeDtypeStruct((t, FOX_WIDTH), BF16)] * 4,
        grid=(t // tm,),
        in_specs=[row(d), mod, mod, full(w_bf), full(wf_bf), full(bf_pad), full(place)],
        out_specs=[row(FOX_WIDTH)] * 4,
        scratch_shapes=[pltpu.VMEM((1, LANES), F32)],
        compiler_params=_params(("arbitrary",)),
        name="odd_in_proj",
    )(x2, sc, sh, w_bf, wf_bf, bf_pad, place)


def _fox_kernel(q_ref, k_ref, v_ref, f_ref, o_ref, acc_ref, m_ref, alpha_ref, s0_ref, s1_ref, p0_ref, p1_ref):
    qi = pl.program_id(2)
    tq = q_ref.shape[1]
    tk = FOX_K_TILE
    half = HEAD_DIM
    q = q_ref[0]
    lane_q = lax.broadcasted_iota(jnp.int32, q.shape, 1)
    lane_k = lax.broadcasted_iota(jnp.int32, (tk, LANES), 1)
    qa = jnp.where(lane_q < half, q, jnp.where(lane_q < half + 3, -1.0, 0.0).astype(BF16))
    qb = jnp.where(lane_q >= half, q, jnp.where(lane_q < 3, -1.0, 0.0).astype(BF16))
    ones_a = jnp.where(lane_k == half, 1.0, 0.0).astype(BF16)
    ones_b = jnp.where(lane_k == 0, 1.0, 0.0).astype(BF16)
    acc_ref[...] = jnp.zeros_like(acc_ref)
    m_ref[...] = jnp.full(m_ref.shape, NEG, F32)

    def scores(j, s_ref, row0=0):
        start = pl.multiple_of(j * tk, tk)
        k = k_ref[0, pl.ds(start, tk), :]
        fa = f_ref[0, pl.ds(start, tk), :]
        ka = jnp.where(lane_k < half, k, fa)
        kb = jnp.where(lane_k >= half, k, fa)
        for hh, (qh, kh) in enumerate(((qa, ka), (qb, kb))):
            s_ref[hh, row0:, :] = lax.dot_general(qh[row0:], kh, (((1,), (1,)), ((), ())),
                                                  preferred_element_type=F32)

    def softmax(j, s_ref, p_ref, row0=0, masked_rows=0):
        start = j * tk
        for hh in range(2):
            for r0 in range(row0, tq, FOX_ROW_CHUNK):
                rows = pl.ds(r0, FOX_ROW_CHUNK)
                s = s_ref[hh, rows, :]
                if r0 < row0 + masked_rows:
                    rr = lax.broadcasted_iota(jnp.int32, s.shape, 0) + (qi * tq + r0)
                    cc = lax.broadcasted_iota(jnp.int32, s.shape, 1) + start
                    s = jnp.where(cc <= rr, s, NEG)
                m_old = m_ref[hh, rows, :]
                m_new = jnp.maximum(m_old, jnp.max(s, axis=-1, keepdims=True))
                p_ref[hh, rows, :] = jnp.exp2(s - jnp.tile(m_new, (1, tk // LANES))).astype(BF16)
                alpha_ref[hh, rows, :] = jnp.exp2(m_old - m_new)
                m_ref[hh, rows, :] = m_new

    def values(j, p_ref, row0=0):
        start = pl.multiple_of(jnp.maximum(j, 0) * tk, tk)
        v = v_ref[0, pl.ds(start, tk), :]
        va = jnp.where(lane_k < half, v, ones_a)
        vb = jnp.where(lane_k >= half, v, ones_b)
        return [jnp.dot(p_ref[hh, row0:, :], vh, preferred_element_type=F32) for hh, vh in enumerate((va, vb))]

    def fold(pv, scale_row0=0):
        for hh in range(2):
            if scale_row0:
                acc_ref[hh, :scale_row0, :] += pv[hh][:scale_row0]
            acc_ref[hh, scale_row0:, :] = ((acc_ref[hh, scale_row0:, :] + pv[hh][scale_row0:])
                                           * alpha_ref[hh, scale_row0:, :])

    p1_ref[...] = jnp.zeros_like(p1_ref)
    scores(0, s0_ref)

    def body(jp, carry):
        pv = values(2 * jp - 1, p1_ref)
        scores(2 * jp + 1, s1_ref)
        softmax(2 * jp, s0_ref, p0_ref)
        fold(pv)
        pv = values(2 * jp, p0_ref)
        scores(2 * jp + 2, s0_ref)
        softmax(2 * jp + 1, s1_ref, p1_ref)
        fold(pv)
        return carry

    lax.fori_loop(0, qi, body, 0)
    pv = values(2 * qi - 1, p1_ref)
    scores(2 * qi + 1, s1_ref, row0=tk)
    softmax(2 * qi, s0_ref, p0_ref, masked_rows=tk)
    fold(pv)
    pv = values(2 * qi, p0_ref)
    softmax(2 * qi + 1, s1_ref, p1_ref, row0=tk, masked_rows=tk)
    fold(pv, scale_row0=tk)
    pv = values(2 * qi + 1, p1_ref, row0=tk)
    for hh in range(2):
        acc_ref[hh, tk:, :] += pv[hh]

    acc_a, acc_b = acc_ref[0], acc_ref[1]
    lane_o = lax.broadcasted_iota(jnp.int32, (tq, LANES), 1)
    o = jnp.where(lane_o < half, acc_a / acc_a[:, half:half + 1], acc_b / acc_b[:, 0:1])
    o_ref[0] = o.astype(BF16)


def _fox_attention(q, k, v, fa, bsz, seq):
    tq = FOX_Q_TILE
    npairs = FOX_HEADS // 2
    tile = pl.BlockSpec((1, tq, LANES), lambda b, j, i: (b, i, j))
    whole = pl.BlockSpec((1, seq, LANES), lambda b, j, i: (b, 0, j))
    return pl.pallas_call(
        _fox_kernel,
        out_shape=jax.ShapeDtypeStruct((bsz, seq, FOX_WIDTH), BF16),
        grid=(bsz, npairs, seq // tq),
        in_specs=[tile, whole, whole, whole],
        out_specs=tile,
        scratch_shapes=[pltpu.VMEM((2, tq, LANES), F32), pltpu.VMEM((2, tq, LANES), F32),
                        pltpu.VMEM((2, tq, LANES), F32),
                        pltpu.VMEM((2, tq, FOX_K_TILE), F32), pltpu.VMEM((2, tq, FOX_K_TILE), F32),
                        pltpu.VMEM((2, tq, FOX_K_TILE), BF16), pltpu.VMEM((2, tq, FOX_K_TILE), BF16)],
        compiler_params=_params(("parallel", "parallel", "arbitrary")),
        name="fox_attn",
    )(q, k, v, fa)


def _odd_out_kernel(o_ref, wo_ref, x_ref, gate_ref, g_ref, b_ref, sc_ref, sh_ref, wrt_ref, tri_ref, out_ref,
                    rankt_ref, cnt_ref, comb_ref, rank_ref):
    sub = jnp.dot(o_ref[...], wo_ref[...], preferred_element_type=F32)
    x_new = _deepnorm_ln(x_ref[...], sub, gate_ref[0], g_ref[...], b_ref[...])
    out_ref[...] = x_new
    _route_chunk(_modulate(x_new, sc_ref[0], sh_ref[0]), wrt_ref, tri_ref, rankt_ref, cnt_ref, comb_ref, rank_ref)


def _odd_out(o2, wo_bf, x2, gate, ln_g, ln_b, sc_next, sh_next, wrt, seq):
    t, d = x2.shape
    tm = MOE_CHUNK
    tps = seq // tm
    row = lambda w: pl.BlockSpec((tm, w), lambda i: (i, 0))
    full = lambda a: pl.BlockSpec(a.shape, lambda i: (0,) * a.ndim)
    mod = pl.BlockSpec((1, 1, d), lambda i: (i // tps, 0, 0))
    tri = jnp.asarray(np.triu(np.ones((tm, tm), np.float32), 1), BF16)
    return pl.pallas_call(
        _odd_out_kernel,
        out_shape=[jax.ShapeDtypeStruct((t, d), F32),
                   jax.ShapeDtypeStruct((N_EXPERTS, t), F32),
                   jax.ShapeDtypeStruct((t // tm, N_EXPERTS, LANES), F32),
                   jax.ShapeDtypeStruct((t, LANES), F32),
                   jax.ShapeDtypeStruct((t, LANES), F32)],
        grid=(t // tm,),
        in_specs=[row(FOX_WIDTH), full(wo_bf), row(d), mod, full(ln_g), full(ln_b), mod, mod, full(wrt), full(tri)],
        out_specs=[row(d),
                   pl.BlockSpec((N_EXPERTS, tm), lambda i: (0, i)),
                   pl.BlockSpec((1, N_EXPERTS, LANES), lambda i: (i, 0, 0)),
                   row(LANES),
                   row(LANES)],
        compiler_params=_params(("parallel",)),
        name="odd_out_proj",
    )(o2, wo_bf, x2, gate, ln_g, ln_b, sc_next, sh_next, wrt, tri)


def _route_chunk(h, wrt_ref, tri_ref, rankt_ref, cnt_ref, comb_ref, rank_ref):
    c = h.shape[0]
    lt = lax.dot_general(wrt_ref[...], h, (((1,), (1,)), ((), ())), preferred_element_type=F32)[0:N_EXPERTS]
    row = lax.broadcasted_iota(jnp.int32, lt.shape, 0)
    m1 = jnp.max(lt, axis=0, keepdims=True)
    i1 = jnp.min(jnp.where(lt == m1, row, N_EXPERTS), axis=0, keepdims=True)
    rest = jnp.where(row == i1, -jnp.inf, lt)
    m2 = jnp.max(rest, axis=0, keepdims=True)
    i2 = jnp.min(jnp.where(rest == m2, row, N_EXPERTS), axis=0, keepdims=True)
    e2 = jnp.exp(m2 - m1)
    g1 = 1.0 / (1.0 + e2)
    g2 = e2 / (1.0 + e2)
    combt = jnp.where(row == i1, g1, jnp.where(row == i2, g2, 0.0))
    sel = jnp.where((row == i1) | (row == i2), 1.0, 0.0)
    sel16 = jnp.concatenate([sel, jnp.zeros_like(sel)], axis=0).astype(BF16)
    earlier = jnp.dot(sel16, tri_ref[...], preferred_element_type=F32)[0:N_EXPERTS]
    rankt = jnp.where(sel > 0.0, earlier, NO_ROW)
    rankt_ref[...] = rankt
    cnt_ref[0] = jnp.broadcast_to(jnp.sum(sel, axis=1, keepdims=True), (N_EXPERTS, LANES))
    pad = LANES - N_EXPERTS
    comb_ref[...] = jnp.concatenate([combt, jnp.zeros((pad, c), F32)], axis=0).T
    rank_ref[...] = jnp.concatenate([rankt, jnp.full((pad, c), NO_ROW, F32)], axis=0).T


def _route_meta(cnt, n_tiles):
    before = jnp.cumsum(cnt, axis=0) - cnt
    total = jnp.sum(cnt, axis=0)
    region = ((total + MOE_TILE - 1) // MOE_TILE) * MOE_TILE
    base = jnp.cumsum(region) - region
    n_used = (jnp.sum(region) // MOE_TILE).reshape(1)
    starts = jnp.arange(n_tiles, dtype=jnp.int32) * MOE_TILE
    tile_expert = jnp.minimum(jnp.sum(starts[:, None] >= (base + region)[None, :], axis=1), N_EXPERTS - 1)
    i32 = lambda a: a.astype(jnp.int32)
    return i32(before.reshape(-1)), i32(cnt.reshape(-1)), i32(base), i32(total), i32(tile_expert), i32(n_used)


def _chunk_block(before_ref, cnt_ref, chunk, e, j):
    g = before_ref[chunk * N_EXPERTS + e]
    n = cnt_ref[chunk * N_EXPERTS + e]
    k = g // ROW_BLOCK + j
    return g, n, k, (n > 0) & (k * ROW_BLOCK < g + n)


def _dispatch_kernel(before_ref, cnt_ref, base_ref, total_ref, x_ref, sc_ref, sh_ref, rankt_ref, xs_ref,
                     stage_ref, slot_ref, zero_ref, sem, fsem, zsem, usem):
    c = pl.program_id(0)
    last = c == pl.num_programs(0) - 1
    h = _modulate(x_ref[...], sc_ref[0], sh_ref[0])

    @pl.when(c == 0)
    def _():
        stage_ref[...] = jnp.zeros_like(stage_ref)
        zero_ref[...] = jnp.zeros_like(zero_ref)

    block_row = lax.broadcasted_iota(jnp.int32, (ROW_BLOCK, MOE_CHUNK), 0).astype(F32)
    pair_row = lax.broadcasted_iota(jnp.int32, (2 * ROW_BLOCK, MOE_CHUNK), 0).astype(F32)

    def block_copy(e, j, k):
        row0 = pl.multiple_of(base_ref[e] + k * ROW_BLOCK, ROW_BLOCK)
        return pltpu.make_async_copy(slot_ref.at[e, j], xs_ref.at[pl.ds(row0, ROW_BLOCK)], sem.at[e, j])

    def any_beyond_two(chunk):
        flag = _chunk_block(before_ref, cnt_ref, chunk, 0, 2)[3]
        for e in range(1, N_EXPERTS):
            flag = flag | _chunk_block(before_ref, cnt_ref, chunk, e, 2)[3]
        return flag

    def for_blocks(chunk, fn):
        for e in range(N_EXPERTS):
            for j in range(2):
                fn(e, j)

        @pl.when(any_beyond_two(chunk))
        def _():
            for e in range(N_EXPERTS):
                for j in range(2, MAX_SPAN):
                    fn(e, j)

    def wait_block_copies(chunk):
        def wait_one(e, j):
            g, n, k, touched = _chunk_block(before_ref, cnt_ref, chunk, e, j)

            @pl.when(touched & (g + n >= (k + 1) * ROW_BLOCK))
            def _():
                block_copy(e, j, k).wait()

        for_blocks(chunk, wait_one)

    selectors = []
    for e in range(N_EXPERTS):
        g = before_ref[c * N_EXPERTS + e]
        off = (g % ROW_BLOCK).astype(F32)
        selectors.append(jnp.where(rankt_ref[e:e + 1, :] + off == pair_row, 1.0, 0.0).astype(BF16))
    first_two = jnp.dot(jnp.concatenate(selectors, axis=0), h, preferred_element_type=F32).astype(BF16)

    @pl.when(c > 0)
    def _():
        wait_block_copies(c - 1)

    def emit_block(e, j):
        g, n, k, touched = _chunk_block(before_ref, cnt_ref, c, e, j)
        complete = g + n >= (k + 1) * ROW_BLOCK

        @pl.when(touched)
        def _():
            if j < 2:
                r0 = (2 * e + j) * ROW_BLOCK
                blk = first_two[r0:r0 + ROW_BLOCK]
            else:
                off = (g - k * ROW_BLOCK).astype(F32)
                onehot = jnp.where(rankt_ref[e:e + 1, :] + off == block_row, 1.0, 0.0).astype(BF16)
                blk = jnp.dot(onehot, h, preferred_element_type=F32).astype(BF16)
            if j == 0:
                blk = (blk.astype(F32) + stage_ref[e].astype(F32)).astype(BF16)
            slot_ref[e, j] = blk
            stage_ref[e] = jnp.where(complete, jnp.zeros_like(blk), blk)

            @pl.when(complete)
            def _():
                block_copy(e, j, k).start()

    for_blocks(c, emit_block)

    def tail_copies(e):
        total = total_ref[e]
        n_data = (total + ROW_BLOCK - 1) // ROW_BLOCK
        n_region = ((total + MOE_TILE - 1) // MOE_TILE) * (MOE_TILE // ROW_BLOCK)
        part_row = pl.multiple_of(base_ref[e] + (total // ROW_BLOCK) * ROW_BLOCK, ROW_BLOCK)
        part = pltpu.make_async_copy(stage_ref.at[e], xs_ref.at[pl.ds(part_row, ROW_BLOCK)], fsem.at[e])
        out = [(total % ROW_BLOCK != 0, part)]
        for z in range(MOE_TILE // ROW_BLOCK - 1):
            zrow = pl.multiple_of(base_ref[e] + (n_data + z) * ROW_BLOCK, ROW_BLOCK)
            zc = pltpu.make_async_copy(zero_ref, xs_ref.at[pl.ds(zrow, ROW_BLOCK)], zsem.at[e, z])
            out.append((n_data + z < n_region, zc))
        return out

    def unused_copies():
        e = N_EXPERTS - 1
        used = base_ref[e] + ((total_ref[e] + MOE_TILE - 1) // MOE_TILE) * MOE_TILE
        out = []
        for z in range(usem.shape[0]):
            zrow = pl.multiple_of(used + z * ROW_BLOCK, ROW_BLOCK)
            zc = pltpu.make_async_copy(zero_ref, xs_ref.at[pl.ds(zrow, ROW_BLOCK)], usem.at[z])
            out.append((zrow < xs_ref.shape[0], zc))
        return out

    def all_tail_copies():
        return [cc for e in range(N_EXPERTS) for cc in tail_copies(e)] + unused_copies()

    @pl.when(last)
    def _():
        for cond, cp in all_tail_copies():
            @pl.when(cond)
            def _():
                cp.start()

    @pl.when(last)
    def _():
        wait_block_copies(c)

    @pl.when(last)
    def _():
        for cond, cp in all_tail_copies():
            @pl.when(cond)
            def _():
                cp.wait()


def _dispatch(meta, x2, sc, sh, rankt, n_tiles, seq):
    before, cnt, base, total = meta
    t, d = x2.shape
    cs = MOE_CHUNK
    cps = seq // cs
    mod = pl.BlockSpec((1, 1, d), lambda i, *_: (i // cps, 0, 0))
    grid_spec = pltpu.PrefetchScalarGridSpec(
        num_scalar_prefetch=4,
        grid=(t // cs,),
        in_specs=[pl.BlockSpec((cs, d), lambda i, *_: (i, 0)), mod, mod,
                  pl.BlockSpec((N_EXPERTS, cs), lambda i, *_: (0, i))],
        out_specs=pl.BlockSpec(memory_space=pl.ANY),
        scratch_shapes=[pltpu.VMEM((N_EXPERTS, ROW_BLOCK, d), BF16),
                        pltpu.VMEM((N_EXPERTS, MAX_SPAN, ROW_BLOCK, d), BF16),
                        pltpu.VMEM((ROW_BLOCK, d), BF16),
                        pltpu.SemaphoreType.DMA((N_EXPERTS, MAX_SPAN)),
                        pltpu.SemaphoreType.DMA((N_EXPERTS,)),
                        pltpu.SemaphoreType.DMA((N_EXPERTS, MOE_TILE // ROW_BLOCK - 1)),
                        pltpu.SemaphoreType.DMA(((n_tiles * MOE_TILE - 2 * t) // ROW_BLOCK,))],
    )
    return pl.pallas_call(
        _dispatch_kernel,
        out_shape=jax.ShapeDtypeStruct((n_tiles * MOE_TILE, d), BF16),
        grid_spec=grid_spec,
        compiler_params=_params(("arbitrary",)),
        name="moe_dispatch",
    )(before, cnt, base, total, x2, sc, sh, rankt)


def _experts_kernel(te_ref, nu_ref, x_ref, wgu_ref, wd_ref, y_ref, acc_ref):
    @pl.when(pl.program_id(0) >= nu_ref[0])
    def _():
        y_ref[...] = jnp.zeros_like(y_ref)

    @pl.when(pl.program_id(0) < nu_ref[0])
    def _():
        _swiglu_rows(x_ref[...], wgu_ref, wd_ref, acc_ref)
        y_ref[...] = acc_ref[...].astype(BF16)


def _experts(tile_expert, n_used, xs, wgu_bf, wd_bf):
    rows, d = xs.shape
    tm = MOE_TILE
    clamp = lambda i, nu: jnp.minimum(i, nu[0] - 1)
    grid_spec = pltpu.PrefetchScalarGridSpec(
        num_scalar_prefetch=2,
        grid=(rows // tm,),
        in_specs=[pl.BlockSpec((tm, d), lambda i, te, nu: (clamp(i, nu), 0)),
                  pl.BlockSpec((1, d, 2 * D_FF), lambda i, te, nu: (te[clamp(i, nu)], 0, 0)),
                  pl.BlockSpec((1, D_FF, d), lambda i, te, nu: (te[clamp(i, nu)], 0, 0))],
        out_specs=pl.BlockSpec((tm, d), lambda i, te, nu: (i, 0)),
        scratch_shapes=[pltpu.VMEM((tm, d), F32)],
    )
    return pl.pallas_call(
        _experts_kernel,
        out_shape=jax.ShapeDtypeStruct((rows, d), BF16),
        grid_spec=grid_spec,
        compiler_params=_params(("arbitrary",)),
        name="moe_experts",
    )(tile_expert, n_used, xs, wgu_bf, wd_bf)


def _combine_kernel(before_ref, cnt_ref, base_ref, rank_ref, comb_ref, x_ref, gate_ref, g_ref, b_ref, y_ref,
                    out_ref, ybuf, xbuf, acc_ref, sem, xsem):
    c = pl.program_id(0)
    slot = c % 2

    def window(chunk, e, w):
        g = before_ref[chunk * N_EXPERTS + e]
        n = cnt_ref[chunk * N_EXPERTS + e]
        first = (g // ROW_BLOCK) * ROW_BLOCK + w * COMBINE_WIN
        row0 = pl.multiple_of(jnp.minimum(base_ref[e] + first, y_ref.shape[0] - COMBINE_WIN), ROW_BLOCK)
        off = (base_ref[e] + g - row0).astype(F32)
        return (n > 0) & (first < g + n), row0, off

    def first_copy(chunk, sl, e):
        touched, row0, _ = window(chunk, e, 0)
        return touched, pltpu.make_async_copy(y_ref.at[pl.ds(row0, COMBINE_WIN)], ybuf.at[sl, e], sem.at[sl, e])

    def later_copy(chunk, sl, e, w):
        touched, row0, _ = window(chunk, e, w)
        return touched, pltpu.make_async_copy(y_ref.at[pl.ds(row0, COMBINE_WIN)], xbuf.at[sl, e, w - 1],
                                              xsem.at[sl, e, w - 1])

    def any_later(chunk):
        flag = window(chunk, 0, 1)[0]
        for e in range(1, N_EXPERTS):
            flag = flag | window(chunk, e, 1)[0]
        return flag

    def fetch(chunk, sl):
        for e in range(N_EXPERTS):
            touched, cp = first_copy(chunk, sl, e)

            @pl.when(touched)
            def _():
                cp.start()

        @pl.when(any_later(chunk))
        def _():
            for e in range(N_EXPERTS):
                for w in range(1, COMBINE_MAX_WIN):
                    touched, cp = later_copy(chunk, sl, e, w)

                    @pl.when(touched)
                    def _():
                        cp.start()

    @pl.when(c == 0)
    def _():
        ybuf[...] = jnp.zeros_like(ybuf)
        fetch(0, 0)

    @pl.when(c + 1 < pl.num_programs(0))
    def _():
        fetch(c + 1, 1 - slot)

    lane = lax.broadcasted_iota(jnp.int32, (MOE_CHUNK, COMBINE_WIN), 1).astype(F32)
    acc_ref[...] = jnp.zeros_like(acc_ref)
    for e in range(N_EXPERTS):
        col = rank_ref[:, e:e + 1]
        weight = comb_ref[:, e:e + 1]
        touched, _, off = window(c, e, 0)

        @pl.when(touched)
        def _():
            first_copy(c, slot, e)[1].wait()

        onehot = jnp.where(col + off == lane, 1.0, 0.0).astype(BF16)
        acc_ref[...] += weight * jnp.dot(onehot, ybuf[slot, e], preferred_element_type=F32)

    @pl.when(any_later(c))
    def _():
        for e in range(N_EXPERTS):
            for w in range(1, COMBINE_MAX_WIN):
                touched, _, off = window(c, e, w)

                @pl.when(touched)
                def _():
                    later_copy(c, slot, e, w)[1].wait()
                    onehot = jnp.where(rank_ref[:, e:e + 1] + off == lane, 1.0, 0.0).astype(BF16)
                    acc_ref[...] += comb_ref[:, e:e + 1] * jnp.dot(onehot, xbuf[slot, e, w - 1],
                                                                   preferred_element_type=F32)

    out_ref[...] = _deepnorm_ln(x_ref[...], acc_ref[...], gate_ref[0], g_ref[...], b_ref[...])


def _combine(meta, rank, comb, x2, gate, ln_g, ln_b, y, seq):
    before, cnt, base, _ = meta
    t, d = x2.shape
    cs = MOE_CHUNK
    cps = seq // cs
    row = lambda w: pl.BlockSpec((cs, w), lambda i, *_: (i, 0))
    vec = pl.BlockSpec((1, d), lambda i, *_: (0, 0))
    grid_spec = pltpu.PrefetchScalarGridSpec(
        num_scalar_prefetch=3,
        grid=(t // cs,),
        in_specs=[row(LANES), row(LANES), row(d), pl.BlockSpec((1, 1, d), lambda i, *_: (i // cps, 0, 0)),
                  vec, vec, pl.BlockSpec(memory_space=pl.ANY)],
        out_specs=row(d),
        scratch_shapes=[pltpu.VMEM((2, N_EXPERTS, COMBINE_WIN, d), BF16),
                        pltpu.VMEM((2, N_EXPERTS, COMBINE_MAX_WIN - 1, COMBINE_WIN, d), BF16),
                        pltpu.VMEM((cs, d), F32),
                        pltpu.SemaphoreType.DMA((2, N_EXPERTS)),
                        pltpu.SemaphoreType.DMA((2, N_EXPERTS, COMBINE_MAX_WIN - 1))],
    )
    return pl.pallas_call(
        _combine_kernel,
        out_shape=jax.ShapeDtypeStruct((t, d), F32),
        grid_spec=grid_spec,
        compiler_params=_params(("arbitrary",)),
        name="moe_combine",
    )(before, cnt, base, rank, comb, x2, gate, ln_g, ln_b, y)


def _block_diag(w_pool):
    g, c, _ = w_pool.shape
    out = jnp.zeros((g * c, g * c), w_pool.dtype)
    for i in range(g):
        out = out.at[i * c:(i + 1) * c, i * c:(i + 1) * c].set(w_pool[i])
    return out


def kernel(x, c, w_ada, b_ada, ln_g, ln_b, w_in_even, w_pool, pool_scale, w_out_even, w_ffn_gu, w_ffn_down,
           w_in_odd, b_forget, w_out_odd, w_router, w_exp_gu, w_exp_down):
    bsz, seq, d = x.shape
    t = bsz * seq
    tabs = _rope_tables(seq)
    ada = _ada(c, w_ada, b_ada)
    x2 = x.reshape(t, d)
    for l in range(DEPTH):
        i = l // 2
        sh1, sc1, g1, sh2, sc2, g2 = [ada[l, :, n * d:(n + 1) * d].reshape(bsz, 1, d) for n in range(6)]
        lng = ln_g[l].reshape(2, 1, d)
        lnb = ln_b[l].reshape(2, 1, d)
        if l % 2 == 0:
            outs = _even_in(x2, sc1, sh1, w_in_even[i].astype(BF16), tabs, seq)
            u, qkv = outs[0], outs[1:]
            os_, lses = [], []
            for g in range(len(DIL_CONFIGS)):
                o, lse = _band_attention(qkv[g], qkv[3 + g], qkv[6 + g])
                os_.append(o)
                lses.append(lse)
            x2 = _even_out(u, os_, lses, _block_diag(w_pool[i]).astype(BF16), pool_scale[i].reshape(1, -1),
                           w_out_even[i].astype(BF16), x2, g1, lng[0], lnb[0], seq)
            x2 = _ffn(x2, sc2, sh2, w_ffn_gu[i][None].astype(BF16), w_ffn_down[i][None].astype(BF16), g2, lng[1],
                      lnb[1], seq)
        else:
            w_in = w_in_odd[i]
            wf = jnp.zeros((d, LANES), F32).at[:, :FOX_HEADS].set(w_in[:, 3 * FOX_WIDTH:]).astype(BF16)
            bfp = jnp.zeros((1, LANES), F32).at[0, :FOX_HEADS].set(b_forget[i])
            q, k, v, fa = [a.reshape(bsz, seq, FOX_WIDTH) for a in
                           _odd_in(x2, sc1, sh1, w_in[:, :3 * FOX_WIDTH].astype(BF16), wf, bfp, seq)]
            o = _fox_attention(q, k, v, fa, bsz, seq)
            wrt = jnp.zeros((2 * N_EXPERTS, d), F32).at[:N_EXPERTS].set(w_router[i].T).astype(BF16)
            x2, rankt, cnt, comb, rank = _odd_out(o.reshape(t, FOX_WIDTH), w_out_odd[i].astype(BF16), x2, g1, lng[0],
                                                  lnb[0], sc2, sh2, wrt, seq)
            n_tiles = (2 * t + N_EXPERTS * (MOE_TILE - 1) + MOE_TILE - 1) // MOE_TILE
            *meta, tile_expert, n_used = _route_meta(cnt[:, :, 0].astype(jnp.int32), n_tiles)
            xs = _dispatch(meta, x2, sc2, sh2, rankt, n_tiles, seq)
            y = _experts(tile_expert, n_used, xs, w_exp_gu[i].astype(BF16), w_exp_down[i].astype(BF16))
            x2 = _combine(meta, rank, comb, x2, g2, lng[1], lnb[1], y, seq)
    return x2.reshape(bsz, seq, d)
```

```python
import functools
import math

import numpy as np

import jax
import jax.numpy as jnp
from jax import lax
from jax.experimental import pallas as pl
from jax.experimental.pallas import tpu as pltpu

F32 = jnp.float32
BF16 = jnp.bfloat16

D_MODEL = 1024
DEPTH = 4
HEAD_DIM = 64
POOL_WINDOWS = (2, 4, 8, 16)
POOL_GROUP_DIM = 64
POOL_WIDTH = 256
DIL_CONFIGS = ((128, 1), (512, 4), (2048, 16))
DIL_HEADS = 4
DIL_GROUP_WIDTH = DIL_HEADS * HEAD_DIM
BAND_BLOCK = 128
BAND_BLOCKS_PER_STEP = 8
FOX_HEADS = 16
FOX_WIDTH = 1024
ROPE_THETA = 500000.0
ROPE_DIM = 16
D_FF = 2816
N_EXPERTS = 8
DEEPNORM_ALPHA = (2 * DEPTH) ** 0.25
LN_EPS = 1e-5
NEG = -1e30
LOG2E = 1.4426950408889634

LANES = 128
TOKEN_TILE = 512
FOX_K_TILE = 512
FOX_Q_TILE = 2 * FOX_K_TILE
FOX_ROW_CHUNK = 64
MOE_CHUNK = 512
MOE_TILE = 512
ROW_BLOCK = 128
MAX_SPAN = MOE_CHUNK // ROW_BLOCK + 1
FF_SUB = 256
COMBINE_WIN = 2 * ROW_BLOCK
COMBINE_MAX_WIN = (ROW_BLOCK - 1 + MOE_CHUNK + COMBINE_WIN - 1) // COMBINE_WIN
NO_ROW = -1e9
VMEM_LIMIT = 56 * 1024 * 1024


def _params(sem, vmem=VMEM_LIMIT):
    return pltpu.CompilerParams(dimension_semantics=sem, vmem_limit_bytes=vmem)


def _modulate(x, sc, sh):
    return (x * (1.0 + sc) + sh).astype(BF16)


def _deepnorm_ln(x, sub, gate, g, b):
    y = DEEPNORM_ALPHA * x + gate * sub
    mu = jnp.mean(y, axis=-1, keepdims=True)
    yc = y - mu
    var = jnp.mean(yc * yc, axis=-1, keepdims=True)
    return yc * lax.rsqrt(var + LN_EPS) * g + b


def _ada_kernel(c_ref, w_ref, b_ref, o_ref):
    c = c_ref[...]
    ca = (c * jax.nn.sigmoid(c)).astype(BF16)
    o_ref[0] = jnp.dot(ca, w_ref[0].astype(BF16), preferred_element_type=F32) + b_ref[0]


def _ada(c, w_ada, b_ada):
    depth, d, n = w_ada.shape
    bsz = c.shape[0]
    tn = 1024
    return pl.pallas_call(
        _ada_kernel,
        out_shape=jax.ShapeDtypeStruct((depth, bsz, n), F32),
        grid=(depth, n // tn),
        in_specs=[
            pl.BlockSpec((bsz, d), lambda l, j: (0, 0)),
            pl.BlockSpec((1, d, tn), lambda l, j: (l, 0, j)),
            pl.BlockSpec((1, 1, tn), lambda l, j: (l, 0, j)),
        ],
        out_specs=pl.BlockSpec((1, bsz, tn), lambda l, j: (l, 0, j)),
        compiler_params=_params(("parallel", "parallel")),
        name="ada_mod",
    )(c, w_ada, b_ada.reshape(depth, 1, n))


def _even_in_kernel(x_ref, sc_ref, sh_ref, w_ref, ct_ref, sa_ref, sb_ref, u_ref, *rest):
    qkv_refs, z_ref = rest[:9], rest[9]
    tm = x_ref.shape[0]
    h = _modulate(x_ref[...], sc_ref[0], sh_ref[0])
    u_ref[...] = jnp.dot(h, w_ref[:, 0:POOL_WIDTH], preferred_element_type=F32)
    ct, sa, sb = ct_ref[...], sa_ref[...], sb_ref[...]
    for n in range(9):
        lo = POOL_WIDTH + n * DIL_GROUP_WIDTH
        z = jnp.dot(h, w_ref[:, lo:lo + DIL_GROUP_WIDTH], preferred_element_type=F32)
        if n < 6:
            z = z * ct + pltpu.roll(z, 8, 1) * sa + pltpu.roll(z, DIL_GROUP_WIDTH - 8, 1) * sb
        if n < 3:
            z = z * (HEAD_DIM ** -0.5)
        dil = DIL_CONFIGS[n % 3][1]
        if dil == 1:
            qkv_refs[n][0, 0] = z.astype(BF16)
        else:
            for half in range(2):
                z_ref[half] = z[:, half * LANES:(half + 1) * LANES]
            for r in range(dil):
                rows = [z_ref[half, pl.ds(r, tm // dil, stride=dil), :] for half in range(2)]
                qkv_refs[n][0, r] = jnp.concatenate(rows, axis=1).astype(BF16)


def _even_in(x2, sc, sh, w_bf, tabs, seq):
    t, d = x2.shape
    tm = TOKEN_TILE
    tps = seq // tm
    bsz = t // seq
    n_in = w_bf.shape[1]
    mod_spec = pl.BlockSpec((1, 1, d), lambda i: (i // tps, 0, 0))
    tab_spec = pl.BlockSpec((tm, DIL_GROUP_WIDTH), lambda i: (i % tps, 0))
    out_spec = pl.BlockSpec((tm, DIL_GROUP_WIDTH), lambda i: (i, 0))
    dils = [dil for _, dil in DIL_CONFIGS] * 3
    qkv_shapes = [jax.ShapeDtypeStruct((bsz, dil, seq // dil, DIL_GROUP_WIDTH), BF16) for dil in dils]
    qkv_specs = [pl.BlockSpec((1, dil, tm // dil, DIL_GROUP_WIDTH), lambda i: (i // tps, 0, i % tps, 0))
                 for dil in dils]
    return pl.pallas_call(
        _even_in_kernel,
        out_shape=[jax.ShapeDtypeStruct((t, POOL_WIDTH), F32)] + qkv_shapes,
        grid=(t // tm,),
        in_specs=[
            pl.BlockSpec((tm, d), lambda i: (i, 0)),
            mod_spec,
            mod_spec,
            pl.BlockSpec((d, n_in), lambda i: (0, 0)),
            tab_spec,
            tab_spec,
            tab_spec,
        ],
        out_specs=[out_spec] + qkv_specs,
        scratch_shapes=[pltpu.VMEM((DIL_GROUP_WIDTH // LANES, tm, LANES), F32)],
        compiler_params=_params(("parallel",)),
        name="even_in_proj",
    )(x2, sc, sh, w_bf, *tabs)


def _rope_tables(seq):
    pos = jnp.arange(seq, dtype=F32)
    inv = ROPE_THETA ** (-jnp.arange(0, ROPE_DIM, 2, dtype=F32) / ROPE_DIM)
    ang = pos[:, None] * inv[None, :]
    cos, sin = jnp.cos(ang), jnp.sin(ang)
    half = ROPE_DIM // 2
    zeros = jnp.zeros((seq, half), F32)
    rest = HEAD_DIM - ROPE_DIM
    ct = jnp.concatenate([cos, cos, jnp.ones((seq, rest), F32)], axis=1)
    sa = jnp.concatenate([zeros, sin, jnp.zeros((seq, rest), F32)], axis=1)
    sb = jnp.concatenate([-sin, zeros, jnp.zeros((seq, rest), F32)], axis=1)
    return tuple(jnp.tile(a, (1, DIL_HEADS)) for a in (ct, sa, sb))


def _band_kernel(q_ref, kp_ref, kc_ref, vp_ref, vc_ref, o_ref, lse_ref):
    mt = pl.program_id(2)
    blk = BAND_BLOCK
    nres = q_ref.shape[1]
    nblk = q_ref.shape[2] // blk
    qi = lax.broadcasted_iota(jnp.int32, (blk, 2 * blk), 0)
    kj = lax.broadcasted_iota(jnp.int32, (blk, 2 * blk), 1)
    dist = qi + blk - kj
    in_band = (dist >= 0) & (dist <= blk)
    lane_q = lax.broadcasted_iota(jnp.int32, (blk, DIL_GROUP_WIDTH), 1) // HEAD_DIM
    lane_v = lax.broadcasted_iota(jnp.int32, (2 * blk, DIL_GROUP_WIDTH), 1) // HEAD_DIM
    for r in range(nres):
        kall = jnp.concatenate([kp_ref[0, r], kc_ref[0, r]], axis=0)
        vall = jnp.concatenate([vp_ref[0, r], vc_ref[0, r]], axis=0)
        for i in range(nblk):
            q = q_ref[0, r, i * blk:(i + 1) * blk, :]
            kk = kall[i * blk:(i + 2) * blk]
            vv = vall[i * blk:(i + 2) * blk]
            if i == 0:
                valid = in_band & (kj >= jnp.where(mt > 0, 0, blk))
            else:
                valid = in_band
            acc = jnp.zeros((blk, DIL_GROUP_WIDTH), F32)
            den_full = jnp.ones((blk, DIL_GROUP_WIDTH), F32)
            lse_full = jnp.zeros((blk, DIL_GROUP_WIDTH), F32)
            for h in range(DIL_HEADS):
                qh = jnp.where(lane_q == h, q, jnp.zeros_like(q))
                s = lax.dot_general(qh, kk, (((1,), (1,)), ((), ())), preferred_element_type=F32)
                s = jnp.where(valid, s, NEG)
                m = jnp.max(s, axis=-1, keepdims=True)
                p = jnp.exp(s - m)
                den = jnp.sum(p, axis=-1, keepdims=True)
                vh = jnp.where(lane_v == h, vv, jnp.zeros_like(vv))
                acc = acc + jnp.dot(p.astype(BF16), vh, preferred_element_type=F32)
                den_full = jnp.where(lane_q == h, den, den_full)
                lse_full = jnp.where(lane_q == h, m + jnp.log(den), lse_full)
            o_ref[0, r, i * blk:(i + 1) * blk, :] = acc / den_full
            lse_ref[0, r, i * blk:(i + 1) * blk, :] = lse_full


def _band_attention(q, k, v):
    bsz, dil, ln, w = q.shape
    nblk = min(BAND_BLOCKS_PER_STEP, ln // BAND_BLOCK)
    nres = min(dil, BAND_BLOCKS_PER_STEP // nblk)
    rows = nblk * BAND_BLOCK
    cur = pl.BlockSpec((1, nres, rows, w), lambda b, r, m: (b, r, m, 0))
    prev = pl.BlockSpec((1, nres, BAND_BLOCK, w), lambda b, r, m: (b, r, jnp.maximum(m * nblk - 1, 0), 0))
    return pl.pallas_call(
        _band_kernel,
        out_shape=[jax.ShapeDtypeStruct(q.shape, F32)] * 2,
        grid=(bsz, dil // nres, ln // rows),
        in_specs=[cur, prev, cur, prev, cur],
        out_specs=[cur, cur],
        compiler_params=_params(("parallel", "parallel", "parallel")),
        name=f"band_attn_d{dil}",
    )(q, k, k, v, v)


def _even_out_kernel(tps, u_ref, up_ref, o0, o1, o2, l0, l1, l2, wp_ref, ps_ref, wo_ref, x_ref, gate_ref,
                     g_ref, b_ref, out_ref, il_ref):
    i = pl.program_id(0)
    tm = u_ref.shape[0]
    halo = up_ref.shape[0]
    u = u_ref[...]
    prev = jnp.where(i % tps == 0, jnp.zeros_like(up_ref[...]), up_ref[...])
    ext = jnp.concatenate([prev, u], axis=0)
    lane_g = lax.broadcasted_iota(jnp.int32, (tm, POOL_WIDTH), 1) // POOL_GROUP_DIM
    row = lax.broadcasted_iota(jnp.int32, (tm, POOL_WIDTH), 0)
    pos1 = ((i % tps) * tm + row + 1).astype(F32)
    s = ext
    pooled = jnp.zeros((tm, POOL_WIDTH), F32)
    wvec = jnp.zeros((tm, POOL_WIDTH), F32)
    shift = 1
    for g, w in enumerate(POOL_WINDOWS):
        while shift < w:
            s = s + pltpu.roll(s, shift, 0)
            shift *= 2
        pooled = jnp.where(lane_g == g, s[halo:, :], pooled)
        wvec = jnp.where(lane_g == g, float(w), wvec)
    pooled = pooled / jnp.minimum(pos1, wvec) - u
    a_out = jnp.dot(pooled.astype(BF16), wp_ref[...], preferred_element_type=F32) * ps_ref[...]
    def token_order(ref, buf):
        dil = ref.shape[1]
        if dil == 1:
            return ref[0, 0]
        for r in range(dil):
            blk = ref[0, r]
            for half in range(2):
                buf[half, pl.ds(r, tm // dil, stride=dil), :] = blk[:, half * LANES:(half + 1) * LANES]
        return jnp.concatenate([buf[0], buf[1]], axis=1)

    la, lb, lc = [token_order(ref, il_ref.at[n]) for n, ref in enumerate((l0, l1, l2))]
    oa, ob, oc = [token_order(ref, il_ref.at[3 + n]) for n, ref in enumerate((o0, o1, o2))]
    mx = jnp.maximum(jnp.maximum(la, lb), lc)
    ea, eb, ec = jnp.exp(la - mx), jnp.exp(lb - mx), jnp.exp(lc - mx)
    b_out = (ea * oa + eb * ob + ec * oc) / (ea + eb + ec)
    mixed = jnp.concatenate([a_out, b_out], axis=-1).astype(BF16)
    sub = jnp.dot(mixed, wo_ref[...], preferred_element_type=F32)
    out_ref[...] = _deepnorm_ln(x_ref[...], sub, gate_ref[0], g_ref[...], b_ref[...])


def _even_out(u, os_, lses, wpool_bd, pool_scale, wo_bf, x2, gate, ln_g, ln_b, seq):
    t, d = x2.shape
    tm = TOKEN_TILE
    tps = seq // tm
    halo = 16
    row = lambda w: pl.BlockSpec((tm, w), lambda i: (i, 0))
    full = lambda a: pl.BlockSpec(a.shape, lambda i: (0,) * a.ndim)
    strided = [pl.BlockSpec((1, a.shape[1], tm // a.shape[1], DIL_GROUP_WIDTH), lambda i: (i // tps, 0, i % tps, 0))
               for a in (*os_, *lses)]
    return pl.pallas_call(
        functools.partial(_even_out_kernel, tps),
        out_shape=jax.ShapeDtypeStruct((t, d), F32),
        grid=(t // tm,),
        in_specs=[
            row(POOL_WIDTH),
            pl.BlockSpec((halo, POOL_WIDTH), lambda i: (jnp.maximum(i * (tm // halo) - 1, 0), 0)),
            *strided,
            full(wpool_bd),
            full(pool_scale),
            full(wo_bf),
            row(d),
            pl.BlockSpec((1, 1, d), lambda i: (i // tps, 0, 0)),
            full(ln_g),
            full(ln_b),
        ],
        out_specs=row(d),
        scratch_shapes=[pltpu.VMEM((6, DIL_GROUP_WIDTH // LANES, tm, LANES), F32)],
        compiler_params=_params(("parallel",)),
        name="even_out_proj",
    )(u, u, *os_, *lses, wpool_bd, pool_scale, wo_bf, x2, gate, ln_g, ln_b)


def _swiglu_rows(x, wgu_ref, wd_ref, acc_ref):
    for c in range(D_FF // FF_SUB):
        lo = c * FF_SUB
        gt = jnp.dot(x, wgu_ref[0, :, lo:lo + FF_SUB], preferred_element_type=F32)
        up = jnp.dot(x, wgu_ref[0, :, D_FF + lo:D_FF + lo + FF_SUB], preferred_element_type=F32)
        a = (gt * jax.nn.sigmoid(gt) * up).astype(BF16)
        part = jnp.dot(a, wd_ref[0, lo:lo + FF_SUB, :], preferred_element_type=F32)
        if c == 0:
            acc_ref[...] = part
        else:
            acc_ref[...] += part


def _ffn_kernel(x_ref, sc_ref, sh_ref, wgu_ref, wd_ref, gate_ref, g_ref, b_ref, out_ref, acc_ref):
    _swiglu_rows(_modulate(x_ref[...], sc_ref[0], sh_ref[0]), wgu_ref, wd_ref, acc_ref)
    out_ref[...] = _deepnorm_ln(x_ref[...], acc_ref[...], gate_ref[0], g_ref[...], b_ref[...])


def _ffn(x2, sc, sh, wgu_bf, wd_bf, gate, ln_g, ln_b, seq):
    t, d = x2.shape
    tm = TOKEN_TILE
    tps = seq // tm
    mod = pl.BlockSpec((1, 1, d), lambda i: (i // tps, 0, 0))
    row = pl.BlockSpec((tm, d), lambda i: (i, 0))
    vec = pl.BlockSpec((1, d), lambda i: (0, 0))
    full = lambda a: pl.BlockSpec(a.shape, lambda i: (0,) * a.ndim)
    return pl.pallas_call(
        _ffn_kernel,
        out_shape=jax.ShapeDtypeStruct((t, d), F32),
        grid=(t // tm,),
        in_specs=[row, mod, mod, full(wgu_bf), full(wd_bf), mod, vec, vec],
        out_specs=row,
        scratch_shapes=[pltpu.VMEM((tm, d), F32)],
        compiler_params=_params(("parallel",)),
        name="ffn_dense",
    )(x2, sc, sh, wgu_bf, wd_bf, gate, ln_g, ln_b)


def _split3(x):
    hi = x.astype(BF16)
    r = x - hi.astype(F32)
    mid = r.astype(BF16)
    lo = (r - mid.astype(F32)).astype(BF16)
    return hi, mid, lo


def _odd_in_kernel(tps, x_ref, sc_ref, sh_ref, w_ref, wf_ref, bf_ref, place_ref, q_ref, k_ref, v_ref, f_ref,
                   carry_ref):
    i = pl.program_id(0)
    tm = x_ref.shape[0]
    h = _modulate(x_ref[...], sc_ref[0], sh_ref[0])
    q = jnp.dot(h, w_ref[:, 0:FOX_WIDTH], preferred_element_type=F32)
    q_ref[...] = (q * (HEAD_DIM ** -0.5 * LOG2E)).astype(BF16)
    k_ref[...] = jnp.dot(h, w_ref[:, FOX_WIDTH:2 * FOX_WIDTH], preferred_element_type=F32).astype(BF16)
    v_ref[...] = jnp.dot(h, w_ref[:, 2 * FOX_WIDTH:3 * FOX_WIDTH], preferred_element_type=F32).astype(BF16)
    f = jnp.dot(h, wf_ref[...], preferred_element_type=F32) + bf_ref[...]
    logf = jnp.minimum(f, 0.0) - jnp.log(1.0 + jnp.exp(-jnp.abs(f)))
    r = lax.broadcasted_iota(jnp.int32, (tm, tm), 0)
    cidx = lax.broadcasted_iota(jnp.int32, (tm, tm), 1)
    tri = jnp.where(cidx <= r, 1.0, 0.0).astype(BF16)
    hi, mid, lo = _split3(logf)
    cs = (jnp.dot(tri, lo, preferred_element_type=F32) + jnp.dot(tri, mid, preferred_element_type=F32)
          + jnp.dot(tri, hi, preferred_element_type=F32))

    @pl.when(i % tps == 0)
    def _():
        carry_ref[...] = jnp.zeros_like(carry_ref)

    cs = cs + carry_ref[...]
    carry_ref[...] = cs[tm - 1:tm, :]
    parts = jnp.concatenate(_split3(cs * LOG2E), axis=1)
    f_ref[...] = jnp.dot(parts, place_ref[...], preferred_element_type=F32).astype(BF16)


def _gate_placement():
    place = np.zeros((3 * LANES, FOX_WIDTH), np.float32)
    for h in range(FOX_HEADS):
        base = (h // 2) * LANES + (HEAD_DIM if h % 2 == 0 else 0)
        for p in range(3):
            place[p * LANES + h, base + p] = 1.0
    return jnp.asarray(place, BF16)


def _odd_in(x2, sc, sh, w_bf, wf_bf, bf_pad, seq):
    t, d = x2.shape
    tm = TOKEN_TILE
    tps = seq // tm
    mod = pl.BlockSpec((1, 1, d), lambda i: (i // tps, 0, 0))
    row = lambda w: pl.BlockSpec((tm, w), lambda i: (i, 0))
    full = lambda a: pl.BlockSpec(a.shape, lambda i: (0,) * a.ndim)
    place = _gate_placement()
    return pl.pallas_call(
        functools.partial(_odd_in_kernel, tps),
        out_shape=[jax.ShapeDtypeStruct((t, FOX_WIDTH), BF16)] * 4,
        grid=(t // tm,),
        in_specs=[row(d), mod, mod, full(w_bf), full(wf_bf), full(bf_pad), full(place)],
        out_specs=[row(FOX_WIDTH)] * 4,
        scratch_shapes=[pltpu.VMEM((1, LANES), F32)],
        compiler_params=_params(("arbitrary",)),
        name="odd_in_proj",
    )(x2, sc, sh, w_bf, wf_bf, bf_pad, place)


def _fox_kernel(q_ref, k_ref, v_ref, f_ref, o_ref, acc_ref, m_ref, alpha_ref, s0_ref, s1_ref, p0_ref, p1_ref):
    qi = pl.program_id(2)
    tq = q_ref.shape[1]
    tk = FOX_K_TILE
    half = HEAD_DIM
    q = q_ref[0]
    lane_q = lax.broadcasted_iota(jnp.int32, q.shape, 1)
    lane_k = lax.broadcasted_iota(jnp.int32, (tk, LANES), 1)
    qa = jnp.where(lane_q < half, q, jnp.where(lane_q < half + 3, -1.0, 0.0).astype(BF16))
    qb = jnp.where(lane_q >= half, q, jnp.where(lane_q < 3, -1.0, 0.0).astype(BF16))
    ones_a = jnp.where(lane_k == half, 1.0, 0.0).astype(BF16)
    ones_b = jnp.where(lane_k == 0, 1.0, 0.0).astype(BF16)
    acc_ref[...] = jnp.zeros_like(acc_ref)
    m_ref[...] = jnp.full(m_ref.shape, NEG, F32)

    def scores(j, s_ref, row0=0):
        start = pl.multiple_of(j * tk, tk)
        k = k_ref[0, pl.ds(start, tk), :]
        fa = f_ref[0, pl.ds(start, tk), :]
        ka = jnp.where(lane_k < half, k, fa)
        kb = jnp.where(lane_k >= half, k, fa)
        for hh, (qh, kh) in enumerate(((qa, ka), (qb, kb))):
            s_ref[hh, row0:, :] = lax.dot_general(qh[row0:], kh, (((1,), (1,)), ((), ())),
                                                  preferred_element_type=F32)

    def softmax(j, s_ref, p_ref, row0=0, masked_rows=0):
        start = j * tk
        for hh in range(2):
            for r0 in range(row0, tq, FOX_ROW_CHUNK):
                rows = pl.ds(r0, FOX_ROW_CHUNK)
                s = s_ref[hh, rows, :]
                if r0 < row0 + masked_rows:
                    rr = lax.broadcasted_iota(jnp.int32, s.shape, 0) + (qi * tq + r0)
                    cc = lax.broadcasted_iota(jnp.int32, s.shape, 1) + start
                    s = jnp.where(cc <= rr, s, NEG)
                m_old = m_ref[hh, rows, :]
                m_new = jnp.maximum(m_old, jnp.max(s, axis=-1, keepdims=True))
                p_ref[hh, rows, :] = jnp.exp2(s - jnp.tile(m_new, (1, tk // LANES))).astype(BF16)
                alpha_ref[hh, rows, :] = jnp.exp2(m_old - m_new)
                m_ref[hh, rows, :] = m_new

    def values(j, p_ref, row0=0):
        start = pl.multiple_of(jnp.maximum(j, 0) * tk, tk)
        v = v_ref[0, pl.ds(start, tk), :]
        va = jnp.where(lane_k < half, v, ones_a)
        vb = jnp.where(lane_k >= half, v, ones_b)
        return [jnp.dot(p_ref[hh, row0:, :], vh, preferred_element_type=F32) for hh, vh in enumerate((va, vb))]

    def fold(pv, scale_row0=0):
        for hh in range(2):
            if scale_row0:
                acc_ref[hh, :scale_row0, :] += pv[hh][:scale_row0]
            acc_ref[hh, scale_row0:, :] = ((acc_ref[hh, scale_row0:, :] + pv[hh][scale_row0:])
                                           * alpha_ref[hh, scale_row0:, :])

    p1_ref[...] = jnp.zeros_like(p1_ref)
    scores(0, s0_ref)

    def body(jp, carry):
        pv = values(2 * jp - 1, p1_ref)
        scores(2 * jp + 1, s1_ref)
        softmax(2 * jp, s0_ref, p0_ref)
        fold(pv)
        pv = values(2 * jp, p0_ref)
        scores(2 * jp + 2, s0_ref)
        softmax(2 * jp + 1, s1_ref, p1_ref)
        fold(pv)
        return carry

    lax.fori_loop(0, qi, body, 0)
    pv = values(2 * qi - 1, p1_ref)
    scores(2 * qi + 1, s1_ref, row0=tk)
    softmax(2 * qi, s0_ref, p0_ref, masked_rows=tk)
    fold(pv)
    pv = values(2 * qi, p0_ref)
    softmax(2 * qi + 1, s1_ref, p1_ref, row0=tk, masked_rows=tk)
    fold(pv, scale_row0=tk)
    pv = values(2 * qi + 1, p1_ref, row0=tk)
    for hh in range(2):
        acc_ref[hh, tk:, :] += pv[hh]

    acc_a, acc_b = acc_ref[0], acc_ref[1]
    lane_o = lax.broadcasted_iota(jnp.int32, (tq, LANES), 1)
    o = jnp.where(lane_o < half, acc_a / acc_a[:, half:half + 1], acc_b / acc_b[:, 0:1])
    o_ref[0] = o.astype(BF16)


def _fox_attention(q, k, v, fa, bsz, seq):
    tq = FOX_Q_TILE
    npairs = FOX_HEADS // 2
    tile = pl.BlockSpec((1, tq, LANES), lambda b, j, i: (b, i, j))
    whole = pl.BlockSpec((1, seq, LANES), lambda b, j, i: (b, 0, j))
    return pl.pallas_call(
        _fox_kernel,
        out_shape=jax.ShapeDtypeStruct((bsz, seq, FOX_WIDTH), BF16),
        grid=(bsz, npairs, seq // tq),
        in_specs=[tile, whole, whole, whole],
        out_specs=tile,
        scratch_shapes=[pltpu.VMEM((2, tq, LANES), F32), pltpu.VMEM((2, tq, LANES), F32),
                        pltpu.VMEM((2, tq, LANES), F32),
                        pltpu.VMEM((2, tq, FOX_K_TILE), F32), pltpu.VMEM((2, tq, FOX_K_TILE), F32),
                        pltpu.VMEM((2, tq, FOX_K_TILE), BF16), pltpu.VMEM((2, tq, FOX_K_TILE), BF16)],
        compiler_params=_params(("parallel", "parallel", "arbitrary")),
        name="fox_attn",
    )(q, k, v, fa)


def _odd_out_kernel(o_ref, wo_ref, x_ref, gate_ref, g_ref, b_ref, sc_ref, sh_ref, wrt_ref, tri_ref, out_ref,
                    rankt_ref, cnt_ref, comb_ref, rank_ref):
    sub = jnp.dot(o_ref[...], wo_ref[...], preferred_element_type=F32)
    x_new = _deepnorm_ln(x_ref[...], sub, gate_ref[0], g_ref[...], b_ref[...])
    out_ref[...] = x_new
    _route_chunk(_modulate(x_new, sc_ref[0], sh_ref[0]), wrt_ref, tri_ref, rankt_ref, cnt_ref, comb_ref, rank_ref)


def _odd_out(o2, wo_bf, x2, gate, ln_g, ln_b, sc_next, sh_next, wrt, seq):
    t, d = x2.shape
    tm = MOE_CHUNK
    tps = seq // tm
    row = lambda w: pl.BlockSpec((tm, w), lambda i: (i, 0))
    full = lambda a: pl.BlockSpec(a.shape, lambda i: (0,) * a.ndim)
    mod = pl.BlockSpec((1, 1, d), lambda i: (i // tps, 0, 0))
    tri = jnp.asarray(np.triu(np.ones((tm, tm), np.float32), 1), BF16)
    return pl.pallas_call(
        _odd_out_kernel,
        out_shape=[jax.ShapeDtypeStruct((t, d), F32),
                   jax.ShapeDtypeStruct((N_EXPERTS, t), F32),
                   jax.ShapeDtypeStruct((t // tm, N_EXPERTS, LANES), F32),
                   jax.ShapeDtypeStruct((t, LANES), F32),
                   jax.ShapeDtypeStruct((t, LANES), F32)],
        grid=(t // tm,),
        in_specs=[row(FOX_WIDTH), full(wo_bf), row(d), mod, full(ln_g), full(ln_b), mod, mod, full(wrt), full(tri)],
        out_specs=[row(d),
                   pl.BlockSpec((N_EXPERTS, tm), lambda i: (0, i)),
                   pl.BlockSpec((1, N_EXPERTS, LANES), lambda i: (i, 0, 0)),
                   row(LANES),
                   row(LANES)],
        compiler_params=_params(("parallel",)),
        name="odd_out_proj",
    )(o2, wo_bf, x2, gate, ln_g, ln_b, sc_next, sh_next, wrt, tri)


def _route_chunk(h, wrt_ref, tri_ref, rankt_ref, cnt_ref, comb_ref, rank_ref):
    c = h.shape[0]
    lt = lax.dot_general(wrt_ref[...], h, (((1,), (1,)), ((), ())), preferred_element_type=F32)[0:N_EXPERTS]
    row = lax.broadcasted_iota(jnp.int32, lt.shape, 0)
    m1 = jnp.max(lt, axis=0, keepdims=True)
    i1 = jnp.min(jnp.where(lt == m1, row, N_EXPERTS), axis=0, keepdims=True)
    rest = jnp.where(row == i1, -jnp.inf, lt)
    m2 = jnp.max(rest, axis=0, keepdims=True)
    i2 = jnp.min(jnp.where(rest == m2, row, N_EXPERTS), axis=0, keepdims=True)
    e2 = jnp.exp(m2 - m1)
    g1 = 1.0 / (1.0 + e2)
    g2 = e2 / (1.0 + e2)
    combt = jnp.where(row == i1, g1, jnp.where(row == i2, g2, 0.0))
    sel = jnp.where((row == i1) | (row == i2), 1.0, 0.0)
    sel16 = jnp.concatenate([sel, jnp.zeros_like(sel)], axis=0).astype(BF16)
    earlier = jnp.dot(sel16, tri_ref[...], preferred_element_type=F32)[0:N_EXPERTS]
    rankt = jnp.where(sel > 0.0, earlier, NO_ROW)
    rankt_ref[...] = rankt
    cnt_ref[0] = jnp.broadcast_to(jnp.sum(sel, axis=1, keepdims=True), (N_EXPERTS, LANES))
    pad = LANES - N_EXPERTS
    comb_ref[...] = jnp.concatenate([combt, jnp.zeros((pad, c), F32)], axis=0).T
    rank_ref[...] = jnp.concatenate([rankt, jnp.full((pad, c), NO_ROW, F32)], axis=0).T


def _route_meta(cnt, n_tiles):
    before = jnp.cumsum(cnt, axis=0) - cnt
    total = jnp.sum(cnt, axis=0)
    region = ((total + MOE_TILE - 1) // MOE_TILE) * MOE_TILE
    base = jnp.cumsum(region) - region
    n_used = (jnp.sum(region) // MOE_TILE).reshape(1)
    starts = jnp.arange(n_tiles, dtype=jnp.int32) * MOE_TILE
    tile_expert = jnp.minimum(jnp.sum(starts[:, None] >= (base + region)[None, :], axis=1), N_EXPERTS - 1)
    i32 = lambda a: a.astype(jnp.int32)
    return i32(before.reshape(-1)), i32(cnt.reshape(-1)), i32(base), i32(total), i32(tile_expert), i32(n_used)


def _chunk_block(before_ref, cnt_ref, chunk, e, j):
    g = before_ref[chunk * N_EXPERTS + e]
    n = cnt_ref[chunk * N_EXPERTS + e]
    k = g // ROW_BLOCK + j
    return g, n, k, (n > 0) & (k * ROW_BLOCK < g + n)


def _dispatch_kernel(before_ref, cnt_ref, base_ref, total_ref, x_ref, sc_ref, sh_ref, rankt_ref, xs_ref,
                     stage_ref, slot_ref, zero_ref, sem, fsem, zsem, usem):
    c = pl.program_id(0)
    last = c == pl.num_programs(0) - 1
    h = _modulate(x_ref[...], sc_ref[0], sh_ref[0])

    @pl.when(c == 0)
    def _():
        stage_ref[...] = jnp.zeros_like(stage_ref)
        zero_ref[...] = jnp.zeros_like(zero_ref)

    block_row = lax.broadcasted_iota(jnp.int32, (ROW_BLOCK, MOE_CHUNK), 0).astype(F32)
    pair_row = lax.broadcasted_iota(jnp.int32, (2 * ROW_BLOCK, MOE_CHUNK), 0).astype(F32)

    def block_copy(e, j, k):
        row0 = pl.multiple_of(base_ref[e] + k * ROW_BLOCK, ROW_BLOCK)
        return pltpu.make_async_copy(slot_ref.at[e, j], xs_ref.at[pl.ds(row0, ROW_BLOCK)], sem.at[e, j])

    def any_beyond_two(chunk):
        flag = _chunk_block(before_ref, cnt_ref, chunk, 0, 2)[3]
        for e in range(1, N_EXPERTS):
            flag = flag | _chunk_block(before_ref, cnt_ref, chunk, e, 2)[3]
        return flag

    def for_blocks(chunk, fn):
        for e in range(N_EXPERTS):
            for j in range(2):
                fn(e, j)

        @pl.when(any_beyond_two(chunk))
        def _():
            for e in range(N_EXPERTS):
                for j in range(2, MAX_SPAN):
                    fn(e, j)

    def wait_block_copies(chunk):
        def wait_one(e, j):
            g, n, k, touched = _chunk_block(before_ref, cnt_ref, chunk, e, j)

            @pl.when(touched & (g + n >= (k + 1) * ROW_BLOCK))
            def _():
                block_copy(e, j, k).wait()

        for_blocks(chunk, wait_one)

    selectors = []
    for e in range(N_EXPERTS):
        g = before_ref[c * N_EXPERTS + e]
        off = (g % ROW_BLOCK).astype(F32)
        selectors.append(jnp.where(rankt_ref[e:e + 1, :] + off == pair_row, 1.0, 0.0).astype(BF16))
    first_two = jnp.dot(jnp.concatenate(selectors, axis=0), h, preferred_element_type=F32).astype(BF16)

    @pl.when(c > 0)
    def _():
        wait_block_copies(c - 1)

    def emit_block(e, j):
        g, n, k, touched = _chunk_block(before_ref, cnt_ref, c, e, j)
        complete = g + n >= (k + 1) * ROW_BLOCK

        @pl.when(touched)
        def _():
            if j < 2:
                r0 = (2 * e + j) * ROW_BLOCK
                blk = first_two[r0:r0 + ROW_BLOCK]
            else:
                off = (g - k * ROW_BLOCK).astype(F32)
                onehot = jnp.where(rankt_ref[e:e + 1, :] + off == block_row, 1.0, 0.0).astype(BF16)
                blk = jnp.dot(onehot, h, preferred_element_type=F32).astype(BF16)
            if j == 0:
                blk = (blk.astype(F32) + stage_ref[e].astype(F32)).astype(BF16)
            slot_ref[e, j] = blk
            stage_ref[e] = jnp.where(complete, jnp.zeros_like(blk), blk)

            @pl.when(complete)
            def _():
                block_copy(e, j, k).start()

    for_blocks(c, emit_block)

    def tail_copies(e):
        total = total_ref[e]
        n_data = (total + ROW_BLOCK - 1) // ROW_BLOCK
        n_region = ((total + MOE_TILE - 1) // MOE_TILE) * (MOE_TILE // ROW_BLOCK)
        part_row = pl.multiple_of(base_ref[e] + (total // ROW_BLOCK) * ROW_BLOCK, ROW_BLOCK)
        part = pltpu.make_async_copy(stage_ref.at[e], xs_ref.at[pl.ds(part_row, ROW_BLOCK)], fsem.at[e])
        out = [(total % ROW_BLOCK != 0, part)]
        for z in range(MOE_TILE // ROW_BLOCK - 1):
            zrow = pl.multiple_of(base_ref[e] + (n_data + z) * ROW_BLOCK, ROW_BLOCK)
            zc = pltpu.make_async_copy(zero_ref, xs_ref.at[pl.ds(zrow, ROW_BLOCK)], zsem.at[e, z])
            out.append((n_data + z < n_region, zc))
        return out

    def unused_copies():
        e = N_EXPERTS - 1
        used = base_ref[e] + ((total_ref[e] + MOE_TILE - 1) // MOE_TILE) * MOE_TILE
        out = []
        for z in range(usem.shape[0]):
            zrow = pl.multiple_of(used + z * ROW_BLOCK, ROW_BLOCK)
            zc = pltpu.make_async_copy(zero_ref, xs_ref.at[pl.ds(zrow, ROW_BLOCK)], usem.at[z])
            out.append((zrow < xs_ref.shape[0], zc))
        return out

    def all_tail_copies():
        return [cc for e in range(N_EXPERTS) for cc in tail_copies(e)] + unused_copies()

    @pl.when(last)
    def _():
        for cond, cp in all_tail_copies():
            @pl.when(cond)
            def _():
                cp.start()

    @pl.when(last)
    def _():
        wait_block_copies(c)

    @pl.when(last)
    def _():
        for cond, cp in all_tail_copies():
            @pl.when(cond)
            def _():
                cp.wait()


def _dispatch(meta, x2, sc, sh, rankt, n_tiles, seq):
    before, cnt, base, total = meta
    t, d = x2.shape
    cs = MOE_CHUNK
    cps = seq // cs
    mod = pl.BlockSpec((1, 1, d), lambda i, *_: (i // cps, 0, 0))
    grid_spec = pltpu.PrefetchScalarGridSpec(
        num_scalar_prefetch=4,
        grid=(t // cs,),
        in_specs=[pl.BlockSpec((cs, d), lambda i, *_: (i, 0)), mod, mod,
                  pl.BlockSpec((N_EXPERTS, cs), lambda i, *_: (0, i))],
        out_specs=pl.BlockSpec(memory_space=pl.ANY),
        scratch_shapes=[pltpu.VMEM((N_EXPERTS, ROW_BLOCK, d), BF16),
                        pltpu.VMEM((N_EXPERTS, MAX_SPAN, ROW_BLOCK, d), BF16),
                        pltpu.VMEM((ROW_BLOCK, d), BF16),
                        pltpu.SemaphoreType.DMA((N_EXPERTS, MAX_SPAN)),
                        pltpu.SemaphoreType.DMA((N_EXPERTS,)),
                        pltpu.SemaphoreType.DMA((N_EXPERTS, MOE_TILE // ROW_BLOCK - 1)),
                        pltpu.SemaphoreType.DMA(((n_tiles * MOE_TILE - 2 * t) // ROW_BLOCK,))],
    )
    return pl.pallas_call(
        _dispatch_kernel,
        out_shape=jax.ShapeDtypeStruct((n_tiles * MOE_TILE, d), BF16),
        grid_spec=grid_spec,
        compiler_params=_params(("arbitrary",)),
        name="moe_dispatch",
    )(before, cnt, base, total, x2, sc, sh, rankt)


def _experts_kernel(te_ref, nu_ref, x_ref, wgu_ref, wd_ref, y_ref, acc_ref):
    @pl.when(pl.program_id(0) >= nu_ref[0])
    def _():
        y_ref[...] = jnp.zeros_like(y_ref)

    @pl.when(pl.program_id(0) < nu_ref[0])
    def _():
        _swiglu_rows(x_ref[...], wgu_ref, wd_ref, acc_ref)
        y_ref[...] = acc_ref[...].astype(BF16)


def _experts(tile_expert, n_used, xs, wgu_bf, wd_bf):
    rows, d = xs.shape
    tm = MOE_TILE
    clamp = lambda i, nu: jnp.minimum(i, nu[0] - 1)
    grid_spec = pltpu.PrefetchScalarGridSpec(
        num_scalar_prefetch=2,
        grid=(rows // tm,),
        in_specs=[pl.BlockSpec((tm, d), lambda i, te, nu: (clamp(i, nu), 0)),
                  pl.BlockSpec((1, d, 2 * D_FF), lambda i, te, nu: (te[clamp(i, nu)], 0, 0)),
                  pl.BlockSpec((1, D_FF, d), lambda i, te, nu: (te[clamp(i, nu)], 0, 0))],
        out_specs=pl.BlockSpec((tm, d), lambda i, te, nu: (i, 0)),
        scratch_shapes=[pltpu.VMEM((tm, d), F32)],
    )
    return pl.pallas_call(
        _experts_kernel,
        out_shape=jax.ShapeDtypeStruct((rows, d), BF16),
        grid_spec=grid_spec,
        compiler_params=_params(("arbitrary",)),
        name="moe_experts",
    )(tile_expert, n_used, xs, wgu_bf, wd_bf)


def _combine_kernel(before_ref, cnt_ref, base_ref, rank_ref, comb_ref, x_ref, gate_ref, g_ref, b_ref, y_ref,
                    out_ref, ybuf, xbuf, acc_ref, sem, xsem):
    c = pl.program_id(0)
    slot = c % 2

    def window(chunk, e, w):
        g = before_ref[chunk * N_EXPERTS + e]
        n = cnt_ref[chunk * N_EXPERTS + e]
        first = (g // ROW_BLOCK) * ROW_BLOCK + w * COMBINE_WIN
        row0 = pl.multiple_of(jnp.minimum(base_ref[e] + first, y_ref.shape[0] - COMBINE_WIN), ROW_BLOCK)
        off = (base_ref[e] + g - row0).astype(F32)
        return (n > 0) & (first < g + n), row0, off

    def first_copy(chunk, sl, e):
        touched, row0, _ = window(chunk, e, 0)
        return touched, pltpu.make_async_copy(y_ref.at[pl.ds(row0, COMBINE_WIN)], ybuf.at[sl, e], sem.at[sl, e])

    def later_copy(chunk, sl, e, w):
        touched, row0, _ = window(chunk, e, w)
        return touched, pltpu.make_async_copy(y_ref.at[pl.ds(row0, COMBINE_WIN)], xbuf.at[sl, e, w - 1],
                                              xsem.at[sl, e, w - 1])

    def any_later(chunk):
        flag = window(chunk, 0, 1)[0]
        for e in range(1, N_EXPERTS):
            flag = flag | window(chunk, e, 1)[0]
        return flag

    def fetch(chunk, sl):
        for e in range(N_EXPERTS):
            touched, cp = first_copy(chunk, sl, e)

            @pl.when(touched)
            def _():
                cp.start()

        @pl.when(any_later(chunk))
        def _():
            for e in range(N_EXPERTS):
                for w in range(1, COMBINE_MAX_WIN):
                    touched, cp = later_copy(chunk, sl, e, w)

                    @pl.when(touched)
                    def _():
                        cp.start()

    @pl.when(c == 0)
    def _():
        ybuf[...] = jnp.zeros_like(ybuf)
        fetch(0, 0)

    @pl.when(c + 1 < pl.num_programs(0))
    def _():
        fetch(c + 1, 1 - slot)

    lane = lax.broadcasted_iota(jnp.int32, (MOE_CHUNK, COMBINE_WIN), 1).astype(F32)
    acc_ref[...] = jnp.zeros_like(acc_ref)
    for e in range(N_EXPERTS):
        col = rank_ref[:, e:e + 1]
        weight = comb_ref[:, e:e + 1]
        touched, _, off = window(c, e, 0)

        @pl.when(touched)
        def _():
            first_copy(c, slot, e)[1].wait()

        onehot = jnp.where(col + off == lane, 1.0, 0.0).astype(BF16)
        acc_ref[...] += weight * jnp.dot(onehot, ybuf[slot, e], preferred_element_type=F32)

    @pl.when(any_later(c))
    def _():
        for e in range(N_EXPERTS):
            for w in range(1, COMBINE_MAX_WIN):
                touched, _, off = window(c, e, w)

                @pl.when(touched)
                def _():
                    later_copy(c, slot, e, w)[1].wait()
                    onehot = jnp.where(rank_ref[:, e:e + 1] + off == lane, 1.0, 0.0).astype(BF16)
                    acc_ref[...] += comb_ref[:, e:e + 1] * jnp.dot(onehot, xbuf[slot, e, w - 1],
                                                                   preferred_element_type=F32)

    out_ref[...] = _deepnorm_ln(x_ref[...], acc_ref[...], gate_ref[0], g_ref[...], b_ref[...])


def _combine(meta, rank, comb, x2, gate, ln_g, ln_b, y, seq):
    before, cnt, base, _ = meta
    t, d = x2.shape
    cs = MOE_CHUNK
    cps = seq // cs
    row = lambda w: pl.BlockSpec((cs, w), lambda i, *_: (i, 0))
    vec = pl.BlockSpec((1, d), lambda i, *_: (0, 0))
    grid_spec = pltpu.PrefetchScalarGridSpec(
        num_scalar_prefetch=3,
        grid=(t // cs,),
        in_specs=[row(LANES), row(LANES), row(d), pl.BlockSpec((1, 1, d), lambda i, *_: (i // cps, 0, 0)),
                  vec, vec, pl.BlockSpec(memory_space=pl.ANY)],
        out_specs=row(d),
        scratch_shapes=[pltpu.VMEM((2, N_EXPERTS, COMBINE_WIN, d), BF16),
                        pltpu.VMEM((2, N_EXPERTS, COMBINE_MAX_WIN - 1, COMBINE_WIN, d), BF16),
                        pltpu.VMEM((cs, d), F32),
                        pltpu.SemaphoreType.DMA((2, N_EXPERTS)),
                        pltpu.SemaphoreType.DMA((2, N_EXPERTS, COMBINE_MAX_WIN - 1))],
    )
    return pl.pallas_call(
        _combine_kernel,
        out_shape=jax.ShapeDtypeStruct((t, d), F32),
        grid_spec=grid_spec,
        compiler_params=_params(("arbitrary",)),
        name="moe_combine",
    )(before, cnt, base, rank, comb, x2, gate, ln_g, ln_b, y)


def _block_diag(w_pool):
    g, c, _ = w_pool.shape
    out = jnp.zeros((g * c, g * c), w_pool.dtype)
    for i in range(g):
        out = out.at[i * c:(i + 1) * c, i * c:(i + 1) * c].set(w_pool[i])
    return out


def kernel(x, c, w_ada, b_ada, ln_g, ln_b, w_in_even, w_pool, pool_scale, w_out_even, w_ffn_gu, w_ffn_down,
           w_in_odd, b_forget, w_out_odd, w_router, w_exp_gu, w_exp_down):
    bsz, seq, d = x.shape
    t = bsz * seq
    tabs = _rope_tables(seq)
    ada = _ada(c, w_ada, b_ada)
    x2 = x.reshape(t, d)
    for l in range(DEPTH):
        i = l // 2
        sh1, sc1, g1, sh2, sc2, g2 = [ada[l, :, n * d:(n + 1) * d].reshape(bsz, 1, d) for n in range(6)]
        lng = ln_g[l].reshape(2, 1, d)
        lnb = ln_b[l].reshape(2, 1, d)
        if l % 2 == 0:
            outs = _even_in(x2, sc1, sh1, w_in_even[i].astype(BF16), tabs, seq)
            u, qkv = outs[0], outs[1:]
            os_, lses = [], []
            for g in range(len(DIL_CONFIGS)):
                o, lse = _band_attention(qkv[g], qkv[3 + g], qkv[6 + g])
                os_.append(o)
                lses.append(lse)
            x2 = _even_out(u, os_, lses, _block_diag(w_pool[i]).astype(BF16), pool_scale[i].reshape(1, -1),
                           w_out_even[i].astype(BF16), x2, g1, lng[0], lnb[0], seq)
            x2 = _ffn(x2, sc2, sh2, w_ffn_gu[i][None].astype(BF16), w_ffn_down[i][None].astype(BF16), g2, lng[1],
                      lnb[1], seq)
        else:
            w_in = w_in_odd[i]
            wf = jnp.zeros((d, LANES), F32).at[:, :FOX_HEADS].set(w_in[:, 3 * FOX_WIDTH:]).astype(BF16)
            bfp = jnp.zeros((1, LANES), F32).at[0, :FOX_HEADS].set(b_forget[i])
            q, k, v, fa = [a.reshape(bsz, seq, FOX_WIDTH) for a in
                           _odd_in(x2, sc1, sh1, w_in[:, :3 * FOX_WIDTH].astype(BF16), wf, bfp, seq)]
            o = _fox_attention(q, k, v, fa, bsz, seq)
            wrt = jnp.zeros((2 * N_EXPERTS, d), F32).at[:N_EXPERTS].set(w_router[i].T).astype(BF16)
            x2, rankt, cnt, comb, rank = _odd_out(o.reshape(t, FOX_WIDTH), w_out_odd[i].astype(BF16), x2, g1, lng[0],
                                                  lnb[0], sc2, sh2, wrt, seq)
            n_tiles = (2 * t + N_EXPERTS * (MOE_TILE - 1) + MOE_TILE - 1) // MOE_TILE
            *meta, tile_expert, n_used = _route_meta(cnt[:, :, 0].astype(jnp.int32), n_tiles)
            xs = _dispatch(meta, x2, sc2, sh2, rankt, n_tiles, seq)
            y = _experts(tile_expert, n_used, xs, w_exp_gu[i].astype(BF16), w_exp_down[i].astype(BF16))
            x2 = _combine(meta, rank, comb, x2, g2, lng[1], lnb[1], y, seq)
    return x2.reshape(bsz, seq, d)
```

```python
import functools
import math

import numpy as np

import jax
import jax.numpy as jnp
from jax import lax
from jax.experimental import pallas as pl
from jax.experimental.pallas import tpu as pltpu

F32 = jnp.float32
BF16 = jnp.bfloat16

D_MODEL = 1024
DEPTH = 4
HEAD_DIM = 64
POOL_WINDOWS = (2, 4, 8, 16)
POOL_GROUP_DIM = 64
POOL_WIDTH = 256
DIL_CONFIGS = ((128, 1), (512, 4), (2048, 16))
DIL_HEADS = 4
DIL_GROUP_WIDTH = DIL_HEADS * HEAD_DIM
BAND_BLOCK = 128
BAND_BLOCKS_PER_STEP = 8
FOX_HEADS = 16
FOX_WIDTH = 1024
ROPE_THETA = 500000.0
ROPE_DIM = 16
D_FF = 2816
N_EXPERTS = 8
DEEPNORM_ALPHA = (2 * DEPTH) ** 0.25
LN_EPS = 1e-5
NEG = -1e30
LOG2E = 1.4426950408889634

LANES = 128
TOKEN_TILE = 512
FOX_K_TILE = 512
FOX_Q_TILE = 2 * FOX_K_TILE
FOX_ROW_CHUNK = 64
MOE_CHUNK = 512
MOE_TILE = 512
ROW_BLOCK = 128
MAX_SPAN = MOE_CHUNK // ROW_BLOCK + 1
FF_SUB = 256
COMBINE_WIN = 2 * ROW_BLOCK
COMBINE_MAX_WIN = (ROW_BLOCK - 1 + MOE_CHUNK + COMBINE_WIN - 1) // COMBINE_WIN
NO_ROW = -1e9
VMEM_LIMIT = 56 * 1024 * 1024


def _params(sem, vmem=VMEM_LIMIT):
    return pltpu.CompilerParams(dimension_semantics=sem, vmem_limit_bytes=vmem)


def _modulate(x, sc, sh):
    return (x * (1.0 + sc) + sh).astype(BF16)


def _deepnorm_ln(x, sub, gate, g, b):
    y = DEEPNORM_ALPHA * x + gate * sub
    mu = jnp.mean(y, axis=-1, keepdims=True)
    yc = y - mu
    var = jnp.mean(yc * yc, axis=-1, keepdims=True)
    return yc * lax.rsqrt(var + LN_EPS) * g + b


def _ada_kernel(c_ref, w_ref, b_ref, o_ref):
    c = c_ref[...]
    ca = (c * jax.nn.sigmoid(c)).astype(BF16)
    o_ref[0] = jnp.dot(ca, w_ref[0].astype(BF16), preferred_element_type=F32) + b_ref[0]


def _ada(c, w_ada, b_ada):
    depth, d, n = w_ada.shape
    bsz = c.shape[0]
    tn = 1024
    return pl.pallas_call(
        _ada_kernel,
        out_shape=jax.ShapeDtypeStruct((depth, bsz, n), F32),
        grid=(depth, n // tn),
        in_specs=[
            pl.BlockSpec((bsz, d), lambda l, j: (0, 0)),
            pl.BlockSpec((1, d, tn), lambda l, j: (l, 0, j)),
            pl.BlockSpec((1, 1, tn), lambda l, j: (l, 0, j)),
        ],
        out_specs=pl.BlockSpec((1, bsz, tn), lambda l, j: (l, 0, j)),
        compiler_params=_params(("parallel", "parallel")),
        name="ada_mod",
    )(c, w_ada, b_ada.reshape(depth, 1, n))


def _even_in_kernel(x_ref, sc_ref, sh_ref, w_ref, ct_ref, sa_ref, sb_ref, u_ref, *rest):
    qkv_refs, z_ref = rest[:9], rest[9]
    tm = x_ref.shape[0]
    h = _modulate(x_ref[...], sc_ref[0], sh_ref[0])
    u_ref[...] = jnp.dot(h, w_ref[:, 0:POOL_WIDTH], preferred_element_type=F32)
    ct, sa, sb = ct_ref[...], sa_ref[...], sb_ref[...]
    for n in range(9):
        lo = POOL_WIDTH + n * DIL_GROUP_WIDTH
        z = jnp.dot(h, w_ref[:, lo:lo + DIL_GROUP_WIDTH], preferred_element_type=F32)
        if n < 6:
            z = z * ct + pltpu.roll(z, 8, 1) * sa + pltpu.roll(z, DIL_GROUP_WIDTH - 8, 1) * sb
        if n < 3:
            z = z * (HEAD_DIM ** -0.5)
        dil = DIL_CONFIGS[n % 3][1]
        if dil == 1:
            qkv_refs[n][0, 0] = z.astype(BF16)
        else:
            for half in range(2):
                z_ref[half] = z[:, half * LANES:(half + 1) * LANES]
            for r in range(dil):
                rows = [z_ref[half, pl.ds(r, tm // dil, stride=dil), :] for half in range(2)]
                qkv_refs[n][0, r] = jnp.concatenate(rows, axis=1).astype(BF16)


def _even_in(x2, sc, sh, w_bf, tabs, seq):
    t, d = x2.shape
    tm = TOKEN_TILE
    tps = seq // tm
    bsz = t // seq
    n_in = w_bf.shape[1]
    mod_spec = pl.BlockSpec((1, 1, d), lambda i: (i // tps, 0, 0))
    tab_spec = pl.BlockSpec((tm, DIL_GROUP_WIDTH), lambda i: (i % tps, 0))
    out_spec = pl.BlockSpec((tm, DIL_GROUP_WIDTH), lambda i: (i, 0))
    dils = [dil for _, dil in DIL_CONFIGS] * 3
    qkv_shapes = [jax.ShapeDtypeStruct((bsz, dil, seq // dil, DIL_GROUP_WIDTH), BF16) for dil in dils]
    qkv_specs = [pl.BlockSpec((1, dil, tm // dil, DIL_GROUP_WIDTH), lambda i: (i // tps, 0, i % tps, 0))
                 for dil in dils]
    return pl.pallas_call(
        _even_in_kernel,
        out_shape=[jax.ShapeDtypeStruct((t, POOL_WIDTH), F32)] + qkv_shapes,
        grid=(t // tm,),
        in_specs=[
            pl.BlockSpec((tm, d), lambda i: (i, 0)),
            mod_spec,
            mod_spec,
            pl.BlockSpec((d, n_in), lambda i: (0, 0)),
            tab_spec,
            tab_spec,
            tab_spec,
        ],
        out_specs=[out_spec] + qkv_specs,
        scratch_shapes=[pltpu.VMEM((DIL_GROUP_WIDTH // LANES, tm, LANES), F32)],
        compiler_params=_params(("parallel",)),
        name="even_in_proj",
    )(x2, sc, sh, w_bf, *tabs)


def _rope_tables(seq):
    pos = jnp.arange(seq, dtype=F32)
    inv = ROPE_THETA ** (-jnp.arange(0, ROPE_DIM, 2, dtype=F32) / ROPE_DIM)
    ang = pos[:, None] * inv[None, :]
    cos, sin = jnp.cos(ang), jnp.sin(ang)
    half = ROPE_DIM // 2
    zeros = jnp.zeros((seq, half), F32)
    rest = HEAD_DIM - ROPE_DIM
    ct = jnp.concatenate([cos, cos, jnp.ones((seq, rest), F32)], axis=1)
    sa = jnp.concatenate([zeros, sin, jnp.zeros((seq, rest), F32)], axis=1)
    sb = jnp.concatenate([-sin, zeros, jnp.zeros((seq, rest), F32)], axis=1)
    return tuple(jnp.tile(a, (1, DIL_HEADS)) for a in (ct, sa, sb))


def _band_kernel(q_ref, kp_ref, kc_ref, vp_ref, vc_ref, o_ref, lse_ref):
    mt = pl.program_id(2)
    blk = BAND_BLOCK
    nres = q_ref.shape[1]
    nblk = q_ref.shape[2] // blk
    qi = lax.broadcasted_iota(jnp.int32, (blk, 2 * blk), 0)
    kj = lax.broadcasted_iota(jnp.int32, (blk, 2 * blk), 1)
    dist = qi + blk - kj
    in_band = (dist >= 0) & (dist <= blk)
    lane_q = lax.broadcasted_iota(jnp.int32, (blk, DIL_GROUP_WIDTH), 1) // HEAD_DIM
    lane_v = lax.broadcasted_iota(jnp.int32, (2 * blk, DIL_GROUP_WIDTH), 1) // HEAD_DIM
    for r in range(nres):
        kall = jnp.concatenate([kp_ref[0, r], kc_ref[0, r]], axis=0)
        vall = jnp.concatenate([vp_ref[0, r], vc_ref[0, r]], axis=0)
        for i in range(nblk):
            q = q_ref[0, r, i * blk:(i + 1) * blk, :]
            kk = kall[i * blk:(i + 2) * blk]
            vv = vall[i * blk:(i + 2) * blk]
            if i == 0:
                valid = in_band & (kj >= jnp.where(mt > 0, 0, blk))
            else:
                valid = in_band
            acc = jnp.zeros((blk, DIL_GROUP_WIDTH), F32)
            den_full = jnp.ones((blk, DIL_GROUP_WIDTH), F32)
            lse_full = jnp.zeros((blk, DIL_GROUP_WIDTH), F32)
            for h in range(DIL_HEADS):
                qh = jnp.where(lane_q == h, q, jnp.zeros_like(q))
                s = lax.dot_general(qh, kk, (((1,), (1,)), ((), ())), preferred_element_type=F32)
                s = jnp.where(valid, s, NEG)
                m = jnp.max(s, axis=-1, keepdims=True)
                p = jnp.exp(s - m)
                den = jnp.sum(p, axis=-1, keepdims=True)
                vh = jnp.where(lane_v == h, vv, jnp.zeros_like(vv))
                acc = acc + jnp.dot(p.astype(BF16), vh, preferred_element_type=F32)
                den_full = jnp.where(lane_q == h, den, den_full)
                lse_full = jnp.where(lane_q == h, m + jnp.log(den), lse_full)
            o_ref[0, r, i * blk:(i + 1) * blk, :] = acc / den_full
            lse_ref[0, r, i * blk:(i + 1) * blk, :] = lse_full


def _band_attention(q, k, v):
    bsz, dil, ln, w = q.shape
    nblk = min(BAND_BLOCKS_PER_STEP, ln // BAND_BLOCK)
    nres = min(dil, BAND_BLOCKS_PER_STEP // nblk)
    rows = nblk * BAND_BLOCK
    cur = pl.BlockSpec((1, nres, rows, w), lambda b, r, m: (b, r, m, 0))
    prev = pl.BlockSpec((1, nres, BAND_BLOCK, w), lambda b, r, m: (b, r, jnp.maximum(m * nblk - 1, 0), 0))
    return pl.pallas_call(
        _band_kernel,
        out_shape=[jax.ShapeDtypeStruct(q.shape, F32)] * 2,
        grid=(bsz, dil // nres, ln // rows),
        in_specs=[cur, prev, cur, prev, cur],
        out_specs=[cur, cur],
        compiler_params=_params(("parallel", "parallel", "parallel")),
        name=f"band_attn_d{dil}",
    )(q, k, k, v, v)


def _even_out_kernel(tps, u_ref, up_ref, o0, o1, o2, l0, l1, l2, wp_ref, ps_ref, wo_ref, x_ref, gate_ref,
                     g_ref, b_ref, out_ref, il_ref):
    i = pl.program_id(0)
    tm = u_ref.shape[0]
    halo = up_ref.shape[0]
    u = u_ref[...]
    prev = jnp.where(i % tps == 0, jnp.zeros_like(up_ref[...]), up_ref[...])
    ext = jnp.concatenate([prev, u], axis=0)
    lane_g = lax.broadcasted_iota(jnp.int32, (tm, POOL_WIDTH), 1) // POOL_GROUP_DIM
    row = lax.broadcasted_iota(jnp.int32, (tm, POOL_WIDTH), 0)
    pos1 = ((i % tps) * tm + row + 1).astype(F32)
    s = ext
    pooled = jnp.zeros((tm, POOL_WIDTH), F32)
    wvec = jnp.zeros((tm, POOL_WIDTH), F32)
    shift = 1
    for g, w in enumerate(POOL_WINDOWS):
        while shift < w:
            s = s + pltpu.roll(s, shift, 0)
            shift *= 2
        pooled = jnp.where(lane_g == g, s[halo:, :], pooled)
        wvec = jnp.where(lane_g == g, float(w), wvec)
    pooled = pooled / jnp.minimum(pos1, wvec) - u
    a_out = jnp.dot(pooled.astype(BF16), wp_ref[...], preferred_element_type=F32) * ps_ref[...]
    def token_order(ref, buf):
        dil = ref.shape[1]
        if dil == 1:
            return ref[0, 0]
        for r in range(dil):
            blk = ref[0, r]
            for half in range(2):
                buf[half, pl.ds(r, tm // dil, stride=dil), :] = blk[:, half * LANES:(half + 1) * LANES]
        return jnp.concatenate([buf[0], buf[1]], axis=1)

    la, lb, lc = [token_order(ref, il_ref.at[n]) for n, ref in enumerate((l0, l1, l2))]
    oa, ob, oc = [token_order(ref, il_ref.at[3 + n]) for n, ref in enumerate((o0, o1, o2))]
    mx = jnp.maximum(jnp.maximum(la, lb), lc)
    ea, eb, ec = jnp.exp(la - mx), jnp.exp(lb - mx), jnp.exp(lc - mx)
    b_out = (ea * oa + eb * ob + ec * oc) / (ea + eb + ec)
    mixed = jnp.concatenate([a_out, b_out], axis=-1).astype(BF16)
    sub = jnp.dot(mixed, wo_ref[...], preferred_element_type=F32)
    out_ref[...] = _deepnorm_ln(x_ref[...], sub, gate_ref[0], g_ref[...], b_ref[...])


def _even_out(u, os_, lses, wpool_bd, pool_scale, wo_bf, x2, gate, ln_g, ln_b, seq):
    t, d = x2.shape
    tm = TOKEN_TILE
    tps = seq // tm
    halo = 16
    row = lambda w: pl.BlockSpec((tm, w), lambda i: (i, 0))
    full = lambda a: pl.BlockSpec(a.shape, lambda i: (0,) * a.ndim)
    strided = [pl.BlockSpec((1, a.shape[1], tm // a.shape[1], DIL_GROUP_WIDTH), lambda i: (i // tps, 0, i % tps, 0))
               for a in (*os_, *lses)]
    return pl.pallas_call(
        functools.partial(_even_out_kernel, tps),
        out_shape=jax.ShapeDtypeStruct((t, d), F32),
        grid=(t // tm,),
        in_specs=[
            row(POOL_WIDTH),
            pl.BlockSpec((halo, POOL_WIDTH), lambda i: (jnp.maximum(i * (tm // halo) - 1, 0), 0)),
            *strided,
            full(wpool_bd),
            full(pool_scale),
            full(wo_bf),
            row(d),
            pl.BlockSpec((1, 1, d), lambda i: (i // tps, 0, 0)),
            full(ln_g),
            full(ln_b),
        ],
        out_specs=row(d),
        scratch_shapes=[pltpu.VMEM((6, DIL_GROUP_WIDTH // LANES, tm, LANES), F32)],
        compiler_params=_params(("parallel",)),
        name="even_out_proj",
    )(u, u, *os_, *lses, wpool_bd, pool_scale, wo_bf, x2, gate, ln_g, ln_b)


def _swiglu_rows(x, wgu_ref, wd_ref, acc_ref):
    for c in range(D_FF // FF_SUB):
        lo = c * FF_SUB
        gt = jnp.dot(x, wgu_ref[0, :, lo:lo + FF_SUB], preferred_element_type=F32)
        up = jnp.dot(x, wgu_ref[0, :, D_FF + lo:D_FF + lo + FF_SUB], preferred_element_type=F32)
        a = (gt * jax.nn.sigmoid(gt) * up).astype(BF16)
        part = jnp.dot(a, wd_ref[0, lo:lo + FF_SUB, :], preferred_element_type=F32)
        if c == 0:
            acc_ref[...] = part
        else:
            acc_ref[...] += part


def _ffn_kernel(x_ref, sc_ref, sh_ref, wgu_ref, wd_ref, gate_ref, g_ref, b_ref, out_ref, acc_ref):
    _swiglu_rows(_modulate(x_ref[...], sc_ref[0], sh_ref[0]), wgu_ref, wd_ref, acc_ref)
    out_ref[...] = _deepnorm_ln(x_ref[...], acc_ref[...], gate_ref[0], g_ref[...], b_ref[...])


def _ffn(x2, sc, sh, wgu_bf, wd_bf, gate, ln_g, ln_b, seq):
    t, d = x2.shape
    tm = TOKEN_TILE
    tps = seq // tm
    mod = pl.BlockSpec((1, 1, d), lambda i: (i // tps, 0, 0))
    row = pl.BlockSpec((tm, d), lambda i: (i, 0))
    vec = pl.BlockSpec((1, d), lambda i: (0, 0))
    full = lambda a: pl.BlockSpec(a.shape, lambda i: (0,) * a.ndim)
    return pl.pallas_call(
        _ffn_kernel,
        out_shape=jax.ShapeDtypeStruct((t, d), F32),
        grid=(t // tm,),
        in_specs=[row, mod, mod, full(wgu_bf), full(wd_bf), mod, vec, vec],
        out_specs=row,
        scratch_shapes=[pltpu.VMEM((tm, d), F32)],
        compiler_params=_params(("parallel",)),
        name="ffn_dense",
    )(x2, sc, sh, wgu_bf, wd_bf, gate, ln_g, ln_b)


def _split3(x):
    hi = x.astype(BF16)
    r = x - hi.astype(F32)
    mid = r.astype(BF16)
    lo = (r - mid.astype(F32)).astype(BF16)
    return hi, mid, lo


def _odd_in_kernel(tps, x_ref, sc_ref, sh_ref, w_ref, wf_ref, bf_ref, place_ref, q_ref, k_ref, v_ref, f_ref,
                   carry_ref):
    i = pl.program_id(0)
    tm = x_ref.shape[0]
    h = _modulate(x_ref[...], sc_ref[0], sh_ref[0])
    q = jnp.dot(h, w_ref[:, 0:FOX_WIDTH], preferred_element_type=F32)
    q_ref[...] = (q * (HEAD_DIM ** -0.5 * LOG2E)).astype(BF16)
    k_ref[...] = jnp.dot(h, w_ref[:, FOX_WIDTH:2 * FOX_WIDTH], preferred_element_type=F32).astype(BF16)
    v_ref[...] = jnp.dot(h, w_ref[:, 2 * FOX_WIDTH:3 * FOX_WIDTH], preferred_element_type=F32).astype(BF16)
    f = jnp.dot(h, wf_ref[...], preferred_element_type=F32) + bf_ref[...]
    logf = jnp.minimum(f, 0.0) - jnp.log(1.0 + jnp.exp(-jnp.abs(f)))
    r = lax.broadcasted_iota(jnp.int32, (tm, tm), 0)
    cidx = lax.broadcasted_iota(jnp.int32, (tm, tm), 1)
    tri = jnp.where(cidx <= r, 1.0, 0.0).astype(BF16)
    hi, mid, lo = _split3(logf)
    cs = (jnp.dot(tri, lo, preferred_element_type=F32) + jnp.dot(tri, mid, preferred_element_type=F32)
          + jnp.dot(tri, hi, preferred_element_type=F32))

    @pl.when(i % tps == 0)
    def _():
        carry_ref[...] = jnp.zeros_like(carry_ref)

    cs = cs + carry_ref[...]
    carry_ref[...] = cs[tm - 1:tm, :]
    parts = jnp.concatenate(_split3(cs * LOG2E), axis=1)
    f_ref[...] = jnp.dot(parts, place_ref[...], preferred_element_type=F32).astype(BF16)


def _gate_placement():
    place = np.zeros((3 * LANES, FOX_WIDTH), np.float32)
    for h in range(FOX_HEADS):
        base = (h // 2) * LANES + (HEAD_DIM if h % 2 == 0 else 0)
        for p in range(3):
            place[p * LANES + h, base + p] = 1.0
    return jnp.asarray(place, BF16)


def _odd_in(x2, sc, sh, w_bf, wf_bf, bf_pad, seq):
    t, d = x2.shape
    tm = TOKEN_TILE
    tps = seq // tm
    mod = pl.BlockSpec((1, 1, d), lambda i: (i // tps, 0, 0))
    row = lambda w: pl.BlockSpec((tm, w), lambda i: (i, 0))
    full = lambda a: pl.BlockSpec(a.shape, lambda i: (0,) * a.ndim)
    place = _gate_placement()
    return pl.pallas_call(
        functools.partial(_odd_in_kernel, tps),
        out_shape=[jax.ShapeDtypeStruct((t, FOX_WIDTH), BF16)] * 4,
        grid=(t // tm,),
        in_specs=[row(d), mod, mod, full(w_bf), full(wf_bf), full(bf_pad), full(place)],
        out_specs=[row(FOX_WIDTH)] * 4,
        scratch_shapes=[pltpu.VMEM((1, LANES), F32)],
        compiler_params=_params(("arbitrary",)),
        name="odd_in_proj",
    )(x2, sc, sh, w_bf, wf_bf, bf_pad, place)


def _fox_kernel(q_ref, k_ref, v_ref, f_ref, o_ref, acc_ref, m_ref, alpha_ref, s0_ref, s1_ref, p0_ref, p1_ref):
    qi = pl.program_id(2)
    tq = q_ref.shape[1]
    tk = FOX_K_TILE
    half = HEAD_DIM
    q = q_ref[0]
    lane_q = lax.broadcasted_iota(jnp.int32, q.shape, 1)
    lane_k = lax.broadcasted_iota(jnp.int32, (tk, LANES), 1)
    qa = jnp.where(lane_q < half, q, jnp.where(lane_q < half + 3, -1.0, 0.0).astype(BF16))
    qb = jnp.where(lane_q >= half, q, jnp.where(lane_q < 3, -1.0, 0.0).astype(BF16))
    ones_a = jnp.where(lane_k == half, 1.0, 0.0).astype(BF16)
    ones_b = jnp.where(lane_k == 0, 1.0, 0.0).astype(BF16)
    acc_ref[...] = jnp.zeros_like(acc_ref)
    m_ref[...] = jnp.full(m_ref.shape, NEG, F32)

    def scores(j, s_ref, row0=0):
        start = pl.multiple_of(j * tk, tk)
        k = k_ref[0, pl.ds(start, tk), :]
        fa = f_ref[0, pl.ds(start, tk), :]
        ka = jnp.where(lane_k < half, k, fa)
        kb = jnp.where(lane_k >= half, k, fa)
        for hh, (qh, kh) in enumerate(((qa, ka), (qb, kb))):
            s_ref[hh, row0:, :] = lax.dot_general(qh[row0:], kh, (((1,), (1,)), ((), ())),
                                                  preferred_element_type=F32)

    def softmax(j, s_ref, p_ref, row0=0, masked_rows=0):
        start = j * tk
        for hh in range(2):
            for r0 in range(row0, tq, FOX_ROW_CHUNK):
                rows = pl.ds(r0, FOX_ROW_CHUNK)
                s = s_ref[hh, rows, :]
                if r0 < row0 + masked_rows:
                    rr = lax.broadcasted_iota(jnp.int32, s.shape, 0) + (qi * tq + r0)
                    cc = lax.broadcasted_iota(jnp.int32, s.shape, 1) + start
                    s = jnp.where(cc <= rr, s, NEG)
                m_old = m_ref[hh, rows, :]
                m_new = jnp.maximum(m_old, jnp.max(s, axis=-1, keepdims=True))
                p_ref[hh, rows, :] = jnp.exp2(s - jnp.tile(m_new, (1, tk // LANES))).astype(BF16)
                alpha_ref[hh, rows, :] = jnp.exp2(m_old - m_new)
                m_ref[hh, rows, :] = m_new

    def values(j, p_ref, row0=0):
        start = pl.multiple_of(jnp.maximum(j, 0) * tk, tk)
        v = v_ref[0, pl.ds(start, tk), :]
        va = jnp.where(lane_k < half, v, ones_a)
        vb = jnp.where(lane_k >= half, v, ones_b)
        return [jnp.dot(p_ref[hh, row0:, :], vh, preferred_element_type=F32) for hh, vh in enumerate((va, vb))]

    def fold(pv, scale_row0=0):
        for hh in range(2):
            if scale_row0:
                acc_ref[hh, :scale_row0, :] += pv[hh][:scale_row0]
            acc_ref[hh, scale_row0:, :] = ((acc_ref[hh, scale_row0:, :] + pv[hh][scale_row0:])
                                           * alpha_ref[hh, scale_row0:, :])

    p1_ref[...] = jnp.zeros_like(p1_ref)
    scores(0, s0_ref)

    def body(jp, carry):
        pv = values(2 * jp - 1, p1_ref)
        scores(2 * jp + 1, s1_ref)
        softmax(2 * jp, s0_ref, p0_ref)
        fold(pv)
        pv = values(2 * jp, p0_ref)
        scores(2 * jp + 2, s0_ref)
        softmax(2 * jp + 1, s1_ref, p1_ref)
        fold(pv)
        return carry

    lax.fori_loop(0, qi, body, 0)
    pv = values(2 * qi - 1, p1_ref)
    scores(2 * qi + 1, s1_ref, row0=tk)
    softmax(2 * qi, s0_ref, p0_ref, masked_rows=tk)
    fold(pv)
    pv = values(2 * qi, p0_ref)
    softmax(2 * qi + 1, s1_ref, p1_ref, row0=tk, masked_rows=tk)
    fold(pv, scale_row0=tk)
    pv = values(2 * qi + 1, p1_ref, row0=tk)
    for hh in range(2):
        acc_ref[hh, tk:, :] += pv[hh]

    acc_a, acc_b = acc_ref[0], acc_ref[1]
    lane_o = lax.broadcasted_iota(jnp.int32, (tq, LANES), 1)
    o = jnp.where(lane_o < half, acc_a / acc_a[:, half:half + 1], acc_b / acc_b[:, 0:1])
    o_ref[0] = o.astype(BF16)


def _fox_attention(q, k, v, fa, bsz, seq):
    tq = FOX_Q_TILE
    npairs = FOX_HEADS // 2
    tile = pl.BlockSpec((1, tq, LANES), lambda b, j, i: (b, i, j))
    whole = pl.BlockSpec((1, seq, LANES), lambda b, j, i: (b, 0, j))
    return pl.pallas_call(
        _fox_kernel,
        out_shape=jax.ShapeDtypeStruct((bsz, seq, FOX_WIDTH), BF16),
        grid=(bsz, npairs, seq // tq),
        in_specs=[tile, whole, whole, whole],
        out_specs=tile,
        scratch_shapes=[pltpu.VMEM((2, tq, LANES), F32), pltpu.VMEM((2, tq, LANES), F32),
                        pltpu.VMEM((2, tq, LANES), F32),
                        pltpu.VMEM((2, tq, FOX_K_TILE), F32), pltpu.VMEM((2, tq, FOX_K_TILE), F32),
                        pltpu.VMEM((2, tq, FOX_K_TILE), BF16), pltpu.VMEM((2, tq, FOX_K_TILE), BF16)],
        compiler_params=_params(("parallel", "parallel", "arbitrary")),
        name="fox_attn",
    )(q, k, v, fa)


def _odd_out_kernel(o_ref, wo_ref, x_ref, gate_ref, g_ref, b_ref, sc_ref, sh_ref, wrt_ref, tri_ref, out_ref,
                    rankt_ref, cnt_ref, comb_ref, rank_ref):
    sub = jnp.dot(o_ref[...], wo_ref[...], preferred_element_type=F32)
    x_new = _deepnorm_ln(x_ref[...], sub, gate_ref[0], g_ref[...], b_ref[...])
    out_ref[...] = x_new
    _route_chunk(_modulate(x_new, sc_ref[0], sh_ref[0]), wrt_ref, tri_ref, rankt_ref, cnt_ref, comb_ref, rank_ref)


def _odd_out(o2, wo_bf, x2, gate, ln_g, ln_b, sc_next, sh_next, wrt, seq):
    t, d = x2.shape
    tm = MOE_CHUNK
    tps = seq // tm
    row = lambda w: pl.BlockSpec((tm, w), lambda i: (i, 0))
    full = lambda a: pl.BlockSpec(a.shape, lambda i: (0,) * a.ndim)
    mod = pl.BlockSpec((1, 1, d), lambda i: (i // tps, 0, 0))
    tri = jnp.asarray(np.triu(np.ones((tm, tm), np.float32), 1), BF16)
    return pl.pallas_call(
        _odd_out_kernel,
        out_shape=[jax.ShapeDtypeStruct((t, d), F32),
                   jax.ShapeDtypeStruct((N_EXPERTS, t), F32),
                   jax.ShapeDtypeStruct((t // tm, N_EXPERTS, LANES), F32),
                   jax.ShapeDtypeStruct((t, LANES), F32),
                   jax.ShapeDtypeStruct((t, LANES), F32)],
        grid=(t // tm,),
        in_specs=[row(FOX_WIDTH), full(wo_bf), row(d), mod, full(ln_g), full(ln_b), mod, mod, full(wrt), full(tri)],
        out_specs=[row(d),
                   pl.BlockSpec((N_EXPERTS, tm), lambda i: (0, i)),
                   pl.BlockSpec((1, N_EXPERTS, LANES), lambda i: (i, 0, 0)),
                   row(LANES),
                   row(LANES)],
        compiler_params=_params(("parallel",)),
        name="odd_out_proj",
    )(o2, wo_bf, x2, gate, ln_g, ln_b, sc_next, sh_next, wrt, tri)


def _route_chunk(h, wrt_ref, tri_ref, rankt_ref, cnt_ref, comb_ref, rank_ref):
    c = h.shape[0]
    lt = lax.dot_general(wrt_ref[...], h, (((1,), (1,)), ((), ())), preferred_element_type=F32)[0:N_EXPERTS]
    row = lax.broadcasted_iota(jnp.int32, lt.shape, 0)
    m1 = jnp.max(lt, axis=0, keepdims=True)
    i1 = jnp.min(jnp.where(lt == m1, row, N_EXPERTS), axis=0, keepdims=True)
    rest = jnp.where(row == i1, -jnp.inf, lt)
    m2 = jnp.max(rest, axis=0, keepdims=True)
    i2 = jnp.min(jnp.where(rest == m2, row, N_EXPERTS), axis=0, keepdims=True)
    e2 = jnp.exp(m2 - m1)
    g1 = 1.0 / (1.0 + e2)
    g2 = e2 / (1.0 + e2)
    combt = jnp.where(row == i1, g1, jnp.where(row == i2, g2, 0.0))
    sel = jnp.where((row == i1) | (row == i2), 1.0, 0.0)
    sel16 = jnp.concatenate([sel, jnp.zeros_like(sel)], axis=0).astype(BF16)
    earlier = jnp.dot(sel16, tri_ref[...], preferred_element_type=F32)[0:N_EXPERTS]
    rankt = jnp.where(sel > 0.0, earlier, NO_ROW)
    rankt_ref[...] = rankt
    cnt_ref[0] = jnp.broadcast_to(jnp.sum(sel, axis=1, keepdims=True), (N_EXPERTS, LANES))
    pad = LANES - N_EXPERTS
    comb_ref[...] = jnp.concatenate([combt, jnp.zeros((pad, c), F32)], axis=0).T
    rank_ref[...] = jnp.concatenate([rankt, jnp.full((pad, c), NO_ROW, F32)], axis=0).T


def _route_meta(cnt, n_tiles):
    before = jnp.cumsum(cnt, axis=0) - cnt
    total = jnp.sum(cnt, axis=0)
    region = ((total + MOE_TILE - 1) // MOE_TILE) * MOE_TILE
    base = jnp.cumsum(region) - region
    n_used = (jnp.sum(region) // MOE_TILE).reshape(1)
    starts = jnp.arange(n_tiles, dtype=jnp.int32) * MOE_TILE
    tile_expert = jnp.minimum(jnp.sum(starts[:, None] >= (base + region)[None, :], axis=1), N_EXPERTS - 1)
    i32 = lambda a: a.astype(jnp.int32)
    return i32(before.reshape(-1)), i32(cnt.reshape(-1)), i32(base), i32(total), i32(tile_expert), i32(n_used)


def _chunk_block(before_ref, cnt_ref, chunk, e, j):
    g = before_ref[chunk * N_EXPERTS + e]
    n = cnt_ref[chunk * N_EXPERTS + e]
    k = g // ROW_BLOCK + j
    return g, n, k, (n > 0) & (k * ROW_BLOCK < g + n)


def _dispatch_kernel(before_ref, cnt_ref, base_ref, total_ref, x_ref, sc_ref, sh_ref, rankt_ref, xs_ref,
                     stage_ref, slot_ref, zero_ref, sem, fsem, zsem, usem):
    c = pl.program_id(0)
    last = c == pl.num_programs(0) - 1
    h = _modulate(x_ref[...], sc_ref[0], sh_ref[0])

    @pl.when(c == 0)
    def _():
        stage_ref[...] = jnp.zeros_like(stage_ref)
        zero_ref[...] = jnp.zeros_like(zero_ref)

    block_row = lax.broadcasted_iota(jnp.int32, (ROW_BLOCK, MOE_CHUNK), 0).astype(F32)
    pair_row = lax.broadcasted_iota(jnp.int32, (2 * ROW_BLOCK, MOE_CHUNK), 0).astype(F32)

    def block_copy(e, j, k):
        row0 = pl.multiple_of(base_ref[e] + k * ROW_BLOCK, ROW_BLOCK)
        return pltpu.make_async_copy(slot_ref.at[e, j], xs_ref.at[pl.ds(row0, ROW_BLOCK)], sem.at[e, j])

    def any_beyond_two(chunk):
        flag = _chunk_block(before_ref, cnt_ref, chunk, 0, 2)[3]
        for e in range(1, N_EXPERTS):
            flag = flag | _chunk_block(before_ref, cnt_ref, chunk, e, 2)[3]
        return flag

    def for_blocks(chunk, fn):
        for e in range(N_EXPERTS):
            for j in range(2):
                fn(e, j)

        @pl.when(any_beyond_two(chunk))
        def _():
            for e in range(N_EXPERTS):
                for j in range(2, MAX_SPAN):
                    fn(e, j)

    def wait_block_copies(chunk):
        def wait_one(e, j):
            g, n, k, touched = _chunk_block(before_ref, cnt_ref, chunk, e, j)

            @pl.when(touched & (g + n >= (k + 1) * ROW_BLOCK))
            def _():
                block_copy(e, j, k).wait()

        for_blocks(chunk, wait_one)

    selectors = []
    for e in range(N_EXPERTS):
        g = before_ref[c * N_EXPERTS + e]
        off = (g % ROW_BLOCK).astype(F32)
        selectors.append(jnp.where(rankt_ref[e:e + 1, :] + off == pair_row, 1.0, 0.0).astype(BF16))
    first_two = jnp.dot(jnp.concatenate(selectors, axis=0), h, preferred_element_type=F32).astype(BF16)

    @pl.when(c > 0)
    def _():
        wait_block_copies(c - 1)

    def emit_block(e, j):
        g, n, k, touched = _chunk_block(before_ref, cnt_ref, c, e, j)
        complete = g + n >= (k + 1) * ROW_BLOCK

        @pl.when(touched)
        def _():
            if j < 2:
                r0 = (2 * e + j) * ROW_BLOCK
                blk = first_two[r0:r0 + ROW_BLOCK]
            else:
                off = (g - k * ROW_BLOCK).astype(F32)
                onehot = jnp.where(rankt_ref[e:e + 1, :] + off == block_row, 1.0, 0.0).astype(BF16)
                blk = jnp.dot(onehot, h, preferred_element_type=F32).astype(BF16)
            if j == 0:
                blk = (blk.astype(F32) + stage_ref[e].astype(F32)).astype(BF16)
            slot_ref[e, j] = blk
            stage_ref[e] = jnp.where(complete, jnp.zeros_like(blk), blk)

            @pl.when(complete)
            def _():
                block_copy(e, j, k).start()

    for_blocks(c, emit_block)

    def tail_copies(e):
        total = total_ref[e]
        n_data = (total + ROW_BLOCK - 1) // ROW_BLOCK
        n_region = ((total + MOE_TILE - 1) // MOE_TILE) * (MOE_TILE // ROW_BLOCK)
        part_row = pl.multiple_of(base_ref[e] + (total // ROW_BLOCK) * ROW_BLOCK, ROW_BLOCK)
        part = pltpu.make_async_copy(stage_ref.at[e], xs_ref.at[pl.ds(part_row, ROW_BLOCK)], fsem.at[e])
        out = [(total % ROW_BLOCK != 0, part)]
        for z in range(MOE_TILE // ROW_BLOCK - 1):
            zrow = pl.multiple_of(base_ref[e] + (n_data + z) * ROW_BLOCK, ROW_BLOCK)
            zc = pltpu.make_async_copy(zero_ref, xs_ref.at[pl.ds(zrow, ROW_BLOCK)], zsem.at[e, z])
            out.append((n_data + z < n_region, zc))
        return out

    def unused_copies():
        e = N_EXPERTS - 1
        used = base_ref[e] + ((total_ref[e] + MOE_TILE - 1) // MOE_TILE) * MOE_TILE
        out = []
        for z in range(usem.shape[0]):
            zrow = pl.multiple_of(used + z * ROW_BLOCK, ROW_BLOCK)
            zc = pltpu.make_async_copy(zero_ref, xs_ref.at[pl.ds(zrow, ROW_BLOCK)], usem.at[z])
            out.append((zrow < xs_ref.shape[0], zc))
        return out

    def all_tail_copies():
        return [cc for e in range(N_EXPERTS) for cc in tail_copies(e)] + unused_copies()

    @pl.when(last)
    def _():
        for cond, cp in all_tail_copies():
            @pl.when(cond)
            def _():
                cp.start()

    @pl.when(last)
    def _():
        wait_block_copies(c)

    @pl.when(last)
    def _():
        for cond, cp in all_tail_copies():
            @pl.when(cond)
            def _():
                cp.wait()


def _dispatch(meta, x2, sc, sh, rankt, n_tiles, seq):
    before, cnt, base, total = meta
    t, d = x2.shape
    cs = MOE_CHUNK
    cps = seq // cs
    mod = pl.BlockSpec((1, 1, d), lambda i, *_: (i // cps, 0, 0))
    grid_spec = pltpu.PrefetchScalarGridSpec(
        num_scalar_prefetch=4,
        grid=(t // cs,),
        in_specs=[pl.BlockSpec((cs, d), lambda i, *_: (i, 0)), mod, mod,
                  pl.BlockSpec((N_EXPERTS, cs), lambda i, *_: (0, i))],
        out_specs=pl.BlockSpec(memory_space=pl.ANY),
        scratch_shapes=[pltpu.VMEM((N_EXPERTS, ROW_BLOCK, d), BF16),
                        pltpu.VMEM((N_EXPERTS, MAX_SPAN, ROW_BLOCK, d), BF16),
                        pltpu.VMEM((ROW_BLOCK, d), BF16),
                        pltpu.SemaphoreType.DMA((N_EXPERTS, MAX_SPAN)),
                        pltpu.SemaphoreType.DMA((N_EXPERTS,)),
                        pltpu.SemaphoreType.DMA((N_EXPERTS, MOE_TILE // ROW_BLOCK - 1)),
                        pltpu.SemaphoreType.DMA(((n_tiles * MOE_TILE - 2 * t) // ROW_BLOCK,))],
    )
    return pl.pallas_call(
        _dispatch_kernel,
        out_shape=jax.ShapeDtypeStruct((n_tiles * MOE_TILE, d), BF16),
        grid_spec=grid_spec,
        compiler_params=_params(("arbitrary",)),
        name="moe_dispatch",
    )(before, cnt, base, total, x2, sc, sh, rankt)


def _experts_kernel(te_ref, nu_ref, x_ref, wgu_ref, wd_ref, y_ref, acc_ref):
    @pl.when(pl.program_id(0) >= nu_ref[0])
    def _():
        y_ref[...] = jnp.zeros_like(y_ref)

    @pl.when(pl.program_id(0) < nu_ref[0])
    def _():
        _swiglu_rows(x_ref[...], wgu_ref, wd_ref, acc_ref)
        y_ref[...] = acc_ref[...].astype(BF16)


def _experts(tile_expert, n_used, xs, wgu_bf, wd_bf):
    rows, d = xs.shape
    tm = MOE_TILE
    clamp = lambda i, nu: jnp.minimum(i, nu[0] - 1)
    grid_spec = pltpu.PrefetchScalarGridSpec(
        num_scalar_prefetch=2,
        grid=(rows // tm,),
        in_specs=[pl.BlockSpec((tm, d), lambda i, te, nu: (clamp(i, nu), 0)),
                  pl.BlockSpec((1, d, 2 * D_FF), lambda i, te, nu: (te[clamp(i, nu)], 0, 0)),
                  pl.BlockSpec((1, D_FF, d), lambda i, te, nu: (te[clamp(i, nu)], 0, 0))],
        out_specs=pl.BlockSpec((tm, d), lambda i, te, nu: (i, 0)),
        scratch_shapes=[pltpu.VMEM((tm, d), F32)],
    )
    return pl.pallas_call(
        _experts_kernel,
        out_shape=jax.ShapeDtypeStruct((rows, d), BF16),
        grid_spec=grid_spec,
        compiler_params=_params(("arbitrary",)),
        name="moe_experts",
    )(tile_expert, n_used, xs, wgu_bf, wd_bf)


def _combine_kernel(before_ref, cnt_ref, base_ref, rank_ref, comb_ref, x_ref, gate_ref, g_ref, b_ref, y_ref,
                    out_ref, ybuf, xbuf, acc_ref, sem, xsem):
    c = pl.program_id(0)
    slot = c % 2

    def window(chunk, e, w):
        g = before_ref[chunk * N_EXPERTS + e]
        n = cnt_ref[chunk * N_EXPERTS + e]
        first = (g // ROW_BLOCK) * ROW_BLOCK + w * COMBINE_WIN
        row0 = pl.multiple_of(jnp.minimum(base_ref[e] + first, y_ref.shape[0] - COMBINE_WIN), ROW_BLOCK)
        off = (base_ref[e] + g - row0).astype(F32)
        return (n > 0) & (first < g + n), row0, off

    def first_copy(chunk, sl, e):
        touched, row0, _ = window(chunk, e, 0)
        return touched, pltpu.make_async_copy(y_ref.at[pl.ds(row0, COMBINE_WIN)], ybuf.at[sl, e], sem.at[sl, e])

    def later_copy(chunk, sl, e, w):
        touched, row0, _ = window(chunk, e, w)
        return touched, pltpu.make_async_copy(y_ref.at[pl.ds(row0, COMBINE_WIN)], xbuf.at[sl, e, w - 1],
                                              xsem.at[sl, e, w - 1])

    def any_later(chunk):
        flag = window(chunk, 0, 1)[0]
        for e in range(1, N_EXPERTS):
            flag = flag | window(chunk, e, 1)[0]
        return flag

    def fetch(chunk, sl):
        for e in range(N_EXPERTS):
            touched, cp = first_copy(chunk, sl, e)

            @pl.when(touched)
            def _():
                cp.start()

        @pl.when(any_later(chunk))
        def _():
            for e in range(N_EXPERTS):
                for w in range(1, COMBINE_MAX_WIN):
                    touched, cp = later_copy(chunk, sl, e, w)

                    @pl.when(touched)
                    def _():
                        cp.start()

    @pl.when(c == 0)
    def _():
        ybuf[...] = jnp.zeros_like(ybuf)
        fetch(0, 0)

    @pl.when(c + 1 < pl.num_programs(0))
    def _():
        fetch(c + 1, 1 - slot)

    lane = lax.broadcasted_iota(jnp.int32, (MOE_CHUNK, COMBINE_WIN), 1).astype(F32)
    acc_ref[...] = jnp.zeros_like(acc_ref)
    for e in range(N_EXPERTS):
        col = rank_ref[:, e:e + 1]
        weight = comb_ref[:, e:e + 1]
        touched, _, off = window(c, e, 0)

        @pl.when(touched)
        def _():
            first_copy(c, slot, e)[1].wait()

        onehot = jnp.where(col + off == lane, 1.0, 0.0).astype(BF16)
        acc_ref[...] += weight * jnp.dot(onehot, ybuf[slot, e], preferred_element_type=F32)

    @pl.when(any_later(c))
    def _():
        for e in range(N_EXPERTS):
            for w in range(1, COMBINE_MAX_WIN):
                touched, _, off = window(c, e, w)

                @pl.when(touched)
                def _():
                    later_copy(c, slot, e, w)[1].wait()
                    onehot = jnp.where(rank_ref[:, e:e + 1] + off == lane, 1.0, 0.0).astype(BF16)
                    acc_ref[...] += comb_ref[:, e:e + 1] * jnp.dot(onehot, xbuf[slot, e, w - 1],
                                                                   preferred_element_type=F32)

    out_ref[...] = _deepnorm_ln(x_ref[...], acc_ref[...], gate_ref[0], g_ref[...], b_ref[...])


def _combine(meta, rank, comb, x2, gate, ln_g, ln_b, y, seq):
    before, cnt, base, _ = meta
    t, d = x2.shape
    cs = MOE_CHUNK
    cps = seq // cs
    row = lambda w: pl.BlockSpec((cs, w), lambda i, *_: (i, 0))
    vec = pl.BlockSpec((1, d), lambda i, *_: (0, 0))
    grid_spec = pltpu.PrefetchScalarGridSpec(
        num_scalar_prefetch=3,
        grid=(t // cs,),
        in_specs=[row(LANES), row(LANES), row(d), pl.BlockSpec((1, 1, d), lambda i, *_: (i // cps, 0, 0)),
                  vec, vec, pl.BlockSpec(memory_space=pl.ANY)],
        out_specs=row(d),
        scratch_shapes=[pltpu.VMEM((2, N_EXPERTS, COMBINE_WIN, d), BF16),
                        pltpu.VMEM((2, N_EXPERTS, COMBINE_MAX_WIN - 1, COMBINE_WIN, d), BF16),
                        pltpu.VMEM((cs, d), F32),
                        pltpu.SemaphoreType.DMA((2, N_EXPERTS)),
                        pltpu.SemaphoreType.DMA((2, N_EXPERTS, COMBINE_MAX_WIN - 1))],
    )
    return pl.pallas_call(
        _combine_kernel,
        out_shape=jax.ShapeDtypeStruct((t, d), F32),
        grid_spec=grid_spec,
        compiler_params=_params(("arbitrary",)),
        name="moe_combine",
    )(before, cnt, base, rank, comb, x2, gate, ln_g, ln_b, y)


def _block_diag(w_pool):
    g, c, _ = w_pool.shape
    out = jnp.zeros((g * c, g * c), w_pool.dtype)
    for i in range(g):
        out = out.at[i * c:(i + 1) * c, i * c:(i + 1) * c].set(w_pool[i])
    return out


def kernel(x, c, w_ada, b_ada, ln_g, ln_b, w_in_even, w_pool, pool_scale, w_out_even, w_ffn_gu, w_ffn_down,
           w_in_odd, b_forget, w_out_odd, w_router, w_exp_gu, w_exp_down):
    bsz, seq, d = x.shape
    t = bsz * seq
    tabs = _rope_tables(seq)
    ada = _ada(c, w_ada, b_ada)
    wgu_all = w_exp_gu.reshape((-1,) + w_exp_gu.shape[2:]).astype(BF16)
    wd_all = w_exp_down.reshape((-1,) + w_exp_down.shape[2:]).astype(BF16)
    x2 = x.reshape(t, d)
    for l in range(DEPTH):
        i = l // 2
        sh1, sc1, g1, sh2, sc2, g2 = [ada[l, :, n * d:(n + 1) * d].reshape(bsz, 1, d) for n in range(6)]
        lng = ln_g[l].reshape(2, 1, d)
        lnb = ln_b[l].reshape(2, 1, d)
        if l % 2 == 0:
            outs = _even_in(x2, sc1, sh1, w_in_even[i].astype(BF16), tabs, seq)
            u, qkv = outs[0], outs[1:]
            os_, lses = [], []
            for g in range(len(DIL_CONFIGS)):
                o, lse = _band_attention(qkv[g], qkv[3 + g], qkv[6 + g])
                os_.append(o)
                lses.append(lse)
            x2 = _even_out(u, os_, lses, _block_diag(w_pool[i]).astype(BF16), pool_scale[i].reshape(1, -1),
                           w_out_even[i].astype(BF16), x2, g1, lng[0], lnb[0], seq)
            x2 = _ffn(x2, sc2, sh2, w_ffn_gu[i][None].astype(BF16), w_ffn_down[i][None].astype(BF16), g2, lng[1],
                      lnb[1], seq)
        else:
            w_in = w_in_odd[i]
            wf = jnp.zeros((d, LANES), F32).at[:, :FOX_HEADS].set(w_in[:, 3 * FOX_WIDTH:]).astype(BF16)
            bfp = jnp.zeros((1, LANES), F32).at[0, :FOX_HEADS].set(b_forget[i])
            q, k, v, fa = [a.reshape(bsz, seq, FOX_WIDTH) for a in
                           _odd_in(x2, sc1, sh1, w_in[:, :3 * FOX_WIDTH].astype(BF16), wf, bfp, seq)]
            o = _fox_attention(q, k, v, fa, bsz, seq)
            wrt = jnp.zeros((2 * N_EXPERTS, d), F32).at[:N_EXPERTS].set(w_router[i].T).astype(BF16)
            x2, rankt, cnt, comb, rank = _odd_out(o.reshape(t, FOX_WIDTH), w_out_odd[i].astype(BF16), x2, g1, lng[0],
                                                  lnb[0], sc2, sh2, wrt, seq)
            n_tiles = (2 * t + N_EXPERTS * (MOE_TILE - 1) + MOE_TILE - 1) // MOE_TILE
            *meta, tile_expert, n_used = _route_meta(cnt[:, :, 0].astype(jnp.int32), n_tiles)
            xs = _dispatch(meta, x2, sc2, sh2, rankt, n_tiles, seq)
            y = _experts(tile_expert + i * N_EXPERTS, n_used, xs, wgu_all, wd_all)
            x2 = _combine(meta, rank, comb, x2, g2, lng[1], lnb[1], y, seq)
    return x2.reshape(bsz, seq, d)
```
